```python
import jax
import jax.numpy as jnp
from jax import lax
import numpy as np

D_MODEL = 1024
BATCH = 2
SEQ = 8192
DEPTH = 1

MOBA_HEADS = 8
MOBA_HEAD_DIM = 128
MOBA_WIDTH = MOBA_HEADS * MOBA_HEAD_DIM
MOBA_BLOCK = 256
MOBA_TOPK = 3
MOBA_QCHUNK = 32
ROPE_DIM = MOBA_HEAD_DIM // 4
ROPE_THETA = 500000.0

GLA_HEADS = 4
GLA_KEY_DIM = D_MODEL // 2
GLA_VAL_DIM = D_MODEL
GLA_DK = GLA_KEY_DIM // GLA_HEADS
GLA_DV = GLA_VAL_DIM // GLA_HEADS
GLA_GATE_RANK = 16
GLA_GATE_NORMALIZER = 16.0
GLA_CHUNK = 64

N_EXPERTS = 32
TOP_K = 4
D_EXPERT = D_MODEL
SWIGLU_ALPHA = 1.702
SWIGLU_LIMIT = 7.0
EXPERT_ROW_BLOCK = 128

RMS_EPS = 1e-5
N_BRANCHES = 2
IN_SPLITS = (MOBA_WIDTH, MOBA_WIDTH, MOBA_WIDTH, GLA_KEY_DIM, GLA_KEY_DIM, GLA_VAL_DIM, GLA_VAL_DIM, GLA_GATE_RANK, D_MODEL, D_MODEL)
N_IN = 3 * MOBA_WIDTH + 2 * GLA_KEY_DIM + 2 * GLA_VAL_DIM + GLA_GATE_RANK + N_BRANCHES * D_MODEL

kernel_name = "hybrid_moba_gla_moe_adaln_block"


def _rmsnorm(x, w):
    xf = x.astype(jnp.float32)
    y = xf * lax.rsqrt(jnp.mean(xf * xf, axis=-1, keepdims=True) + RMS_EPS)
    return (y * w.astype(jnp.float32)).astype(x.dtype)


def _partial_rope(x):
    S = x.shape[1]
    half = ROPE_DIM // 2
    inv_freq = jnp.float32(ROPE_THETA) ** (-jnp.arange(half, dtype=jnp.float32) * 2.0 / ROPE_DIM)
    ang = jnp.arange(S, dtype=jnp.float32)[:, None] * inv_freq[None, :]
    cos = jnp.cos(ang)[None, :, None, :]
    sin = jnp.sin(ang)[None, :, None, :]
    xr = x[..., :ROPE_DIM].astype(jnp.float32)
    x1, x2 = xr[..., :half], xr[..., half:]
    rot = jnp.concatenate([x1 * cos - x2 * sin, x2 * cos + x1 * sin], axis=-1).astype(x.dtype)
    return jnp.concatenate([rot, x[..., ROPE_DIM:]], axis=-1)


def _moba_attention(q, k, v):
    B, S, H, Dh = q.shape
    n_blocks = -(-S // MOBA_BLOCK)
    s_pad = n_blocks * MOBA_BLOCK
    topk = min(MOBA_TOPK, n_blocks)
    scale = Dh ** -0.5

    def prep(t):
        t = jnp.pad(t, ((0, 0), (0, s_pad - S), (0, 0), (0, 0)))
        return t.transpose(0, 2, 1, 3)

    qh, kh, vh = prep(q), prep(k), prep(v)
    kb = kh.reshape(B, H, n_blocks, MOBA_BLOCK, Dh)
    vb = vh.reshape(B, H, n_blocks, MOBA_BLOCK, Dh)
    k_mean = jnp.mean(kb.astype(jnp.float32), axis=3)
    b_idx = jnp.arange(B)[:, None, None, None]
    h_idx = jnp.arange(H)[None, :, None, None]
    block_ids = jnp.arange(n_blocks)

    def query_chunk(ci):
        start = ci * MOBA_QCHUNK
        blk = start // MOBA_BLOCK
        qc = lax.dynamic_slice_in_dim(qh, start, MOBA_QCHUNK, axis=2)
        gate = jnp.einsum('bhqd,bhnd->bhqn', qc.astype(jnp.float32), k_mean)
        gate = jnp.where(block_ids < blk, gate, -jnp.inf)
        _, sel = lax.top_k(gate, topk)
        valid = sel < blk
        k_sel = kb[b_idx, h_idx, sel]
        v_sel = vb[b_idx, h_idx, sel]
        s_sel = jnp.einsum('bhqd,bhqnkd->bhqnk', qc, k_sel).astype(jnp.float32) * scale
        s_sel = jnp.where(valid[..., None], s_sel, -jnp.inf).reshape(B, H, MOBA_QCHUNK, topk * MOBA_BLOCK)
        k_own = lax.dynamic_index_in_dim(kb, blk, axis=2, keepdims=False)
        v_own = lax.dynamic_index_in_dim(vb, blk, axis=2, keepdims=False)
        s_own = jnp.einsum('bhqd,bhkd->bhqk', qc, k_own).astype(jnp.float32) * scale
        q_pos = start + jnp.arange(MOBA_QCHUNK)
        k_pos = blk * MOBA_BLOCK + jnp.arange(MOBA_BLOCK)
        s_own = jnp.where(k_pos[None, :] <= q_pos[:, None], s_own, -jnp.inf)
        p = jax.nn.softmax(jnp.concatenate([s_sel, s_own], axis=-1), axis=-1).astype(vh.dtype)
        p_sel = p[..., :topk * MOBA_BLOCK].reshape(B, H, MOBA_QCHUNK, topk, MOBA_BLOCK)
        p_own = p[..., topk * MOBA_BLOCK:]
        return (jnp.einsum('bhqnk,bhqnkd->bhqd', p_sel, v_sel)
                + jnp.einsum('bhqk,bhkd->bhqd', p_own, v_own))

    out = lax.map(query_chunk, jnp.arange(s_pad // MOBA_QCHUNK))
    out = out.transpose(1, 0, 3, 2, 4).reshape(B, s_pad, H, Dh)
    return out[:, :S]


def _gla_chunked(q, k, v, log_g):
    B, S, H, dk = q.shape
    dv = v.shape[-1]
    n_chunks = S // GLA_CHUNK

    def to_chunks(t):
        t = t.astype(jnp.float32).reshape(B, n_chunks, GLA_CHUNK, H, t.shape[-1])
        return t.transpose(1, 0, 3, 2, 4)

    qc = to_chunks(q) * (dk ** -0.5)
    kc, vc, gc = to_chunks(k), to_chunks(v), to_chunks(log_g)
    causal = jnp.tril(jnp.ones((GLA_CHUNK, GLA_CHUNK), dtype=bool))[:, :, None]

    def step(state, inp):
        qi, ki, vi, gi = inp
        b = jnp.cumsum(gi, axis=2)
        b_last = b[:, :, -1:, :]
        o_inter = jnp.einsum('bhcd,bhdv->bhcv', qi * jnp.exp(b), state)
        diff = b[:, :, :, None, :] - b[:, :, None, :, :]
        decay = jnp.exp(jnp.where(causal, diff, -jnp.inf))
        scores = jnp.einsum('bhid,bhjd,bhijd->bhij', qi, ki, decay)
        o = o_inter + jnp.einsum('bhij,bhjv->bhiv', scores, vi)
        state = (state * jnp.exp(b_last[:, :, 0, :, None])
                 + jnp.einsum('bhcd,bhcv->bhdv', ki * jnp.exp(b_last - b), vi))
        return state, o

    state0 = jnp.zeros((B, H, dk, dv), jnp.float32)
    _, o = lax.scan(step, state0, (qc, kc, vc, gc))
    return o.transpose(1, 0, 3, 2, 4).reshape(B, S, H, dv)


def _mixer(h, w_in, gla_gate_w, gla_gate_b, gla_norm_w, w_o_moba, w_o_gla, b_merge, w_out):
    B, S, _ = h.shape
    proj = h @ w_in
    idx = []
    acc = 0
    for size in IN_SPLITS[:-1]:
        acc += size
        idx.append(acc)
    mq, mk, mv, gq, gk, gv, gr, g_low, gate_a, gate_b = jnp.split(proj, idx, axis=-1)

    def heads(t, n):
        return t.reshape(B, S, n, -1)

    mq = _partial_rope(heads(mq, MOBA_HEADS))
    mk = _partial_rope(heads(mk, MOBA_HEADS))
    o_a = _moba_attention(mq, mk, heads(mv, MOBA_HEADS)).reshape(B, S, MOBA_WIDTH)
    y_a = o_a @ w_o_moba

    log_g = jax.nn.log_sigmoid((g_low @ gla_gate_w + gla_gate_b).astype(jnp.float32)) / GLA_GATE_NORMALIZER
    o_b = _gla_chunked(heads(gq, GLA_HEADS), heads(gk, GLA_HEADS), heads(gv, GLA_HEADS), heads(log_g, GLA_HEADS))
    o_b = _rmsnorm(o_b, gla_norm_w) * jax.nn.silu(heads(gr, GLA_HEADS).astype(jnp.float32))
    y_b = o_b.reshape(B, S, GLA_VAL_DIM).astype(h.dtype) @ w_o_gla

    g_a = jax.nn.sigmoid(gate_a + b_merge[:D_MODEL])
    g_b = jax.nn.sigmoid(gate_b + b_merge[D_MODEL:])
    return (g_a * y_a + g_b * y_b) @ w_out


def _clamped_swiglu(h):
    x_glu = jnp.minimum(h[..., ::2], SWIGLU_LIMIT)
    x_lin = jnp.clip(h[..., 1::2], -SWIGLU_LIMIT, SWIGLU_LIMIT)
    return x_glu * jax.nn.sigmoid(SWIGLU_ALPHA * x_glu) * (x_lin + 1.0)


def _moe(h, w_router, b_router, w_exp_in, b_exp_in, w_exp_out, b_exp_out):
    B, S, D = h.shape
    T = B * S
    R = EXPERT_ROW_BLOCK
    tokens = h.reshape(T, D)
    logits = (tokens @ w_router + b_router).astype(jnp.float32)
    top_logits, top_idx = lax.top_k(logits, TOP_K)
    top_w = jax.nn.softmax(top_logits, axis=-1)
    flat_expert = top_idx.reshape(-1)
    flat_token = jnp.arange(T * TOP_K, dtype=jnp.int32) // TOP_K
    flat_w = top_w.reshape(-1)
    order = jnp.argsort(flat_expert)
    sorted_expert = flat_expert[order]
    counts = jax.ops.segment_sum(jnp.ones_like(flat_expert), flat_expert, num_segments=N_EXPERTS)
    group_start = jnp.cumsum(counts) - counts
    padded_counts = (counts + R - 1) // R * R
    padded_end = jnp.cumsum(padded_counts)
    padded_start = padded_end - padded_counts
    rank = jnp.arange(T * TOP_K, dtype=jnp.int32) - group_start[sorted_expert]
    dest = padded_start[sorted_expert] + rank
    n_row_blocks = -(-(T * TOP_K + N_EXPERTS * (R - 1)) // R)
    P = n_row_blocks * R
    row_token = jnp.zeros((P,), jnp.int32).at[dest].set(flat_token[order])
    row_weight = jnp.zeros((P,), jnp.float32).at[dest].set(flat_w[order])
    block_expert = jnp.minimum(
        jnp.searchsorted(padded_end, jnp.arange(n_row_blocks) * R, side='right'), N_EXPERTS - 1)

    def expert_block(j):
        rows = lax.dynamic_slice_in_dim(row_token, j * R, R)
        wts = lax.dynamic_slice_in_dim(row_weight, j * R, R)
        e = block_expert[j]
        hid = tokens[rows] @ w_exp_in[e] + b_exp_in[e]
        out = _clamped_swiglu(hid) @ w_exp_out[e] + b_exp_out[e]
        return out * wts[:, None].astype(out.dtype)

    out = lax.map(expert_block, jnp.arange(n_row_blocks)).reshape(P, D)
    y = jnp.zeros((T, D), out.dtype).at[row_token].add(out)
    return y.reshape(B, S, D)


def setup_inputs(seed: int = 0) -> dict:
    key = jax.random.key(seed)
    ks = jax.random.split(key, 21)
    L = DEPTH

    def nrm(k, shape, scale):
        return jax.random.normal(k, shape, jnp.float32) * scale

    return {
        'x': nrm(ks[0], (BATCH, SEQ, D_MODEL), 1.0),
        'c': nrm(ks[1], (BATCH, D_MODEL), 1.0),
        'w_ada': nrm(ks[2], (L, D_MODEL, 6 * D_MODEL), 0.5 * D_MODEL ** -0.5),
        'b_ada': nrm(ks[3], (L, 6 * D_MODEL), 0.01),
        'norm1_w': 1.0 + nrm(ks[4], (L, D_MODEL), 0.01),
        'w_in': nrm(ks[5], (L, D_MODEL, N_IN), D_MODEL ** -0.5),
        'gla_gate_w': nrm(ks[6], (L, GLA_GATE_RANK, GLA_KEY_DIM), GLA_GATE_RANK ** -0.5),
        'gla_gate_b': nrm(ks[7], (L, GLA_KEY_DIM), 0.01),
        'gla_norm_w': 1.0 + nrm(ks[8], (L, GLA_DV), 0.01),
        'w_o_moba': nrm(ks[9], (L, MOBA_WIDTH, D_MODEL), MOBA_WIDTH ** -0.5),
        'w_o_gla': nrm(ks[10], (L, GLA_VAL_DIM, D_MODEL), GLA_VAL_DIM ** -0.5),
        'b_merge': nrm(ks[11], (L, N_BRANCHES * D_MODEL), 0.01),
        'w_out': nrm(ks[12], (L, D_MODEL, D_MODEL), D_MODEL ** -0.5),
        'norm2_w': 1.0 + nrm(ks[13], (L, D_MODEL), 0.01),
        'w_router': nrm(ks[14], (L, D_MODEL, N_EXPERTS), D_MODEL ** -0.5),
        'b_router': nrm(ks[15], (L, N_EXPERTS), 0.01),
        'w_exp_in': nrm(ks[16], (L, N_EXPERTS, D_MODEL, 2 * D_EXPERT), D_MODEL ** -0.5),
        'b_exp_in': nrm(ks[17], (L, N_EXPERTS, 2 * D_EXPERT), 0.01),
        'w_exp_out': nrm(ks[18], (L, N_EXPERTS, D_EXPERT, D_MODEL), D_EXPERT ** -0.5),
        'b_exp_out': nrm(ks[19], (L, N_EXPERTS, D_MODEL), 0.01),
        'final_norm_w': 1.0 + nrm(ks[20], (D_MODEL,), 0.01),
    }


def reference(x, c, w_ada, b_ada, norm1_w, w_in, gla_gate_w, gla_gate_b, gla_norm_w, w_o_moba, w_o_gla, b_merge, w_out, norm2_w, w_router, b_router, w_exp_in, b_exp_in, w_exp_out, b_exp_out, final_norm_w):
    c_act = jax.nn.silu(c)
    for l in range(DEPTH):
        mod = (c_act @ w_ada[l] + b_ada[l])[:, None, :]
        shift1, scale1, gate1, shift2, scale2, gate2 = jnp.split(mod, 6, axis=-1)
        h = _rmsnorm(x, norm1_w[l]) * (1.0 + scale1) + shift1
        x = x + gate1 * _mixer(h, w_in[l], gla_gate_w[l], gla_gate_b[l], gla_norm_w[l],
                               w_o_moba[l], w_o_gla[l], b_merge[l], w_out[l])
        h = _rmsnorm(x, norm2_w[l]) * (1.0 + scale2) + shift2
        x = x + gate2 * _moe(h, w_router[l], b_router[l], w_exp_in[l], b_exp_in[l],
                             w_exp_out[l], b_exp_out[l])
    return _rmsnorm(x, final_norm_w)
```

```python
import functools

import jax
import jax.numpy as jnp
from jax import lax
from jax.experimental import pallas as pl
from jax.experimental.pallas import tpu as pltpu

F32 = jnp.float32
BF16 = jnp.bfloat16
HIGHEST = lax.Precision.HIGHEST

D_MODEL = 1024
MOBA_HEADS = 8
MOBA_HEAD_DIM = 128
MOBA_BLOCK = 256
MOBA_TOPK = 3
ROPE_DIM = MOBA_HEAD_DIM // 4
ROPE_THETA = 500000.0
GLA_HEADS = 4
GLA_KEY_DIM = D_MODEL // 2
GLA_VAL_DIM = D_MODEL
GLA_DK = GLA_KEY_DIM // GLA_HEADS
GLA_DV = GLA_VAL_DIM // GLA_HEADS
GLA_GATE_RANK = 16
GLA_GATE_NORMALIZER = 16.0
N_EXPERTS = 32
TOP_K = 4
SWIGLU_ALPHA = 1.702
SWIGLU_LIMIT = 7.0
RMS_EPS = 1e-5

LANES = 128
SUBLANES = 8
VMEM_LIMIT = 48 * 1024 * 1024

INPROJ_TM = 512
INPROJ_TN = 1664
GLA_CHUNK = 64
GLA_SUB = 16
MERGE_TM = 256
EXPERT_ROWS = 256
COMBINE_TM = 256
MASK_NEG = -1e30

N_MAIN = 3 * D_MODEL + 2 * GLA_KEY_DIM + 2 * GLA_VAL_DIM
COL_GATES = N_MAIN
COL_GLOW = N_MAIN + 2 * D_MODEL
N_PROJ = COL_GLOW + LANES


def _dot(a, b, **kw):
    return jnp.dot(a, b, preferred_element_type=F32, **kw)


def _dot_nt(a, b, **kw):
    return lax.dot_general(a, b, (((1,), (1,)), ((), ())), preferred_element_type=F32, **kw)


def _cparams(*sem):
    return pltpu.CompilerParams(dimension_semantics=sem, vmem_limit_bytes=VMEM_LIMIT)


def _mod_kernel(c_ref, w_ref, b_ref, o_ref):
    c = c_ref[...]
    ca = c * jax.nn.sigmoid(c)
    o_ref[...] = _dot(ca, w_ref[...], precision=HIGHEST) + b_ref[...]


def _modulation(c, w_ada, b_ada):
    B = c.shape[0]
    n = w_ada.shape[1]
    c8 = jnp.zeros((SUBLANES, D_MODEL), F32).at[:B].set(c)
    out = pl.pallas_call(
        _mod_kernel,
        out_shape=jax.ShapeDtypeStruct((SUBLANES, n), F32),
        grid=(n // D_MODEL,),
        in_specs=[
            pl.BlockSpec((SUBLANES, D_MODEL), lambda j: (0, 0)),
            pl.BlockSpec((D_MODEL, D_MODEL), lambda j: (0, j)),
            pl.BlockSpec((1, D_MODEL), lambda j: (0, j)),
        ],
        out_specs=pl.BlockSpec((SUBLANES, D_MODEL), lambda j: (0, j)),
        compiler_params=_cparams("arbitrary"),
        name="adaln_mod",
    )(c8, w_ada, b_ada.reshape(1, n))
    return out[:B].reshape(B, 1, n)


def _inproj_kernel(x_ref, nw_ref, sc_ref, sh_ref, w_ref, o_ref, h_ref):
    @pl.when(pl.program_id(1) == 0)
    def _():
        x = x_ref[...]
        ms = jnp.mean(x * x, axis=-1, keepdims=True)
        y = x * lax.rsqrt(ms + RMS_EPS) * nw_ref[...]
        h_ref[...] = (y * (1.0 + sc_ref[...]) + sh_ref[...]).astype(BF16)

    o_ref[...] = _dot(h_ref[...], w_ref[...])


def _input_projection(x2, norm_w, mod, w_proj, S):
    T = x2.shape[0]
    tiles_per_batch = S // INPROJ_TM
    return pl.pallas_call(
        _inproj_kernel,
        out_shape=jax.ShapeDtypeStruct((T, N_PROJ), F32),
        grid=(T // INPROJ_TM, N_PROJ // INPROJ_TN),
        in_specs=[
            pl.BlockSpec((INPROJ_TM, D_MODEL), lambda i, j: (i, 0)),
            pl.BlockSpec((1, D_MODEL), lambda i, j: (0, 0)),
            pl.BlockSpec((None, 1, D_MODEL), lambda i, j: (i // tiles_per_batch, 0, 1)),
            pl.BlockSpec((None, 1, D_MODEL), lambda i, j: (i // tiles_per_batch, 0, 0)),
            pl.BlockSpec((D_MODEL, INPROJ_TN), lambda i, j: (0, j)),
        ],
        out_specs=pl.BlockSpec((INPROJ_TM, INPROJ_TN), lambda i, j: (i, j)),
        scratch_shapes=[pltpu.VMEM((INPROJ_TM, D_MODEL), BF16)],
        compiler_params=_cparams("parallel", "arbitrary"),
        name="inproj",
    )(x2, norm_w.reshape(1, D_MODEL), mod, mod, w_proj)


def _rope_tables(S):
    half = ROPE_DIM // 2
    inv_freq = jnp.float32(ROPE_THETA) ** (-jnp.arange(half, dtype=jnp.float32) * 2.0 / ROPE_DIM)
    ang = jnp.arange(S, dtype=jnp.float32)[:, None] * inv_freq[None, :]
    cos, sin = jnp.cos(ang), jnp.sin(ang)
    ones = jnp.ones((S, MOBA_HEAD_DIM - ROPE_DIM), F32)
    zeros_hi = jnp.zeros((S, MOBA_HEAD_DIM - half), F32)
    zeros_lo = jnp.zeros((S, half), F32)
    cos_f = jnp.concatenate([cos, cos, ones], axis=1)
    sin_a = jnp.concatenate([-sin, zeros_hi], axis=1)
    sin_b = jnp.concatenate([zeros_lo, sin, ones * 0.0], axis=1)
    return cos_f, sin_a, sin_b


def _rope(x, cos_f, sin_a, sin_b):
    half = ROPE_DIM // 2
    up = pltpu.roll(x, MOBA_HEAD_DIM - half, 1)
    dn = pltpu.roll(x, half, 1)
    return x * cos_f + up * sin_a + dn * sin_b


def _kprep_kernel(k_ref, v_ref, cos_ref, sa_ref, sb_ref, kaug_ref, vb_ref, kmean_ref):
    n = pl.program_id(1)
    cos_f, sin_a, sin_b = cos_ref[...], sa_ref[...], sb_ref[...]
    lane = lax.broadcasted_iota(jnp.int32, (MOBA_BLOCK, LANES), 1)
    onehot = jnp.where(lane == n, 1.0, 0.0).astype(BF16)
    for h in range(MOBA_HEADS):
        cols = slice(h * MOBA_HEAD_DIM, (h + 1) * MOBA_HEAD_DIM)
        kr = _rope(k_ref[:, cols], cos_f, sin_a, sin_b)
        kmean_ref[:, cols] = jnp.mean(kr, axis=0, keepdims=True)
        kaug_ref[h, :, :MOBA_HEAD_DIM] = kr.astype(BF16)
        kaug_ref[h, :, MOBA_HEAD_DIM:] = onehot
        vb_ref[h] = v_ref[:, cols].astype(BF16)


def _moba_prep(proj, tables, B, S):
    NB = S // MOBA_BLOCK
    tab_spec = pl.BlockSpec((MOBA_BLOCK, LANES), lambda b, n: (n, 0))
    return pl.pallas_call(
        _kprep_kernel,
        out_shape=(
            jax.ShapeDtypeStruct((B, MOBA_HEADS, S, 2 * MOBA_HEAD_DIM), BF16),
            jax.ShapeDtypeStruct((B, MOBA_HEADS, S, MOBA_HEAD_DIM), BF16),
            jax.ShapeDtypeStruct((B, NB, 1, D_MODEL), F32),
        ),
        grid=(B, NB),
        in_specs=[
            pl.BlockSpec((MOBA_BLOCK, D_MODEL), lambda b, n: (b * NB + n, 1)),
            pl.BlockSpec((MOBA_BLOCK, D_MODEL), lambda b, n: (b * NB + n, 2)),
            tab_spec, tab_spec, tab_spec,
        ],
        out_specs=(
            pl.BlockSpec((None, MOBA_HEADS, MOBA_BLOCK, 2 * MOBA_HEAD_DIM), lambda b, n: (b, 0, n, 0)),
            pl.BlockSpec((None, MOBA_HEADS, MOBA_BLOCK, MOBA_HEAD_DIM), lambda b, n: (b, 0, n, 0)),
            pl.BlockSpec((None, None, 1, D_MODEL), lambda b, n: (b, n, 0, 0)),
        ),
        compiler_params=_cparams("parallel", "parallel"),
        name="moba_kprep",
    )(proj, proj, *tables)


def _moba_kernel(q_ref, cos_ref, sa_ref, sb_ref, kmean_ref, kaug_ref, v_ref, o_ref,
                 m_ref, l_ref, acc_ref):
    i = pl.program_id(2)
    scale = MOBA_HEAD_DIM ** -0.5
    blk = MOBA_BLOCK
    qr = _rope(q_ref[...], cos_ref[...], sa_ref[...], sb_ref[...])

    lane = lax.broadcasted_iota(jnp.int32, (blk, LANES), 1)
    gate = _dot_nt(qr, kmean_ref[...], precision=HIGHEST)
    neg_inf = jnp.float32(-jnp.inf)
    g = jnp.where(lane < i, gate, neg_inf)
    bias = jnp.where(lane == i, 0.0, MASK_NEG)
    for _ in range(MOBA_TOPK):
        mx = jnp.max(g, axis=1, keepdims=True)
        idx = jnp.min(jnp.where(g == mx, lane, LANES), axis=1, keepdims=True)
        pick = lane == idx
        bias = jnp.where(pick, jnp.where(mx > neg_inf, 0.0, bias), bias)
        g = jnp.where(pick, neg_inf, g)
    q_aug = jnp.concatenate([qr.astype(BF16), bias.astype(BF16)], axis=1)

    def scores(n):
        start = pl.multiple_of(n * blk, blk)
        s = _dot_nt(q_aug, kaug_ref[pl.ds(start, blk), :]) * scale
        return s, v_ref[pl.ds(start, blk), :]

    s, vb = scores(i)
    row = lax.broadcasted_iota(jnp.int32, (blk, blk), 0)
    col = lax.broadcasted_iota(jnp.int32, (blk, blk), 1)
    s = jnp.where(col <= row, s, neg_inf)
    m0 = jnp.max(s, axis=1, keepdims=True)
    p = jnp.exp(s - m0)
    m_ref[...] = jnp.broadcast_to(m0, (blk, LANES))
    l_ref[...] = jnp.broadcast_to(jnp.sum(p, axis=1, keepdims=True), (blk, LANES))
    acc_ref[...] = _dot(p.astype(BF16), vb)

    def body(n, carry):
        s, vb = scores(n)
        s0, s1 = s[:, :LANES], s[:, LANES:]
        m_prev = m_ref[...]
        m_new = jnp.maximum(m_prev, jnp.max(jnp.maximum(s0, s1), axis=1, keepdims=True))
        alpha = jnp.exp(m_prev - m_new)
        p0 = jnp.exp(s0 - m_new)
        p1 = jnp.exp(s1 - m_new)
        l_ref[...] = alpha * l_ref[...] + jnp.sum(p0 + p1, axis=1, keepdims=True)
        m_ref[...] = m_new
        pb = jnp.concatenate([p0, p1], axis=1).astype(BF16)
        acc_ref[...] = alpha * acc_ref[...] + _dot(pb, vb)
        return carry

    lax.fori_loop(0, i, body, 0)
    o_ref[...] = (acc_ref[...] / l_ref[...]).astype(o_ref.dtype)


def _moba_attention(proj, tables, kaug, vb, kmean, B, S):
    NB = S // MOBA_BLOCK
    T = B * S
    tab_spec = pl.BlockSpec((MOBA_BLOCK, LANES), lambda b, h, i: (i, 0))
    return pl.pallas_call(
        _moba_kernel,
        out_shape=jax.ShapeDtypeStruct((T, D_MODEL), BF16),
        grid=(B, MOBA_HEADS, NB),
        in_specs=[
            pl.BlockSpec((MOBA_BLOCK, MOBA_HEAD_DIM), lambda b, h, i: (b * NB + i, h)),
            tab_spec, tab_spec, tab_spec,
            pl.BlockSpec((None, None, LANES, MOBA_HEAD_DIM), lambda b, h, i: (b, h, 0, 0)),
            pl.BlockSpec((None, None, S, 2 * MOBA_HEAD_DIM), lambda b, h, i: (b, h, 0, 0)),
            pl.BlockSpec((None, None, S, MOBA_HEAD_DIM), lambda b, h, i: (b, h, 0, 0)),
        ],
        out_specs=pl.BlockSpec((MOBA_BLOCK, MOBA_HEAD_DIM), lambda b, h, i: (b * NB + i, h)),
        scratch_shapes=[
            pltpu.VMEM((MOBA_BLOCK, LANES), F32),
            pltpu.VMEM((MOBA_BLOCK, LANES), F32),
            pltpu.VMEM((MOBA_BLOCK, MOBA_HEAD_DIM), F32),
        ],
        compiler_params=_cparams("parallel", "parallel", "arbitrary"),
        name="moba_attn",
    )(proj, *tables, kmean, kaug, vb)


def _gla_kernel(q_ref, k_ref, v_ref, gr_ref, gl_ref, gw_ref, gb_ref, nw_ref, o_ref, state_ref):
    C, SUB = GLA_CHUNK, GLA_SUB
    nsub = C // SUB

    @pl.when(pl.program_id(1) == 0)
    def _():
        state_ref[...] = jnp.zeros_like(state_ref)

    z = _dot(gl_ref[...].astype(BF16), gw_ref[...].astype(BF16)) + gb_ref[...]
    log_g = jax.nn.log_sigmoid(z) / GLA_GATE_NORMALIZER
    r_i = lax.broadcasted_iota(jnp.int32, (C, C), 0)
    c_i = lax.broadcasted_iota(jnp.int32, (C, C), 1)
    tril = jnp.where(c_i <= r_i, 1.0, 0.0).astype(F32)
    b_all = _dot(tril, log_g, precision=HIGHEST)

    row_c = lax.broadcasted_iota(jnp.int32, (C, GLA_DK), 0)
    row_s = lax.broadcasted_iota(jnp.int32, (SUB, GLA_DK), 0)
    lane_s = lax.broadcasted_iota(jnp.int32, (SUB, C), 1)
    neg_inf = jnp.float32(-jnp.inf)
    nw = nw_ref[...]

    for h in range(GLA_HEADS):
        kc = slice(h * GLA_DK, (h + 1) * GLA_DK)
        vc = slice(h * GLA_DV, (h + 1) * GLA_DV)
        b = b_all[:, kc]
        q = q_ref[:, kc] * (GLA_DK ** -0.5)
        k = k_ref[:, kc]
        v = v_ref[:, vc]
        v16 = v.astype(BF16)
        st = state_ref[h]
        b_last = b[C - 1:C, :]

        o = _dot_nt((q * jnp.exp(b)).astype(BF16), st.astype(BF16))

        s_rows = []
        for I in range(nsub):
            rs = slice(I * SUB, (I + 1) * SUB)
            q_i, k_i, b_i = q[rs], k[rs], b[rs]
            if I == 0:
                s_i = jnp.zeros((SUB, C), F32)
            else:
                ref_b = b[I * SUB - 1:I * SUB, :]
                qs = q_i * jnp.exp(b_i - ref_b)
                ks = k * jnp.exp(jnp.where(row_c < I * SUB, ref_b - b, neg_inf))
                s_i = _dot_nt(qs.astype(BF16), ks.astype(BF16))
            for j in range(SUB):
                diff = jnp.where(row_s >= j, b_i - b_i[j:j + 1, :], neg_inf)
                colv = jnp.sum(q_i * k_i[j:j + 1, :] * jnp.exp(diff), axis=1, keepdims=True)
                s_i = jnp.where(lane_s == I * SUB + j, colv, s_i)
            s_rows.append(s_i)
        s_full = jnp.concatenate(s_rows, axis=0)
        o = o + _dot(s_full.astype(BF16), v16)

        kd = k * jnp.exp(b_last - b)
        state_ref[h] = st * jnp.exp(b_last) + _dot(v.T.astype(BF16), kd.astype(BF16))

        ms = jnp.mean(o * o, axis=-1, keepdims=True)
        y = o * lax.rsqrt(ms + RMS_EPS) * nw
        g = gr_ref[:, vc]
        o_ref[:, vc] = (y * (g * jax.nn.sigmoid(g))).astype(o_ref.dtype)


def _gla(proj, gate_w, gate_b, norm_w, B, S):
    T = B * S
    NC = S // GLA_CHUNK
    gw = jnp.zeros((LANES, GLA_KEY_DIM), F32).at[:GLA_GATE_RANK].set(gate_w)
    q_blk = (3 * D_MODEL) // GLA_KEY_DIM
    v_blk = (3 * D_MODEL + 2 * GLA_KEY_DIM) // GLA_VAL_DIM
    return pl.pallas_call(
        _gla_kernel,
        out_shape=jax.ShapeDtypeStruct((T, GLA_VAL_DIM), BF16),
        grid=(B, NC),
        in_specs=[
            pl.BlockSpec((GLA_CHUNK, GLA_KEY_DIM), lambda b, c: (b * NC + c, q_blk)),
            pl.BlockSpec((GLA_CHUNK, GLA_KEY_DIM), lambda b, c: (b * NC + c, q_blk + 1)),
            pl.BlockSpec((GLA_CHUNK, GLA_VAL_DIM), lambda b, c: (b * NC + c, v_blk)),
            pl.BlockSpec((GLA_CHUNK, GLA_VAL_DIM), lambda b, c: (b * NC + c, v_blk + 1)),
            pl.BlockSpec((GLA_CHUNK, LANES), lambda b, c: (b * NC + c, COL_GLOW // LANES)),
            pl.BlockSpec((LANES, GLA_KEY_DIM), lambda b, c: (0, 0)),
            pl.BlockSpec((1, GLA_KEY_DIM), lambda b, c: (0, 0)),
            pl.BlockSpec((1, GLA_DV), lambda b, c: (0, 0)),
        ],
        out_specs=pl.BlockSpec((GLA_CHUNK, GLA_VAL_DIM), lambda b, c: (b * NC + c, 0)),
        scratch_shapes=[pltpu.VMEM((GLA_HEADS, GLA_DV, GLA_DK), F32)],
        compiler_params=_cparams("parallel", "arbitrary"),
        name="gla",
    )(proj, proj, proj, proj, proj, gw, gate_b.reshape(1, GLA_KEY_DIM), norm_w.reshape(1, GLA_DV))


def _merge_kernel(oa_ref, ob_ref, ga_ref, gb_ref, x_ref, wa_ref, wb_ref, wo_ref, bma_ref, bmb_ref,
                  g1_ref, sc2_ref, sh2_ref, nw_ref, wr_ref, br_ref,
                  x1_ref, h2_ref, e4_ref, w4_ref, p4_ref, cnt_ref, run_ref):
    tm = MERGE_TM

    @pl.when(pl.program_id(0) == 0)
    def _():
        run_ref[...] = jnp.zeros_like(run_ref)

    y_a = _dot(oa_ref[...], wa_ref[...])
    y_b = _dot(ob_ref[...], wb_ref[...])
    g_a = jax.nn.sigmoid(ga_ref[...] + bma_ref[...])
    g_b = jax.nn.sigmoid(gb_ref[...] + bmb_ref[...])
    mix = _dot((g_a * y_a + g_b * y_b).astype(BF16), wo_ref[...])
    x1 = x_ref[...] + g1_ref[...] * mix
    x1_ref[...] = x1

    ms = jnp.mean(x1 * x1, axis=-1, keepdims=True)
    h2 = x1 * lax.rsqrt(ms + RMS_EPS) * nw_ref[...] * (1.0 + sc2_ref[...]) + sh2_ref[...]
    for s in range(D_MODEL // LANES):
        h2_ref[pl.ds(s, tm, stride=SUBLANES), :] = h2[:, s * LANES:(s + 1) * LANES]

    logits = _dot(h2, wr_ref[...], precision=HIGHEST) + br_ref[...]
    lane = lax.broadcasted_iota(jnp.int32, (tm, LANES), 1)
    neg_inf = jnp.float32(-jnp.inf)
    g = jnp.where(lane < N_EXPERTS, logits, neg_inf)
    picks, tops = [], []
    for _ in range(TOP_K):
        mx = jnp.max(g, axis=1, keepdims=True)
        idx = jnp.min(jnp.where(g == mx, lane, LANES), axis=1, keepdims=True)
        pick = lane == idx
        picks.append((pick, idx))
        tops.append(mx)
        g = jnp.where(pick, neg_inf, g)
    ex = [jnp.exp(t - tops[0]) for t in tops]
    denom = ex[0] + ex[1] + ex[2] + ex[3]

    sel = sum(jnp.where(pick, 1.0, 0.0) for pick, _ in picks)
    r_i = lax.broadcasted_iota(jnp.int32, (tm, tm), 0)
    c_i = lax.broadcasted_iota(jnp.int32, (tm, tm), 1)
    lower = jnp.where(c_i < r_i, 1.0, 0.0).astype(BF16)
    rank = run_ref[0:1, :] + _dot(lower, sel.astype(BF16))
    run_new = run_ref[0:1, :] + jnp.sum(sel, axis=0, keepdims=True)
    run_ref[...] = jnp.broadcast_to(run_new, run_ref.shape)
    cnt_ref[...] = jnp.broadcast_to(run_new, cnt_ref.shape)

    e4 = jnp.zeros((tm, LANES), jnp.int32)
    w4 = jnp.zeros((tm, LANES), F32)
    p4 = jnp.zeros((tm, LANES), F32)
    for r in range(TOP_K):
        pick, idx = picks[r]
        pos = jnp.sum(jnp.where(pick, rank, 0.0), axis=1, keepdims=True)
        e4 = jnp.where(lane == r, idx, e4)
        w4 = jnp.where(lane == r, ex[r] / denom, w4)
        p4 = jnp.where(lane == r, pos, p4)
    e4_ref[...] = e4
    w4_ref[...] = w4
    p4_ref[...] = p4


def _merge_and_route(o_a, o_b, proj, x2, w_a, w_b, w_o, b_merge, mod, norm2_w, w_router, b_router, S):
    T = x2.shape[0]
    tm = MERGE_TM
    tiles_per_batch = S // tm
    full = lambda shape: pl.BlockSpec(shape, lambda i: tuple(0 for _ in shape))
    row = lambda width, col: pl.BlockSpec((tm, width), lambda i: (i, col))
    modv = lambda k: pl.BlockSpec((None, 1, D_MODEL), lambda i: (i // tiles_per_batch, 0, k))
    wr = jnp.zeros((D_MODEL, LANES), F32).at[:, :N_EXPERTS].set(w_router)
    br = jnp.zeros((1, LANES), F32).at[0, :N_EXPERTS].set(b_router)
    bm = b_merge.reshape(1, 2 * D_MODEL)
    return pl.pallas_call(
        _merge_kernel,
        out_shape=(
            jax.ShapeDtypeStruct((T, D_MODEL), F32),
            jax.ShapeDtypeStruct((T * SUBLANES, LANES), F32),
            jax.ShapeDtypeStruct((T, LANES), jnp.int32),
            jax.ShapeDtypeStruct((T, LANES), F32),
            jax.ShapeDtypeStruct((T, LANES), F32),
            jax.ShapeDtypeStruct((SUBLANES, LANES), F32),
        ),
        grid=(T // tm,),
        in_specs=[
            row(D_MODEL, 0), row(D_MODEL, 0),
            row(D_MODEL, COL_GATES // D_MODEL), row(D_MODEL, COL_GATES // D_MODEL + 1),
            row(D_MODEL, 0),
            full((D_MODEL, D_MODEL)), full((D_MODEL, D_MODEL)), full((D_MODEL, D_MODEL)),
            pl.BlockSpec((1, D_MODEL), lambda i: (0, 0)), pl.BlockSpec((1, D_MODEL), lambda i: (0, 1)),
            modv(2), modv(4), modv(3),
            full((1, D_MODEL)), full((D_MODEL, LANES)), full((1, LANES)),
        ],
        out_specs=(
            row(D_MODEL, 0),
            pl.BlockSpec((tm * SUBLANES, LANES), lambda i: (i, 0)),
            row(LANES, 0), row(LANES, 0), row(LANES, 0),
            full((SUBLANES, LANES)),
        ),
        scratch_shapes=[pltpu.VMEM((SUBLANES, LANES), F32)],
        compiler_params=_cparams("arbitrary"),
        name="merge_route",
    )(o_a, o_b, proj, proj, x2, w_a, w_b, w_o, bm, bm, mod, mod, mod,
      norm2_w.reshape(1, D_MODEL), wr, br)


def _dispatch_kernel(dest_ref, h_ref, xs_in_ref, xs_ref, sem):
    del xs_in_ref
    tm = COMBINE_TM
    base = pl.program_id(0) * (tm * TOP_K)

    def copy(a):
        t = a // TOP_K
        d = dest_ref[base + a]
        return pltpu.make_async_copy(
            h_ref.at[pl.ds(pl.multiple_of(t * SUBLANES, SUBLANES), SUBLANES), :],
            xs_ref.at[pl.ds(pl.multiple_of(d * SUBLANES, SUBLANES), SUBLANES), :],
            sem)

    def start(a, c):
        copy(a).start()
        return c

    def wait(a, c):
        copy(a).wait()
        return c

    lax.fori_loop(0, tm * TOP_K, start, 0)
    lax.fori_loop(0, tm * TOP_K, wait, 0)


def _dispatch(dest, h2_slabs, P):
    T = h2_slabs.shape[0] // SUBLANES
    tm = COMBINE_TM
    xs0 = jnp.zeros((P * SUBLANES, LANES), F32)
    return pl.pallas_call(
        _dispatch_kernel,
        out_shape=jax.ShapeDtypeStruct((P * SUBLANES, LANES), F32),
        grid_spec=pltpu.PrefetchScalarGridSpec(
            num_scalar_prefetch=1,
            grid=(T // tm,),
            in_specs=[
                pl.BlockSpec((tm * SUBLANES, LANES), lambda i, dest: (i, 0)),
                pl.BlockSpec(memory_space=pl.ANY),
            ],
            out_specs=pl.BlockSpec(memory_space=pl.ANY),
            scratch_shapes=[pltpu.SemaphoreType.DMA],
        ),
        input_output_aliases={2: 0},
        compiler_params=_cparams("arbitrary"),
        name="moe_dispatch",
    )(dest, h2_slabs, xs0)


def _expert_kernel(be_ref, nu_ref, xs_ref, wi_ref, bi_ref, wo_ref, bo_ref, ys_ref, x_ref):
    del be_ref
    R = EXPERT_ROWS
    j = pl.program_id(0)

    @pl.when(j < nu_ref[0])
    def _():
        for s in range(D_MODEL // LANES):
            x_ref[:, s * LANES:(s + 1) * LANES] = xs_ref[pl.ds(s, R, stride=SUBLANES), :].astype(BF16)
        hid = _dot(x_ref[...], wi_ref[...]) + bi_ref[...]
        glu = jnp.minimum(hid[:, :D_MODEL], SWIGLU_LIMIT)
        lin = jnp.clip(hid[:, D_MODEL:], -SWIGLU_LIMIT, SWIGLU_LIMIT)
        act = glu * jax.nn.sigmoid(SWIGLU_ALPHA * glu) * (lin + 1.0)
        out = _dot(act.astype(BF16), wo_ref[...]) + bo_ref[...]
        for s in range(D_MODEL // LANES):
            ys_ref[pl.ds(s, R, stride=SUBLANES), :] = out[:, s * LANES:(s + 1) * LANES]

    @pl.when(j >= nu_ref[0])
    def _():
        ys_ref[...] = jnp.zeros_like(ys_ref)


def _experts(block_expert, n_used, xs, w_in, b_in, w_out, b_out):
    R = EXPERT_ROWS
    n_blocks = xs.shape[0] // (R * SUBLANES)
    return pl.pallas_call(
        _expert_kernel,
        out_shape=jax.ShapeDtypeStruct(xs.shape, F32),
        grid_spec=pltpu.PrefetchScalarGridSpec(
            num_scalar_prefetch=2,
            grid=(n_blocks,),
            in_specs=[
                pl.BlockSpec((R * SUBLANES, LANES), lambda j, be, nu: (jnp.minimum(j, nu[0] - 1), 0)),
                pl.BlockSpec((None, D_MODEL, 2 * D_MODEL), lambda j, be, nu: (be[j], 0, 0)),
                pl.BlockSpec((None, 1, 2 * D_MODEL), lambda j, be, nu: (be[j], 0, 0)),
                pl.BlockSpec((None, D_MODEL, D_MODEL), lambda j, be, nu: (be[j], 0, 0)),
                pl.BlockSpec((None, 1, D_MODEL), lambda j, be, nu: (be[j], 0, 0)),
            ],
            out_specs=pl.BlockSpec((R * SUBLANES, LANES), lambda j, be, nu: (j, 0)),
            scratch_shapes=[pltpu.VMEM((R, D_MODEL), BF16)],
        ),
        compiler_params=_cparams("arbitrary"),
        name="moe_experts",
    )(block_expert, n_used, xs, w_in, b_in, w_out, b_out)


def _combine_kernel(dest_ref, ys_ref, w4_ref, x1_ref, g2_ref, nw_ref, o_ref, buf_ref, sem):
    tm = COMBINE_TM
    base = pl.program_id(0) * (tm * TOP_K)

    def copy(a):
        t = a // TOP_K
        kk = a % TOP_K
        d = dest_ref[base + a]
        return pltpu.make_async_copy(
            ys_ref.at[pl.ds(pl.multiple_of(d * SUBLANES, SUBLANES), SUBLANES), :],
            buf_ref.at[kk, pl.ds(pl.multiple_of(t * SUBLANES, SUBLANES), SUBLANES), :],
            sem)

    def start(a, c):
        copy(a).start()
        return c

    def wait(a, c):
        copy(a).wait()
        return c

    lax.fori_loop(0, tm * TOP_K, start, 0)
    lax.fori_loop(0, tm * TOP_K, wait, 0)

    w4 = w4_ref[...]
    g2 = g2_ref[...]
    parts = []
    ssq = jnp.zeros((tm, 1), F32)
    for s in range(D_MODEL // LANES):
        cols = slice(s * LANES, (s + 1) * LANES)
        y = jnp.zeros((tm, LANES), F32)
        for kk in range(TOP_K):
            y = y + buf_ref[kk, pl.ds(s, tm, stride=SUBLANES), :] * w4[:, kk:kk + 1]
        x2 = x1_ref[:, cols] + g2[:, cols] * y
        ssq = ssq + jnp.sum(x2 * x2, axis=1, keepdims=True)
        parts.append(x2)
    inv = lax.rsqrt(ssq / D_MODEL + RMS_EPS)
    nw = nw_ref[...]
    for s in range(D_MODEL // LANES):
        cols = slice(s * LANES, (s + 1) * LANES)
        o_ref[:, cols] = parts[s] * inv * nw[:, cols]


def _combine(dest, ys, w4, x1, mod, final_w, S):
    T = x1.shape[0]
    tm = COMBINE_TM
    tiles_per_batch = S // tm
    return pl.pallas_call(
        _combine_kernel,
        out_shape=jax.ShapeDtypeStruct((T, D_MODEL), F32),
        grid_spec=pltpu.PrefetchScalarGridSpec(
            num_scalar_prefetch=1,
            grid=(T // tm,),
            in_specs=[
                pl.BlockSpec(memory_space=pl.ANY),
                pl.BlockSpec((tm, LANES), lambda i, dest: (i, 0)),
                pl.BlockSpec((tm, D_MODEL), lambda i, dest: (i, 0)),
                pl.BlockSpec((None, 1, D_MODEL), lambda i, dest: (i // tiles_per_batch, 0, 5)),
                pl.BlockSpec((1, D_MODEL), lambda i, dest: (0, 0)),
            ],
            out_specs=pl.BlockSpec((tm, D_MODEL), lambda i, dest: (i, 0)),
            scratch_shapes=[
                pltpu.VMEM((TOP_K, tm * SUBLANES, LANES), F32),
                pltpu.SemaphoreType.DMA,
            ],
        ),
        compiler_params=_cparams("arbitrary"),
        name="moe_combine",
    )(dest, ys, w4, x1, mod, final_w.reshape(1, D_MODEL))


def _layer(x2, c_mod, B, S, norm1_w, w_in, gla_gate_w, gla_gate_b, gla_norm_w, w_o_moba, w_o_gla,
           b_merge, w_out, norm2_w, w_router, b_router, w_exp_in, b_exp_in, w_exp_out, b_exp_out):
    T = B * S
    glow_lo = N_MAIN
    glow_hi = N_MAIN + GLA_GATE_RANK
    w_proj = jnp.concatenate(
        [w_in[:, :glow_lo], w_in[:, glow_hi:], w_in[:, glow_lo:glow_hi],
         jnp.zeros((D_MODEL, LANES - GLA_GATE_RANK), F32)], axis=1).astype(BF16)
    proj = _input_projection(x2, norm1_w, c_mod, w_proj, S)

    tables = _rope_tables(S)
    kaug, vb, kmean = _moba_prep(proj, tables, B, S)
    NB = S // MOBA_BLOCK
    kmean = kmean.reshape(B, NB, MOBA_HEADS, MOBA_HEAD_DIM).transpose(0, 2, 1, 3)
    kmean = jnp.pad(kmean, ((0, 0), (0, 0), (0, LANES - NB), (0, 0)))
    o_a = _moba_attention(proj, tables, kaug, vb, kmean, B, S)
    o_b = _gla(proj, gla_gate_w, gla_gate_b, gla_norm_w, B, S)

    x1, h2, e4, w4, p4, cnt = _merge_and_route(
        o_a, o_b, proj, x2, w_o_moba.astype(BF16), w_o_gla.astype(BF16), w_out.astype(BF16),
        b_merge, c_mod, norm2_w, w_router, b_router, S)

    R = EXPERT_ROWS
    counts = cnt[0, :N_EXPERTS].astype(jnp.int32)
    padded = (counts + R - 1) // R * R
    padded_end = jnp.cumsum(padded)
    padded_start = padded_end - padded
    n_blocks = -(-(T * TOP_K + N_EXPERTS * (R - 1)) // R)
    P = n_blocks * R
    dest = (padded_start[e4[:, :TOP_K]] + p4[:, :TOP_K].astype(jnp.int32)).reshape(-1)
    block_expert = jnp.minimum(
        jnp.searchsorted(padded_end, jnp.arange(n_blocks, dtype=jnp.int32) * R, side='right'),
        N_EXPERTS - 1).astype(jnp.int32)
    n_used = (padded_end[-1:] // R).astype(jnp.int32)

    xs = _dispatch(dest, h2, P)
    w_ei = jnp.concatenate([w_exp_in[:, :, 0::2], w_exp_in[:, :, 1::2]], axis=-1).astype(BF16)
    b_ei = jnp.concatenate([b_exp_in[:, 0::2], b_exp_in[:, 1::2]], axis=-1)[:, None, :]
    ys = _experts(block_expert, n_used, xs, w_ei, b_ei, w_exp_out.astype(BF16), b_exp_out[:, None, :])
    return x1, ys, dest, w4


def kernel(x, c, w_ada, b_ada, norm1_w, w_in, gla_gate_w, gla_gate_b, gla_norm_w, w_o_moba, w_o_gla, b_merge, w_out, norm2_w, w_router, b_router, w_exp_in, b_exp_in, w_exp_out, b_exp_out, final_norm_w):
    B, S, _ = x.shape
    depth = w_ada.shape[0]
    assert depth == 1, "the combine kernel fuses the final norm, so a single layer is supported"
    x2 = x.reshape(B * S, D_MODEL)
    l = 0
    mod = _modulation(c, w_ada[l], b_ada[l])
    x1, ys, dest, w4 = _layer(
        x2, mod, B, S, norm1_w[l], w_in[l], gla_gate_w[l], gla_gate_b[l], gla_norm_w[l],
        w_o_moba[l], w_o_gla[l], b_merge[l], w_out[l], norm2_w[l], w_router[l], b_router[l],
        w_exp_in[l], b_exp_in[l], w_exp_out[l], b_exp_out[l])
    out = _combine(dest, ys, w4, x1, mod, final_norm_w, S)
    return out.reshape(B, S, D_MODEL)
```

```python
import functools

import jax
import jax.numpy as jnp
from jax import lax
from jax.experimental import pallas as pl
from jax.experimental.pallas import tpu as pltpu

F32 = jnp.float32
BF16 = jnp.bfloat16
HIGHEST = lax.Precision.HIGHEST

D_MODEL = 1024
MOBA_HEADS = 8
MOBA_HEAD_DIM = 128
MOBA_BLOCK = 256
MOBA_TOPK = 3
ROPE_DIM = MOBA_HEAD_DIM // 4
ROPE_THETA = 500000.0
GLA_HEADS = 4
GLA_KEY_DIM = D_MODEL // 2
GLA_VAL_DIM = D_MODEL
GLA_DK = GLA_KEY_DIM // GLA_HEADS
GLA_DV = GLA_VAL_DIM // GLA_HEADS
GLA_GATE_RANK = 16
GLA_GATE_NORMALIZER = 16.0
N_EXPERTS = 32
TOP_K = 4
SWIGLU_ALPHA = 1.702
SWIGLU_LIMIT = 7.0
RMS_EPS = 1e-5

LANES = 128
SUBLANES = 8
VMEM_LIMIT = 48 * 1024 * 1024
EXPERT_VMEM_LIMIT = 56 * 1024 * 1024

INPROJ_TM = 512
INPROJ_TN = 1664
MOBA_GROUP = 4
GLA_CHUNK = 64
GLA_SUB = 16
MERGE_TM = 256
EXPERT_ROWS = 256
COMBINE_TM = 256
MASK_NEG = -1e30
DMA_ISSUE_UNROLL = 4
DMA_WAIT_UNROLL = 16

N_MAIN = 3 * D_MODEL + 2 * GLA_KEY_DIM + 2 * GLA_VAL_DIM
COL_GATES = N_MAIN
COL_GLOW = N_MAIN + 2 * D_MODEL
N_PROJ = COL_GLOW + LANES


def _dot(a, b, **kw):
    return jnp.dot(a, b, preferred_element_type=F32, **kw)


def _dot_nt(a, b, **kw):
    return lax.dot_general(a, b, (((1,), (1,)), ((), ())), preferred_element_type=F32, **kw)


def _cparams(*sem):
    return pltpu.CompilerParams(dimension_semantics=sem, vmem_limit_bytes=VMEM_LIMIT)


def _mod_kernel(c_ref, w_ref, b_ref, o_ref):
    c = c_ref[...]
    ca = c * jax.nn.sigmoid(c)
    o_ref[...] = _dot(ca, w_ref[...], precision=HIGHEST) + b_ref[...]


def _modulation(c, w_ada, b_ada):
    B = c.shape[0]
    n = w_ada.shape[1]
    c8 = jnp.zeros((SUBLANES, D_MODEL), F32).at[:B].set(c)
    out = pl.pallas_call(
        _mod_kernel,
        out_shape=jax.ShapeDtypeStruct((SUBLANES, n), F32),
        grid=(n // D_MODEL,),
        in_specs=[
            pl.BlockSpec((SUBLANES, D_MODEL), lambda j: (0, 0)),
            pl.BlockSpec((D_MODEL, D_MODEL), lambda j: (0, j)),
            pl.BlockSpec((1, D_MODEL), lambda j: (0, j)),
        ],
        out_specs=pl.BlockSpec((SUBLANES, D_MODEL), lambda j: (0, j)),
        compiler_params=_cparams("arbitrary"),
        name="adaln_mod",
    )(c8, w_ada, b_ada.reshape(1, n))
    return out[:B].reshape(B, 1, n)


def _inproj_kernel(x_ref, nw_ref, sc_ref, sh_ref, w_ref, o_ref, h_ref):
    @pl.when(pl.program_id(1) == 0)
    def _():
        x = x_ref[...]
        ms = jnp.mean(x * x, axis=-1, keepdims=True)
        y = x * lax.rsqrt(ms + RMS_EPS) * nw_ref[...]
        h_ref[...] = (y * (1.0 + sc_ref[...]) + sh_ref[...]).astype(BF16)

    o_ref[...] = _dot(h_ref[...], w_ref[...])


def _input_projection(x2, norm_w, mod, w_proj, S):
    T = x2.shape[0]
    tiles_per_batch = S // INPROJ_TM
    return pl.pallas_call(
        _inproj_kernel,
        out_shape=jax.ShapeDtypeStruct((T, N_PROJ), F32),
        grid=(T // INPROJ_TM, N_PROJ // INPROJ_TN),
        in_specs=[
            pl.BlockSpec((INPROJ_TM, D_MODEL), lambda i, j: (i, 0)),
            pl.BlockSpec((1, D_MODEL), lambda i, j: (0, 0)),
            pl.BlockSpec((None, 1, D_MODEL), lambda i, j: (i // tiles_per_batch, 0, 1)),
            pl.BlockSpec((None, 1, D_MODEL), lambda i, j: (i // tiles_per_batch, 0, 0)),
            pl.BlockSpec((D_MODEL, INPROJ_TN), lambda i, j: (0, j)),
        ],
        out_specs=pl.BlockSpec((INPROJ_TM, INPROJ_TN), lambda i, j: (i, j)),
        scratch_shapes=[pltpu.VMEM((INPROJ_TM, D_MODEL), BF16)],
        compiler_params=_cparams("parallel", "arbitrary"),
        name="inproj",
    )(x2, norm_w.reshape(1, D_MODEL), mod, mod, w_proj)


def _rope_tables(S):
    half = ROPE_DIM // 2
    inv_freq = jnp.float32(ROPE_THETA) ** (-jnp.arange(half, dtype=jnp.float32) * 2.0 / ROPE_DIM)
    ang = jnp.arange(S, dtype=jnp.float32)[:, None] * inv_freq[None, :]
    cos, sin = jnp.cos(ang), jnp.sin(ang)
    ones = jnp.ones((S, MOBA_HEAD_DIM - ROPE_DIM), F32)
    zeros_hi = jnp.zeros((S, MOBA_HEAD_DIM - half), F32)
    zeros_lo = jnp.zeros((S, half), F32)
    cos_f = jnp.concatenate([cos, cos, ones], axis=1)
    sin_a = jnp.concatenate([-sin, zeros_hi], axis=1)
    sin_b = jnp.concatenate([zeros_lo, sin, ones * 0.0], axis=1)
    return cos_f, sin_a, sin_b


def _rope(x, cos_f, sin_a, sin_b):
    half = ROPE_DIM // 2
    up = pltpu.roll(x, MOBA_HEAD_DIM - half, 1)
    dn = pltpu.roll(x, half, 1)
    return x * cos_f + up * sin_a + dn * sin_b


def _kprep_kernel(k_ref, v_ref, cos_ref, sa_ref, sb_ref, kaug_ref, vb_ref, kmean_ref):
    n = pl.program_id(1)
    cos_f, sin_a, sin_b = cos_ref[...], sa_ref[...], sb_ref[...]
    lane = lax.broadcasted_iota(jnp.int32, (MOBA_BLOCK, LANES), 1)
    onehot = jnp.where(lane == n, 1.0, 0.0).astype(BF16)
    for h in range(MOBA_HEADS):
        cols = slice(h * MOBA_HEAD_DIM, (h + 1) * MOBA_HEAD_DIM)
        kr = _rope(k_ref[:, cols], cos_f, sin_a, sin_b)
        kmean_ref[:, cols] = jnp.mean(kr, axis=0, keepdims=True)
        kaug_ref[h, :, :MOBA_HEAD_DIM] = kr.astype(BF16)
        kaug_ref[h, :, MOBA_HEAD_DIM:] = onehot
        vb_ref[h] = v_ref[:, cols].astype(BF16)


def _moba_prep(proj, tables, B, S):
    NB = S // MOBA_BLOCK
    NBP = NB + MOBA_GROUP - 1
    src = lambda n: jnp.minimum(n, NB - 1)
    tab_spec = pl.BlockSpec((MOBA_BLOCK, LANES), lambda b, n: (src(n), 0))
    return pl.pallas_call(
        _kprep_kernel,
        out_shape=(
            jax.ShapeDtypeStruct((B, MOBA_HEADS, NBP * MOBA_BLOCK, 2 * MOBA_HEAD_DIM), BF16),
            jax.ShapeDtypeStruct((B, MOBA_HEADS, NBP * MOBA_BLOCK, MOBA_HEAD_DIM), BF16),
            jax.ShapeDtypeStruct((B, NBP, 1, D_MODEL), F32),
        ),
        grid=(B, NBP),
        in_specs=[
            pl.BlockSpec((MOBA_BLOCK, D_MODEL), lambda b, n: (b * NB + src(n), 1)),
            pl.BlockSpec((MOBA_BLOCK, D_MODEL), lambda b, n: (b * NB + src(n), 2)),
            tab_spec, tab_spec, tab_spec,
        ],
        out_specs=(
            pl.BlockSpec((None, MOBA_HEADS, MOBA_BLOCK, 2 * MOBA_HEAD_DIM), lambda b, n: (b, 0, n, 0)),
            pl.BlockSpec((None, MOBA_HEADS, MOBA_BLOCK, MOBA_HEAD_DIM), lambda b, n: (b, 0, n, 0)),
            pl.BlockSpec((None, None, 1, D_MODEL), lambda b, n: (b, n, 0, 0)),
        ),
        compiler_params=_cparams("parallel", "parallel"),
        name="moba_kprep",
    )(proj, proj, *tables)


def _qprep_kernel(q_ref, cos_ref, sa_ref, sb_ref, kmean_ref, qaug_ref):
    i = pl.program_id(1)
    blk = MOBA_BLOCK
    scale = MOBA_HEAD_DIM ** -0.5
    cos_f, sin_a, sin_b = cos_ref[...], sa_ref[...], sb_ref[...]
    lane = lax.broadcasted_iota(jnp.int32, (blk, LANES), 1)
    neg_inf = jnp.float32(-jnp.inf)
    for h in range(MOBA_HEADS):
        cols = slice(h * MOBA_HEAD_DIM, (h + 1) * MOBA_HEAD_DIM)
        qr = _rope(q_ref[:, cols], cos_f, sin_a, sin_b)
        gate = _dot_nt(qr, kmean_ref[h], precision=HIGHEST)
        g = jnp.where(lane < i, gate, neg_inf)
        bias = jnp.where(lane == i, 0.0, MASK_NEG)
        for _ in range(MOBA_TOPK):
            mx = jnp.max(g, axis=1, keepdims=True)
            idx = jnp.min(jnp.where(g == mx, lane, LANES), axis=1, keepdims=True)
            pick = lane == idx
            bias = jnp.where(pick, jnp.where(mx > neg_inf, 0.0, bias), bias)
            g = jnp.where(pick, neg_inf, g)
        qaug_ref[h, :, :MOBA_HEAD_DIM] = (qr * scale).astype(BF16)
        qaug_ref[h, :, MOBA_HEAD_DIM:] = bias.astype(BF16)


def _moba_qprep(proj, tables, kmean, B, S):
    NB = S // MOBA_BLOCK
    tab_spec = pl.BlockSpec((MOBA_BLOCK, LANES), lambda b, i: (i, 0))
    return pl.pallas_call(
        _qprep_kernel,
        out_shape=jax.ShapeDtypeStruct((B, MOBA_HEADS, S, 2 * MOBA_HEAD_DIM), BF16),
        grid=(B, NB),
        in_specs=[
            pl.BlockSpec((MOBA_BLOCK, D_MODEL), lambda b, i: (b * NB + i, 0)),
            tab_spec, tab_spec, tab_spec,
            pl.BlockSpec((None, MOBA_HEADS, LANES, MOBA_HEAD_DIM), lambda b, i: (b, 0, 0, 0)),
        ],
        out_specs=pl.BlockSpec((None, MOBA_HEADS, MOBA_BLOCK, 2 * MOBA_HEAD_DIM), lambda b, i: (b, 0, i, 0)),
        compiler_params=_cparams("parallel", "parallel"),
        name="moba_qprep",
    )(proj, *tables, kmean)


def _moba_kernel(q_ref, kaug_ref, v_ref, o_ref, m_ref, l_ref, acc_ref):
    i = pl.program_id(2)
    blk, G = MOBA_BLOCK, MOBA_GROUP
    width = G * blk
    nch = width // LANES
    q = q_ref[...]
    n_full = i // G

    def group(g):
        start = pl.multiple_of(g * width, width)
        s = _dot_nt(q, kaug_ref[pl.ds(start, width), :])
        return [s[:, c * LANES:(c + 1) * LANES] for c in range(nch)], v_ref[pl.ds(start, width), :]

    def row_max(chunks):
        return jnp.max(functools.reduce(jnp.maximum, chunks), axis=1, keepdims=True)

    def row_sum(chunks):
        return jnp.sum(functools.reduce(jnp.add, chunks), axis=1, keepdims=True)

    chunks, vg = group(n_full)
    row = lax.broadcasted_iota(jnp.int32, (blk, LANES), 0)
    col = lax.broadcasted_iota(jnp.int32, (blk, LANES), 1)
    off = (i - n_full * G) * blk
    neg_inf = jnp.float32(-jnp.inf)
    chunks = [jnp.where(col + (c * LANES) - off > row, neg_inf, s) for c, s in enumerate(chunks)]
    m0 = jnp.broadcast_to(row_max(chunks), (blk, LANES))
    p = [jnp.exp(s - m0) for s in chunks]
    m_ref[...] = m0
    l_ref[...] = jnp.broadcast_to(row_sum(p), (blk, LANES))
    acc_ref[...] = _dot(jnp.concatenate(p, axis=1).astype(BF16), vg)

    def body(g, carry):
        chunks, vg = group(g)
        m_prev = m_ref[...]
        m_new = jnp.maximum(m_prev, row_max(chunks))
        alpha = jnp.exp(m_prev - m_new)
        p = [jnp.exp(s - m_new) for s in chunks]
        l_ref[...] = alpha * l_ref[...] + row_sum(p)
        m_ref[...] = m_new
        acc_ref[...] = alpha * acc_ref[...] + _dot(jnp.concatenate(p, axis=1).astype(BF16), vg)
        return carry

    lax.fori_loop(0, n_full, body, 0)
    o_ref[...] = (acc_ref[...] / l_ref[...]).astype(o_ref.dtype)


def _moba_attention(qaug, kaug, vb, B, S):
    NB = S // MOBA_BLOCK
    T = B * S
    SP = kaug.shape[2]
    return pl.pallas_call(
        _moba_kernel,
        out_shape=jax.ShapeDtypeStruct((T, D_MODEL), BF16),
        grid=(B, MOBA_HEADS, NB),
        in_specs=[
            pl.BlockSpec((None, None, MOBA_BLOCK, 2 * MOBA_HEAD_DIM), lambda b, h, i: (b, h, i, 0)),
            pl.BlockSpec((None, None, SP, 2 * MOBA_HEAD_DIM), lambda b, h, i: (b, h, 0, 0)),
            pl.BlockSpec((None, None, SP, MOBA_HEAD_DIM), lambda b, h, i: (b, h, 0, 0)),
        ],
        out_specs=pl.BlockSpec((MOBA_BLOCK, MOBA_HEAD_DIM), lambda b, h, i: (b * NB + i, h)),
        scratch_shapes=[
            pltpu.VMEM((MOBA_BLOCK, LANES), F32),
            pltpu.VMEM((MOBA_BLOCK, LANES), F32),
            pltpu.VMEM((MOBA_BLOCK, MOBA_HEAD_DIM), F32),
        ],
        compiler_params=_cparams("parallel", "parallel", "arbitrary"),
        name="moba_attn",
    )(qaug, kaug, vb)


def _gla_kernel(q_ref, k_ref, v_ref, gr_ref, gl_ref, gw_ref, gb_ref, nw_ref, o_ref, state_ref):
    C, SUB = GLA_CHUNK, GLA_SUB
    nsub = C // SUB

    @pl.when(pl.program_id(1) == 0)
    def _():
        state_ref[...] = jnp.zeros_like(state_ref)

    z = _dot(gl_ref[...].astype(BF16), gw_ref[...].astype(BF16)) + gb_ref[...]
    log_g = jax.nn.log_sigmoid(z) / GLA_GATE_NORMALIZER
    r_i = lax.broadcasted_iota(jnp.int32, (C, C), 0)
    c_i = lax.broadcasted_iota(jnp.int32, (C, C), 1)
    tril = jnp.where(c_i <= r_i, 1.0, 0.0).astype(F32)
    b_all = _dot(tril, log_g, precision=HIGHEST)

    row_c = lax.broadcasted_iota(jnp.int32, (C, GLA_DK), 0)
    row_s = lax.broadcasted_iota(jnp.int32, (SUB, GLA_DK), 0)
    lane_s = lax.broadcasted_iota(jnp.int32, (SUB, C), 1)
    neg_inf = jnp.float32(-jnp.inf)
    nw = nw_ref[...]

    for h in range(GLA_HEADS):
        kc = slice(h * GLA_DK, (h + 1) * GLA_DK)
        vc = slice(h * GLA_DV, (h + 1) * GLA_DV)
        b = b_all[:, kc]
        q = q_ref[:, kc] * (GLA_DK ** -0.5)
        k = k_ref[:, kc]
        v = v_ref[:, vc]
        v16 = v.astype(BF16)
        st = state_ref[h]
        b_last = b[C - 1:C, :]

        o = _dot_nt((q * jnp.exp(b)).astype(BF16), st.astype(BF16))

        s_rows = []
        for I in range(nsub):
            rs = slice(I * SUB, (I + 1) * SUB)
            q_i, k_i, b_i = q[rs], k[rs], b[rs]
            if I == 0:
                s_i = jnp.zeros((SUB, C), F32)
            else:
                ref_b = b[I * SUB - 1:I * SUB, :]
                qs = q_i * jnp.exp(b_i - ref_b)
                ks = k * jnp.exp(jnp.where(row_c < I * SUB, ref_b - b, neg_inf))
                s_i = _dot_nt(qs.astype(BF16), ks.astype(BF16))
            for j in range(SUB):
                diff = jnp.where(row_s >= j, b_i - b_i[j:j + 1, :], neg_inf)
                colv = jnp.sum(q_i * k_i[j:j + 1, :] * jnp.exp(diff), axis=1, keepdims=True)
                s_i = jnp.where(lane_s == I * SUB + j, colv, s_i)
            s_rows.append(s_i)
        s_full = jnp.concatenate(s_rows, axis=0)
        o = o + _dot(s_full.astype(BF16), v16)

        kd = k * jnp.exp(b_last - b)
        state_ref[h] = st * jnp.exp(b_last) + _dot(v.T.astype(BF16), kd.astype(BF16))

        ms = jnp.mean(o * o, axis=-1, keepdims=True)
        y = o * lax.rsqrt(ms + RMS_EPS) * nw
        g = gr_ref[:, vc]
        o_ref[:, vc] = (y * (g * jax.nn.sigmoid(g))).astype(o_ref.dtype)


def _gla(proj, gate_w, gate_b, norm_w, B, S):
    T = B * S
    NC = S // GLA_CHUNK
    gw = jnp.zeros((LANES, GLA_KEY_DIM), F32).at[:GLA_GATE_RANK].set(gate_w)
    q_blk = (3 * D_MODEL) // GLA_KEY_DIM
    v_blk = (3 * D_MODEL + 2 * GLA_KEY_DIM) // GLA_VAL_DIM
    return pl.pallas_call(
        _gla_kernel,
        out_shape=jax.ShapeDtypeStruct((T, GLA_VAL_DIM), BF16),
        grid=(B, NC),
        in_specs=[
            pl.BlockSpec((GLA_CHUNK, GLA_KEY_DIM), lambda b, c: (b * NC + c, q_blk)),
            pl.BlockSpec((GLA_CHUNK, GLA_KEY_DIM), lambda b, c: (b * NC + c, q_blk + 1)),
            pl.BlockSpec((GLA_CHUNK, GLA_VAL_DIM), lambda b, c: (b * NC + c, v_blk)),
            pl.BlockSpec((GLA_CHUNK, GLA_VAL_DIM), lambda b, c: (b * NC + c, v_blk + 1)),
            pl.BlockSpec((GLA_CHUNK, LANES), lambda b, c: (b * NC + c, COL_GLOW // LANES)),
            pl.BlockSpec((LANES, GLA_KEY_DIM), lambda b, c: (0, 0)),
            pl.BlockSpec((1, GLA_KEY_DIM), lambda b, c: (0, 0)),
            pl.BlockSpec((1, GLA_DV), lambda b, c: (0, 0)),
        ],
        out_specs=pl.BlockSpec((GLA_CHUNK, GLA_VAL_DIM), lambda b, c: (b * NC + c, 0)),
        scratch_shapes=[pltpu.VMEM((GLA_HEADS, GLA_DV, GLA_DK), F32)],
        compiler_params=_cparams("parallel", "arbitrary"),
        name="gla",
    )(proj, proj, proj, proj, proj, gw, gate_b.reshape(1, GLA_KEY_DIM), norm_w.reshape(1, GLA_DV))


def _merge_kernel(oa_ref, ob_ref, ga_ref, gb_ref, x_ref, wa_ref, wb_ref, wo_ref, bma_ref, bmb_ref,
                  g1_ref, sc2_ref, sh2_ref, nw_ref, wr_ref, br_ref,
                  x1_ref, h2_ref, e4_ref, w4_ref, p4_ref, cnt_ref, run_ref):
    tm = MERGE_TM

    @pl.when(pl.program_id(0) == 0)
    def _():
        run_ref[...] = jnp.zeros_like(run_ref)

    y_a = _dot(oa_ref[...], wa_ref[...])
    y_b = _dot(ob_ref[...], wb_ref[...])
    g_a = jax.nn.sigmoid(ga_ref[...] + bma_ref[...])
    g_b = jax.nn.sigmoid(gb_ref[...] + bmb_ref[...])
    mix = _dot((g_a * y_a + g_b * y_b).astype(BF16), wo_ref[...])
    x1 = x_ref[...] + g1_ref[...] * mix
    x1_ref[...] = x1

    ms = jnp.mean(x1 * x1, axis=-1, keepdims=True)
    h2 = x1 * lax.rsqrt(ms + RMS_EPS) * nw_ref[...] * (1.0 + sc2_ref[...]) + sh2_ref[...]
    for s in range(D_MODEL // LANES):
        h2_ref[pl.ds(s, tm, stride=SUBLANES), :] = h2[:, s * LANES:(s + 1) * LANES]

    logits = _dot(h2, wr_ref[...], precision=HIGHEST) + br_ref[...]
    lane = lax.broadcasted_iota(jnp.int32, (tm, LANES), 1)
    neg_inf = jnp.float32(-jnp.inf)
    g = jnp.where(lane < N_EXPERTS, logits, neg_inf)
    picks, tops = [], []
    for _ in range(TOP_K):
        mx = jnp.max(g, axis=1, keepdims=True)
        idx = jnp.min(jnp.where(g == mx, lane, LANES), axis=1, keepdims=True)
        pick = lane == idx
        picks.append((pick, idx))
        tops.append(mx)
        g = jnp.where(pick, neg_inf, g)
    ex = [jnp.exp(t - tops[0]) for t in tops]
    denom = ex[0] + ex[1] + ex[2] + ex[3]

    sel = sum(jnp.where(pick, 1.0, 0.0) for pick, _ in picks)
    r_i = lax.broadcasted_iota(jnp.int32, (tm, tm), 0)
    c_i = lax.broadcasted_iota(jnp.int32, (tm, tm), 1)
    lower = jnp.where(c_i < r_i, 1.0, 0.0).astype(BF16)
    rank = run_ref[0:1, :] + _dot(lower, sel.astype(BF16))
    run_new = run_ref[0:1, :] + jnp.sum(sel, axis=0, keepdims=True)
    run_ref[...] = jnp.broadcast_to(run_new, run_ref.shape)
    cnt_ref[...] = jnp.broadcast_to(run_new, cnt_ref.shape)

    e4 = jnp.zeros((tm, LANES), jnp.int32)
    w4 = jnp.zeros((tm, LANES), F32)
    p4 = jnp.zeros((tm, LANES), F32)
    for r in range(TOP_K):
        pick, idx = picks[r]
        pos = jnp.sum(jnp.where(pick, rank, 0.0), axis=1, keepdims=True)
        e4 = jnp.where(lane == r, idx, e4)
        w4 = jnp.where(lane == r, ex[r] / denom, w4)
        p4 = jnp.where(lane == r, pos, p4)
    e4_ref[...] = e4
    w4_ref[...] = w4
    p4_ref[...] = p4


def _merge_and_route(o_a, o_b, proj, x2, w_a, w_b, w_o, b_merge, mod, norm2_w, w_router, b_router, S):
    T = x2.shape[0]
    tm = MERGE_TM
    tiles_per_batch = S // tm
    full = lambda shape: pl.BlockSpec(shape, lambda i: tuple(0 for _ in shape))
    row = lambda width, col: pl.BlockSpec((tm, width), lambda i: (i, col))
    modv = lambda k: pl.BlockSpec((None, 1, D_MODEL), lambda i: (i // tiles_per_batch, 0, k))
    wr = jnp.zeros((D_MODEL, LANES), F32).at[:, :N_EXPERTS].set(w_router)
    br = jnp.zeros((1, LANES), F32).at[0, :N_EXPERTS].set(b_router)
    bm = b_merge.reshape(1, 2 * D_MODEL)
    return pl.pallas_call(
        _merge_kernel,
        out_shape=(
            jax.ShapeDtypeStruct((T, D_MODEL), F32),
            jax.ShapeDtypeStruct((T * SUBLANES, LANES), F32),
            jax.ShapeDtypeStruct((T, LANES), jnp.int32),
            jax.ShapeDtypeStruct((T, LANES), F32),
            jax.ShapeDtypeStruct((T, LANES), F32),
            jax.ShapeDtypeStruct((SUBLANES, LANES), F32),
        ),
        grid=(T // tm,),
        in_specs=[
            row(D_MODEL, 0), row(D_MODEL, 0),
            row(D_MODEL, COL_GATES // D_MODEL), row(D_MODEL, COL_GATES // D_MODEL + 1),
            row(D_MODEL, 0),
            full((D_MODEL, D_MODEL)), full((D_MODEL, D_MODEL)), full((D_MODEL, D_MODEL)),
            pl.BlockSpec((1, D_MODEL), lambda i: (0, 0)), pl.BlockSpec((1, D_MODEL), lambda i: (0, 1)),
            modv(2), modv(4), modv(3),
            full((1, D_MODEL)), full((D_MODEL, LANES)), full((1, LANES)),
        ],
        out_specs=(
            row(D_MODEL, 0),
            pl.BlockSpec((tm * SUBLANES, LANES), lambda i: (i, 0)),
            row(LANES, 0), row(LANES, 0), row(LANES, 0),
            full((SUBLANES, LANES)),
        ),
        scratch_shapes=[pltpu.VMEM((SUBLANES, LANES), F32)],
        compiler_params=_cparams("arbitrary"),
        name="merge_route",
    )(o_a, o_b, proj, proj, x2, w_a, w_b, w_o, bm, bm, mod, mod, mod,
      norm2_w.reshape(1, D_MODEL), wr, br)


def _row_slab(ref, row):
    return ref.at[pl.ds(pl.multiple_of(row * SUBLANES, SUBLANES), SUBLANES), :]


def _wait_slabs(src_ref, dst_ref, sem, count):
    def wait(a, c):
        pltpu.make_async_copy(_row_slab(src_ref, 0), _row_slab(dst_ref, 0), sem).wait()
        return c

    lax.fori_loop(0, count, wait, 0, unroll=DMA_WAIT_UNROLL)


def _dispatch_kernel(dest_ref, h_ref, xs_in_ref, xs_ref, sem):
    del xs_in_ref
    tm = COMBINE_TM
    base = pl.program_id(0) * (tm * TOP_K)

    def start(t, c):
        src = _row_slab(h_ref, t)
        for kk in range(TOP_K):
            pltpu.make_async_copy(src, _row_slab(xs_ref, dest_ref[base + t * TOP_K + kk]), sem).start()
        return c

    lax.fori_loop(0, tm, start, 0, unroll=DMA_ISSUE_UNROLL)
    _wait_slabs(h_ref, xs_ref, sem, tm * TOP_K)


def _dispatch(dest, h2_slabs, P):
    T = h2_slabs.shape[0] // SUBLANES
    tm = COMBINE_TM
    xs0 = jnp.zeros((P * SUBLANES, LANES), F32)
    return pl.pallas_call(
        _dispatch_kernel,
        out_shape=jax.ShapeDtypeStruct((P * SUBLANES, LANES), F32),
        grid_spec=pltpu.PrefetchScalarGridSpec(
            num_scalar_prefetch=1,
            grid=(T // tm,),
            in_specs=[
                pl.BlockSpec((tm * SUBLANES, LANES), lambda i, dest: (i, 0)),
                pl.BlockSpec(memory_space=pl.ANY),
            ],
            out_specs=pl.BlockSpec(memory_space=pl.ANY),
            scratch_shapes=[pltpu.SemaphoreType.DMA],
        ),
        input_output_aliases={2: 0},
        compiler_params=_cparams("arbitrary"),
        name="moe_dispatch",
    )(dest, h2_slabs, xs0)


def _expert_kernel(be_ref, nu_ref, xs_ref, wi_ref, bi_ref, wo_ref, bo_ref, perm_ref, ys_ref,
                   x_ref, wi16_ref, wo16_ref):
    R = EXPERT_ROWS
    j = pl.program_id(0)
    used = j < nu_ref[0]
    new_expert = (j == 0) | (be_ref[j] != be_ref[jnp.maximum(j - 1, 0)])

    @pl.when(used & new_expert)
    def _():
        perm = perm_ref[...]
        for g in range(D_MODEL // LANES):
            w = wi_ref[:, 2 * LANES * g:2 * LANES * (g + 1)].astype(BF16)
            sep = _dot(w, perm).astype(BF16)
            wi16_ref[:, LANES * g:LANES * (g + 1)] = sep[:, :LANES]
            wi16_ref[:, D_MODEL + LANES * g:D_MODEL + LANES * (g + 1)] = sep[:, LANES:]
        wo16_ref[...] = wo_ref[...].astype(BF16)

    @pl.when(used)
    def _():
        for s in range(D_MODEL // LANES):
            x_ref[:, s * LANES:(s + 1) * LANES] = xs_ref[pl.ds(s, R, stride=SUBLANES), :].astype(BF16)
        hid = _dot(x_ref[...], wi16_ref[...]) + bi_ref[...]
        glu = jnp.minimum(hid[:, :D_MODEL], SWIGLU_LIMIT)
        lin = jnp.clip(hid[:, D_MODEL:], -SWIGLU_LIMIT, SWIGLU_LIMIT)
        act = glu * jax.nn.sigmoid(SWIGLU_ALPHA * glu) * (lin + 1.0)
        out = _dot(act.astype(BF16), wo16_ref[...]) + bo_ref[...]
        for s in range(D_MODEL // LANES):
            ys_ref[pl.ds(s, R, stride=SUBLANES), :] = out[:, s * LANES:(s + 1) * LANES]

    @pl.when(j >= nu_ref[0])
    def _():
        ys_ref[...] = jnp.zeros_like(ys_ref)


def _experts(block_expert, n_used, xs, w_in, b_in, w_out, b_out):
    R = EXPERT_ROWS
    n_blocks = xs.shape[0] // (R * SUBLANES)
    src = jnp.arange(2 * LANES)[:, None]
    dst = jnp.arange(2 * LANES)[None, :]
    perm = jnp.where(src == jnp.where(dst < LANES, 2 * dst, 2 * (dst - LANES) + 1), 1.0, 0.0).astype(BF16)
    return pl.pallas_call(
        _expert_kernel,
        out_shape=jax.ShapeDtypeStruct(xs.shape, F32),
        grid_spec=pltpu.PrefetchScalarGridSpec(
            num_scalar_prefetch=2,
            grid=(n_blocks,),
            in_specs=[
                pl.BlockSpec((R * SUBLANES, LANES), lambda j, be, nu: (jnp.minimum(j, nu[0] - 1), 0)),
                pl.BlockSpec((None, D_MODEL, 2 * D_MODEL), lambda j, be, nu: (be[j], 0, 0)),
                pl.BlockSpec((None, 1, 2 * D_MODEL), lambda j, be, nu: (be[j], 0, 0)),
                pl.BlockSpec((None, D_MODEL, D_MODEL), lambda j, be, nu: (be[j], 0, 0)),
                pl.BlockSpec((None, 1, D_MODEL), lambda j, be, nu: (be[j], 0, 0)),
                pl.BlockSpec((2 * LANES, 2 * LANES), lambda j, be, nu: (0, 0)),
            ],
            out_specs=pl.BlockSpec((R * SUBLANES, LANES), lambda j, be, nu: (j, 0)),
            scratch_shapes=[
                pltpu.VMEM((R, D_MODEL), BF16),
                pltpu.VMEM((D_MODEL, 2 * D_MODEL), BF16),
                pltpu.VMEM((D_MODEL, D_MODEL), BF16),
            ],
        ),
        compiler_params=pltpu.CompilerParams(
            dimension_semantics=("arbitrary",), vmem_limit_bytes=EXPERT_VMEM_LIMIT),
        name="moe_experts",
    )(block_expert, n_used, xs, w_in, b_in, w_out, b_out, perm)


def _combine_kernel(dest_ref, ys_ref, w4_ref, x1_ref, g2_ref, nw_ref, o_ref, buf_ref, sem):
    tm = COMBINE_TM
    base = pl.program_id(0) * (tm * TOP_K)

    def start(t, c):
        for kk in range(TOP_K):
            pltpu.make_async_copy(_row_slab(ys_ref, dest_ref[base + t * TOP_K + kk]),
                                  _row_slab(buf_ref.at[kk], t), sem).start()
        return c

    lax.fori_loop(0, tm, start, 0, unroll=DMA_ISSUE_UNROLL)
    _wait_slabs(ys_ref, buf_ref.at[0], sem, tm * TOP_K)

    w4 = w4_ref[...]
    g2 = g2_ref[...]
    parts = []
    ssq = jnp.zeros((tm, 1), F32)
    for s in range(D_MODEL // LANES):
        cols = slice(s * LANES, (s + 1) * LANES)
        y = jnp.zeros((tm, LANES), F32)
        for kk in range(TOP_K):
            y = y + buf_ref[kk, pl.ds(s, tm, stride=SUBLANES), :] * w4[:, kk:kk + 1]
        x2 = x1_ref[:, cols] + g2[:, cols] * y
        ssq = ssq + jnp.sum(x2 * x2, axis=1, keepdims=True)
        parts.append(x2)
    inv = lax.rsqrt(ssq / D_MODEL + RMS_EPS)
    nw = nw_ref[...]
    for s in range(D_MODEL // LANES):
        cols = slice(s * LANES, (s + 1) * LANES)
        o_ref[:, cols] = parts[s] * inv * nw[:, cols]


def _combine(dest, ys, w4, x1, mod, final_w, S):
    T = x1.shape[0]
    tm = COMBINE_TM
    tiles_per_batch = S // tm
    return pl.pallas_call(
        _combine_kernel,
        out_shape=jax.ShapeDtypeStruct((T, D_MODEL), F32),
        grid_spec=pltpu.PrefetchScalarGridSpec(
            num_scalar_prefetch=1,
            grid=(T // tm,),
            in_specs=[
                pl.BlockSpec(memory_space=pl.ANY),
                pl.BlockSpec((tm, LANES), lambda i, dest: (i, 0)),
                pl.BlockSpec((tm, D_MODEL), lambda i, dest: (i, 0)),
                pl.BlockSpec((None, 1, D_MODEL), lambda i, dest: (i // tiles_per_batch, 0, 5)),
                pl.BlockSpec((1, D_MODEL), lambda i, dest: (0, 0)),
            ],
            out_specs=pl.BlockSpec((tm, D_MODEL), lambda i, dest: (i, 0)),
            scratch_shapes=[
                pltpu.VMEM((TOP_K, tm * SUBLANES, LANES), F32),
                pltpu.SemaphoreType.DMA,
            ],
        ),
        compiler_params=_cparams("arbitrary"),
        name="moe_combine",
    )(dest, ys, w4, x1, mod, final_w.reshape(1, D_MODEL))


def _layer(x2, c_mod, B, S, norm1_w, w_in, gla_gate_w, gla_gate_b, gla_norm_w, w_o_moba, w_o_gla,
           b_merge, w_out, norm2_w, w_router, b_router, w_exp_in, b_exp_in, w_exp_out, b_exp_out):
    T = B * S
    glow_lo = N_MAIN
    glow_hi = N_MAIN + GLA_GATE_RANK
    w_proj = jnp.concatenate(
        [w_in[:, :glow_lo], w_in[:, glow_hi:], w_in[:, glow_lo:glow_hi],
         jnp.zeros((D_MODEL, LANES - GLA_GATE_RANK), F32)], axis=1).astype(BF16)
    proj = _input_projection(x2, norm1_w, c_mod, w_proj, S)

    tables = _rope_tables(S)
    kaug, vb, kmean = _moba_prep(proj, tables, B, S)
    NB = S // MOBA_BLOCK
    kmean = kmean[:, :NB].reshape(B, NB, MOBA_HEADS, MOBA_HEAD_DIM).transpose(0, 2, 1, 3)
    kmean = jnp.pad(kmean, ((0, 0), (0, 0), (0, LANES - NB), (0, 0)))
    qaug = _moba_qprep(proj, tables, kmean, B, S)
    o_a = _moba_attention(qaug, kaug, vb, B, S)
    o_b = _gla(proj, gla_gate_w, gla_gate_b, gla_norm_w, B, S)

    x1, h2, e4, w4, p4, cnt = _merge_and_route(
        o_a, o_b, proj, x2, w_o_moba.astype(BF16), w_o_gla.astype(BF16), w_out.astype(BF16),
        b_merge, c_mod, norm2_w, w_router, b_router, S)

    R = EXPERT_ROWS
    counts = cnt[0, :N_EXPERTS].astype(jnp.int32)
    padded = (counts + R - 1) // R * R
    padded_end = jnp.cumsum(padded)
    padded_start = padded_end - padded
    n_blocks = -(-(T * TOP_K + N_EXPERTS * (R - 1)) // R)
    P = n_blocks * R
    dest = (padded_start[e4[:, :TOP_K]] + p4[:, :TOP_K].astype(jnp.int32)).reshape(-1)
    block_row0 = jnp.arange(n_blocks, dtype=jnp.int32) * R
    block_expert = jnp.minimum(
        jnp.sum(padded_end[None, :] <= block_row0[:, None], axis=1), N_EXPERTS - 1).astype(jnp.int32)
    n_used = (padded_end[-1:] // R).astype(jnp.int32)

    xs = _dispatch(dest, h2, P)
    b_ei = jnp.concatenate([b_exp_in[:, 0::2], b_exp_in[:, 1::2]], axis=-1)[:, None, :]
    ys = _experts(block_expert, n_used, xs, w_exp_in, b_ei, w_exp_out, b_exp_out[:, None, :])
    return x1, ys, dest, w4


def kernel(x, c, w_ada, b_ada, norm1_w, w_in, gla_gate_w, gla_gate_b, gla_norm_w, w_o_moba, w_o_gla, b_merge, w_out, norm2_w, w_router, b_router, w_exp_in, b_exp_in, w_exp_out, b_exp_out, final_norm_w):
    B, S, _ = x.shape
    depth = w_ada.shape[0]
    assert depth == 1, "the combine kernel fuses the final norm, so a single layer is supported"
    x2 = x.reshape(B * S, D_MODEL)
    l = 0
    mod = _modulation(c, w_ada[l], b_ada[l])
    x1, ys, dest, w4 = _layer(
        x2, mod, B, S, norm1_w[l], w_in[l], gla_gate_w[l], gla_gate_b[l], gla_norm_w[l],
        w_o_moba[l], w_o_gla[l], b_merge[l], w_out[l], norm2_w[l], w_router[l], b_router[l],
        w_exp_in[l], b_exp_in[l], w_exp_out[l], b_exp_out[l])
    out = _combine(dest, ys, w4, x1, mod, final_norm_w, S)
    return out.reshape(B, S, D_MODEL)
```

```python
import functools

import jax
import jax.numpy as jnp
from jax import lax
from jax.experimental import pallas as pl
from jax.experimental.pallas import tpu as pltpu

F32 = jnp.float32
BF16 = jnp.bfloat16
HIGHEST = lax.Precision.HIGHEST

D_MODEL = 1024
MOBA_HEADS = 8
MOBA_HEAD_DIM = 128
MOBA_BLOCK = 256
MOBA_TOPK = 3
ROPE_DIM = MOBA_HEAD_DIM // 4
ROPE_THETA = 500000.0
GLA_HEADS = 4
GLA_KEY_DIM = D_MODEL // 2
GLA_VAL_DIM = D_MODEL
GLA_DK = GLA_KEY_DIM // GLA_HEADS
GLA_DV = GLA_VAL_DIM // GLA_HEADS
GLA_GATE_RANK = 16
GLA_GATE_NORMALIZER = 16.0
N_EXPERTS = 32
TOP_K = 4
SWIGLU_ALPHA = 1.702
SWIGLU_LIMIT = 7.0
RMS_EPS = 1e-5
LOG2E = 1.4426950408889634

LANES = 128
SUBLANES = 8
VMEM_LIMIT = 48 * 1024 * 1024
EXPERT_VMEM_LIMIT = 56 * 1024 * 1024

INPROJ_TM = 512
INPROJ_TN = 1664
MOBA_GROUP = 4
GLA_CHUNK = 64
GLA_SUB = 16
MERGE_TM = 256
EXPERT_ROWS = 256
COMBINE_TM = 256
MASK_NEG = -1e30
DMA_ISSUE_UNROLL = 4
DMA_WAIT_UNROLL = 16
DMA_PRIORITIES = 2

N_MAIN = 3 * D_MODEL + 2 * GLA_KEY_DIM + 2 * GLA_VAL_DIM
COL_GATES = N_MAIN
COL_GLOW = N_MAIN + 2 * D_MODEL
N_PROJ = COL_GLOW + LANES


def _dot(a, b, **kw):
    return jnp.dot(a, b, preferred_element_type=F32, **kw)


def _dot_nt(a, b, **kw):
    return lax.dot_general(a, b, (((1,), (1,)), ((), ())), preferred_element_type=F32, **kw)


def _first_max_lane(g, lane_f):
    mx = jnp.max(g, axis=1, keepdims=True)
    return jnp.min(jnp.where(g == mx, lane_f, float(LANES)), axis=1, keepdims=True)


def _cparams(*sem):
    return pltpu.CompilerParams(dimension_semantics=sem, vmem_limit_bytes=VMEM_LIMIT)


def _mod_kernel(c_ref, w_ref, b_ref, o_ref):
    c = c_ref[...]
    ca = c * jax.nn.sigmoid(c)
    o_ref[...] = _dot(ca, w_ref[...], precision=HIGHEST) + b_ref[...]


def _modulation(c, w_ada, b_ada):
    B = c.shape[0]
    n = w_ada.shape[1]
    c8 = jnp.zeros((SUBLANES, D_MODEL), F32).at[:B].set(c)
    out = pl.pallas_call(
        _mod_kernel,
        out_shape=jax.ShapeDtypeStruct((SUBLANES, n), F32),
        grid=(n // D_MODEL,),
        in_specs=[
            pl.BlockSpec((SUBLANES, D_MODEL), lambda j: (0, 0)),
            pl.BlockSpec((D_MODEL, D_MODEL), lambda j: (0, j)),
            pl.BlockSpec((1, D_MODEL), lambda j: (0, j)),
        ],
        out_specs=pl.BlockSpec((SUBLANES, D_MODEL), lambda j: (0, j)),
        compiler_params=_cparams("arbitrary"),
        name="adaln_mod",
    )(c8, w_ada, b_ada.reshape(1, n))
    return out[:B].reshape(B, 1, n)


def _inproj_kernel(x_ref, nw_ref, sc_ref, sh_ref, w_ref, o_ref, h_ref):
    @pl.when(pl.program_id(1) == 0)
    def _():
        x = x_ref[...]
        ms = jnp.mean(x * x, axis=-1, keepdims=True)
        y = x * lax.rsqrt(ms + RMS_EPS) * nw_ref[...]
        h_ref[...] = (y * (1.0 + sc_ref[...]) + sh_ref[...]).astype(BF16)

    o_ref[...] = _dot(h_ref[...], w_ref[...])


def _input_projection(x2, norm_w, mod, w_proj, S):
    T = x2.shape[0]
    tiles_per_batch = S // INPROJ_TM
    return pl.pallas_call(
        _inproj_kernel,
        out_shape=jax.ShapeDtypeStruct((T, N_PROJ), F32),
        grid=(T // INPROJ_TM, N_PROJ // INPROJ_TN),
        in_specs=[
            pl.BlockSpec((INPROJ_TM, D_MODEL), lambda i, j: (i, 0)),
            pl.BlockSpec((1, D_MODEL), lambda i, j: (0, 0)),
            pl.BlockSpec((None, 1, D_MODEL), lambda i, j: (i // tiles_per_batch, 0, 1)),
            pl.BlockSpec((None, 1, D_MODEL), lambda i, j: (i // tiles_per_batch, 0, 0)),
            pl.BlockSpec((D_MODEL, INPROJ_TN), lambda i, j: (0, j)),
        ],
        out_specs=pl.BlockSpec((INPROJ_TM, INPROJ_TN), lambda i, j: (i, j)),
        scratch_shapes=[pltpu.VMEM((INPROJ_TM, D_MODEL), BF16)],
        compiler_params=_cparams("parallel", "arbitrary"),
        name="inproj",
    )(x2, norm_w.reshape(1, D_MODEL), mod, mod, w_proj)


def _rope_tables(S):
    half = ROPE_DIM // 2
    inv_freq = jnp.float32(ROPE_THETA) ** (-jnp.arange(half, dtype=jnp.float32) * 2.0 / ROPE_DIM)
    ang = jnp.arange(S, dtype=jnp.float32)[:, None] * inv_freq[None, :]
    cos, sin = jnp.cos(ang), jnp.sin(ang)
    ones = jnp.ones((S, MOBA_HEAD_DIM - ROPE_DIM), F32)
    zeros_hi = jnp.zeros((S, MOBA_HEAD_DIM - half), F32)
    zeros_lo = jnp.zeros((S, half), F32)
    cos_f = jnp.concatenate([cos, cos, ones], axis=1)
    sin_a = jnp.concatenate([-sin, zeros_hi], axis=1)
    sin_b = jnp.concatenate([zeros_lo, sin, ones * 0.0], axis=1)
    return cos_f, sin_a, sin_b


def _rope(x, cos_f, sin_a, sin_b):
    half = ROPE_DIM // 2
    up = pltpu.roll(x, MOBA_HEAD_DIM - half, 1)
    dn = pltpu.roll(x, half, 1)
    return x * cos_f + up * sin_a + dn * sin_b


def _kprep_kernel(k_ref, v_ref, cos_ref, sa_ref, sb_ref, kaug_ref, vb_ref, kmean_ref):
    n = pl.program_id(1)
    cos_f, sin_a, sin_b = cos_ref[...], sa_ref[...], sb_ref[...]
    lane = lax.broadcasted_iota(jnp.int32, (MOBA_BLOCK, LANES), 1)
    onehot = jnp.where(lane == n, 1.0, 0.0).astype(BF16)
    for h in range(MOBA_HEADS):
        cols = slice(h * MOBA_HEAD_DIM, (h + 1) * MOBA_HEAD_DIM)
        kr = _rope(k_ref[:, cols], cos_f, sin_a, sin_b)
        kmean_ref[:, cols] = jnp.mean(kr, axis=0, keepdims=True)
        kaug_ref[h, :, :MOBA_HEAD_DIM] = kr.astype(BF16)
        kaug_ref[h, :, MOBA_HEAD_DIM:] = onehot
        vb_ref[h, :, :MOBA_HEAD_DIM] = v_ref[:, cols].astype(BF16)
        vb_ref[h, :, MOBA_HEAD_DIM:] = jnp.ones((MOBA_BLOCK, MOBA_HEAD_DIM), BF16)


def _moba_prep(proj, tables, B, S):
    NB = S // MOBA_BLOCK
    NBP = NB
    src = lambda n: n
    tab_spec = pl.BlockSpec((MOBA_BLOCK, LANES), lambda b, n: (src(n), 0))
    return pl.pallas_call(
        _kprep_kernel,
        out_shape=(
            jax.ShapeDtypeStruct((B, MOBA_HEADS, NBP * MOBA_BLOCK, 2 * MOBA_HEAD_DIM), BF16),
            jax.ShapeDtypeStruct((B, MOBA_HEADS, NBP * MOBA_BLOCK, 2 * MOBA_HEAD_DIM), BF16),
            jax.ShapeDtypeStruct((B, NBP, 1, D_MODEL), F32),
        ),
        grid=(B, NBP),
        in_specs=[
            pl.BlockSpec((MOBA_BLOCK, D_MODEL), lambda b, n: (b * NB + src(n), 1)),
            pl.BlockSpec((MOBA_BLOCK, D_MODEL), lambda b, n: (b * NB + src(n), 2)),
            tab_spec, tab_spec, tab_spec,
        ],
        out_specs=(
            pl.BlockSpec((None, MOBA_HEADS, MOBA_BLOCK, 2 * MOBA_HEAD_DIM), lambda b, n: (b, 0, n, 0)),
            pl.BlockSpec((None, MOBA_HEADS, MOBA_BLOCK, 2 * MOBA_HEAD_DIM), lambda b, n: (b, 0, n, 0)),
            pl.BlockSpec((None, None, 1, D_MODEL), lambda b, n: (b, n, 0, 0)),
        ),
        compiler_params=_cparams("parallel", "parallel"),
        name="moba_kprep",
    )(proj, proj, *tables)


def _qprep_kernel(q_ref, cos_ref, sa_ref, sb_ref, kmean_ref, qaug_ref):
    i = pl.program_id(1)
    blk = MOBA_BLOCK
    scale = MOBA_HEAD_DIM ** -0.5
    cos_f, sin_a, sin_b = cos_ref[...], sa_ref[...], sb_ref[...]
    lane = lax.broadcasted_iota(jnp.int32, (blk, LANES), 1)
    lane_f = lane.astype(F32)
    neg_inf = jnp.float32(-jnp.inf)
    for h in range(MOBA_HEADS):
        cols = slice(h * MOBA_HEAD_DIM, (h + 1) * MOBA_HEAD_DIM)
        qr = _rope(q_ref[:, cols], cos_f, sin_a, sin_b)
        gate = _dot_nt(qr, kmean_ref[h], precision=HIGHEST)
        g = jnp.where(lane < i, gate, neg_inf)
        bias = jnp.where(lane == i, 0.0, MASK_NEG)
        for r in range(MOBA_TOPK):
            pick = lane_f == _first_max_lane(g, lane_f)
            unmask = jnp.where(i > r, 0.0, MASK_NEG)
            bias = jnp.where(pick, jnp.maximum(bias, unmask), bias)
            g = jnp.where(pick, neg_inf, g)
        qaug_ref[h, :, :MOBA_HEAD_DIM] = (qr * (scale * LOG2E)).astype(BF16)
        qaug_ref[h, :, MOBA_HEAD_DIM:] = bias.astype(BF16)


def _moba_qprep(proj, tables, kmean, B, S):
    NB = S // MOBA_BLOCK
    tab_spec = pl.BlockSpec((MOBA_BLOCK, LANES), lambda b, i: (i, 0))
    return pl.pallas_call(
        _qprep_kernel,
        out_shape=jax.ShapeDtypeStruct((B, MOBA_HEADS, S, 2 * MOBA_HEAD_DIM), BF16),
        grid=(B, NB),
        in_specs=[
            pl.BlockSpec((MOBA_BLOCK, D_MODEL), lambda b, i: (b * NB + i, 0)),
            tab_spec, tab_spec, tab_spec,
            pl.BlockSpec((None, MOBA_HEADS, LANES, MOBA_HEAD_DIM), lambda b, i: (b, 0, 0, 0)),
        ],
        out_specs=pl.BlockSpec((None, MOBA_HEADS, MOBA_BLOCK, 2 * MOBA_HEAD_DIM), lambda b, i: (b, 0, i, 0)),
        compiler_params=_cparams("parallel", "parallel"),
        name="moba_qprep",
    )(proj, *tables, kmean)


def _moba_kernel(q_ref, kaug_ref, v_ref, o_ref, m_ref, acc_ref, s_ref, mx_ref):
    n_full = pl.program_id(2)
    width = MOBA_GROUP * MOBA_BLOCK
    blk = width
    nch = width // LANES
    q = q_ref[...]

    def scores(g):
        return _dot_nt(q, kaug_ref[pl.ds(pl.multiple_of(g * width, width), width), :])

    def values(g):
        return v_ref[pl.ds(pl.multiple_of(g * width, width), width), :]

    def split(s):
        return [s[:, c * LANES:(c + 1) * LANES] for c in range(nch)]

    def row_max(chunks):
        return jnp.max(functools.reduce(jnp.maximum, chunks), axis=1, keepdims=True)

    def prob_value(chunks, m, g):
        p = [jnp.exp2(s - m) for s in chunks]
        return _dot(jnp.concatenate(p, axis=1).astype(BF16), values(g))

    def stage(g):
        s = scores(g)
        s_ref[...] = s
        mx_ref[...] = jnp.broadcast_to(row_max(split(s)), (blk, LANES))

    s_own = scores(n_full)
    stage(0)
    row = lax.broadcasted_iota(jnp.int32, (blk, LANES), 0)
    col = lax.broadcasted_iota(jnp.int32, (blk, LANES), 1)
    neg_inf = jnp.float32(-jnp.inf)
    chunks = [jnp.where(col + (c * LANES) > row, neg_inf, s) for c, s in enumerate(split(s_own))]
    m0 = jnp.broadcast_to(row_max(chunks), (blk, LANES))
    m_ref[...] = m0
    acc_ref[...] = prob_value(chunks, m0, n_full)

    def body(g, carry):
        m_prev = m_ref[...]
        m_new = jnp.maximum(m_prev, mx_ref[...])
        alpha = jnp.exp2(m_prev - m_new)
        m_ref[...] = m_new
        pv = prob_value(split(s_ref[...]), m_new, g)
        stage(jnp.minimum(g + 1, jnp.maximum(n_full - 1, 0)))
        acc_ref[...] = jnp.concatenate([alpha, alpha], axis=1) * acc_ref[...] + pv
        return carry

    lax.fori_loop(0, n_full, body, 0)
    acc = acc_ref[...]
    o_ref[...] = (acc[:, :MOBA_HEAD_DIM] / acc[:, MOBA_HEAD_DIM:]).astype(o_ref.dtype)


def _moba_attention(qaug, kaug, vb, B, S):
    T = B * S
    tq = MOBA_GROUP * MOBA_BLOCK
    assert S % tq == 0
    NG = S // tq
    return pl.pallas_call(
        _moba_kernel,
        out_shape=jax.ShapeDtypeStruct((T, D_MODEL), BF16),
        grid=(B, MOBA_HEADS, NG),
        in_specs=[
            pl.BlockSpec((None, None, tq, 2 * MOBA_HEAD_DIM), lambda b, h, j: (b, h, j, 0)),
            pl.BlockSpec((None, None, S, 2 * MOBA_HEAD_DIM), lambda b, h, j: (b, h, 0, 0)),
            pl.BlockSpec((None, None, S, 2 * MOBA_HEAD_DIM), lambda b, h, j: (b, h, 0, 0)),
        ],
        out_specs=pl.BlockSpec((tq, MOBA_HEAD_DIM), lambda b, h, j: (b * NG + j, h)),
        scratch_shapes=[
            pltpu.VMEM((tq, LANES), F32),
            pltpu.VMEM((tq, 2 * MOBA_HEAD_DIM), F32),
            pltpu.VMEM((tq, tq), F32),
            pltpu.VMEM((tq, LANES), F32),
        ],
        compiler_params=_cparams("parallel", "parallel", "arbitrary"),
        name="moba_attn",
    )(qaug, kaug, vb)


def _gla_kernel(q_ref, k_ref, v_ref, gr_ref, gl_ref, gw_ref, gb_ref, nw_ref, o_ref, state_ref):
    C, SUB = GLA_CHUNK, GLA_SUB
    nsub = C // SUB

    @pl.when(pl.program_id(1) == 0)
    def _():
        state_ref[...] = jnp.zeros_like(state_ref)

    z = _dot(gl_ref[...].astype(BF16), gw_ref[...].astype(BF16)) + gb_ref[...]
    log_g = jax.nn.log_sigmoid(z) / GLA_GATE_NORMALIZER
    r_i = lax.broadcasted_iota(jnp.int32, (C, C), 0)
    c_i = lax.broadcasted_iota(jnp.int32, (C, C), 1)
    tril = jnp.where(c_i <= r_i, 1.0, 0.0).astype(F32)
    b_all = _dot(tril, log_g, precision=HIGHEST)

    row_c = lax.broadcasted_iota(jnp.int32, (C, GLA_DK), 0)
    row_s = lax.broadcasted_iota(jnp.int32, (SUB, GLA_DK), 0)
    lane_s = lax.broadcasted_iota(jnp.int32, (SUB, C), 1)
    neg_inf = jnp.float32(-jnp.inf)
    nw = nw_ref[...]

    for h in range(GLA_HEADS):
        kc = slice(h * GLA_DK, (h + 1) * GLA_DK)
        vc = slice(h * GLA_DV, (h + 1) * GLA_DV)
        b = b_all[:, kc]
        q = q_ref[:, kc] * (GLA_DK ** -0.5)
        k = k_ref[:, kc]
        v = v_ref[:, vc]
        v16 = v.astype(BF16)
        st = state_ref[h]
        b_last = b[C - 1:C, :]

        o = _dot_nt((q * jnp.exp(b)).astype(BF16), st.astype(BF16))

        s_rows = []
        for I in range(nsub):
            rs = slice(I * SUB, (I + 1) * SUB)
            q_i, k_i, b_i = q[rs], k[rs], b[rs]
            if I == 0:
                s_i = jnp.zeros((SUB, C), F32)
            else:
                ref_b = b[I * SUB - 1:I * SUB, :]
                qs = q_i * jnp.exp(b_i - ref_b)
                ks = k * jnp.exp(jnp.where(row_c < I * SUB, ref_b - b, neg_inf))
                s_i = _dot_nt(qs.astype(BF16), ks.astype(BF16))
            for j in range(SUB):
                diff = jnp.where(row_s >= j, b_i - b_i[j:j + 1, :], neg_inf)
                colv = jnp.sum(q_i * k_i[j:j + 1, :] * jnp.exp(diff), axis=1, keepdims=True)
                s_i = jnp.where(lane_s == I * SUB + j, colv, s_i)
            s_rows.append(s_i)
        s_full = jnp.concatenate(s_rows, axis=0)
        o = o + _dot(s_full.astype(BF16), v16)

        kd = k * jnp.exp(b_last - b)
        state_ref[h] = st * jnp.exp(b_last) + _dot(v.T.astype(BF16), kd.astype(BF16))

        ms = jnp.mean(o * o, axis=-1, keepdims=True)
        y = o * lax.rsqrt(ms + RMS_EPS) * nw
        g = gr_ref[:, vc]
        o_ref[:, vc] = (y * (g * jax.nn.sigmoid(g))).astype(o_ref.dtype)


def _gla(proj, gate_w, gate_b, norm_w, B, S):
    T = B * S
    NC = S // GLA_CHUNK
    gw = jnp.zeros((LANES, GLA_KEY_DIM), F32).at[:GLA_GATE_RANK].set(gate_w)
    q_blk = (3 * D_MODEL) // GLA_KEY_DIM
    v_blk = (3 * D_MODEL + 2 * GLA_KEY_DIM) // GLA_VAL_DIM
    return pl.pallas_call(
        _gla_kernel,
        out_shape=jax.ShapeDtypeStruct((T, GLA_VAL_DIM), BF16),
        grid=(B, NC),
        in_specs=[
            pl.BlockSpec((GLA_CHUNK, GLA_KEY_DIM), lambda b, c: (b * NC + c, q_blk)),
            pl.BlockSpec((GLA_CHUNK, GLA_KEY_DIM), lambda b, c: (b * NC + c, q_blk + 1)),
            pl.BlockSpec((GLA_CHUNK, GLA_VAL_DIM), lambda b, c: (b * NC + c, v_blk)),
            pl.BlockSpec((GLA_CHUNK, GLA_VAL_DIM), lambda b, c: (b * NC + c, v_blk + 1)),
            pl.BlockSpec((GLA_CHUNK, LANES), lambda b, c: (b * NC + c, COL_GLOW // LANES)),
            pl.BlockSpec((LANES, GLA_KEY_DIM), lambda b, c: (0, 0)),
            pl.BlockSpec((1, GLA_KEY_DIM), lambda b, c: (0, 0)),
            pl.BlockSpec((1, GLA_DV), lambda b, c: (0, 0)),
        ],
        out_specs=pl.BlockSpec((GLA_CHUNK, GLA_VAL_DIM), lambda b, c: (b * NC + c, 0)),
        scratch_shapes=[pltpu.VMEM((GLA_HEADS, GLA_DV, GLA_DK), F32)],
        compiler_params=_cparams("parallel", "arbitrary"),
        name="gla",
    )(proj, proj, proj, proj, proj, gw, gate_b.reshape(1, GLA_KEY_DIM), norm_w.reshape(1, GLA_DV))


def _merge_kernel(oa_ref, ob_ref, ga_ref, gb_ref, x_ref, wa_ref, wb_ref, wo_ref, bma_ref, bmb_ref,
                  g1_ref, sc2_ref, sh2_ref, nw_ref, wrh_ref, wrl_ref, br_ref,
                  x1_ref, h2_ref, e4_ref, w4_ref, p4_ref, cnt_ref, run_ref):
    tm = MERGE_TM

    @pl.when(pl.program_id(0) == 0)
    def _():
        run_ref[...] = jnp.zeros_like(run_ref)

    y_a = _dot(oa_ref[...], wa_ref[...])
    y_b = _dot(ob_ref[...], wb_ref[...])
    g_a = jax.nn.sigmoid(ga_ref[...] + bma_ref[...])
    g_b = jax.nn.sigmoid(gb_ref[...] + bmb_ref[...])
    mix = _dot((g_a * y_a + g_b * y_b).astype(BF16), wo_ref[...])
    x1 = x_ref[...] + g1_ref[...] * mix
    x1_ref[...] = x1

    ms = jnp.mean(x1 * x1, axis=-1, keepdims=True)
    h2 = x1 * lax.rsqrt(ms + RMS_EPS) * nw_ref[...] * (1.0 + sc2_ref[...]) + sh2_ref[...]
    for s in range(D_MODEL // LANES):
        h2_ref[pl.ds(s, tm, stride=SUBLANES), :] = h2[:, s * LANES:(s + 1) * LANES]

    h_hi = h2.astype(BF16)
    h_lo = (h2 - h_hi.astype(F32)).astype(BF16)
    logits = (_dot(h_hi, wrh_ref[...]) + (_dot(h_lo, wrh_ref[...]) + _dot(h_hi, wrl_ref[...]))) + br_ref[...]
    lane = lax.broadcasted_iota(jnp.int32, (tm, LANES), 1)
    lane_f = lane.astype(F32)
    neg_inf = jnp.float32(-jnp.inf)
    g = jnp.where(lane < N_EXPERTS, logits, neg_inf)
    picks, tops = [], []
    for _ in range(TOP_K):
        mx = jnp.max(g, axis=1, keepdims=True)
        idx = _first_max_lane(g, lane_f)
        pick = lane_f == idx
        picks.append((pick, idx))
        tops.append(mx)
        g = jnp.where(pick, neg_inf, g)
    ex = [jnp.exp(t - tops[0]) for t in tops]
    denom = ex[0] + ex[1] + ex[2] + ex[3]

    sel = sum(jnp.where(pick, 1.0, 0.0) for pick, _ in picks)
    r_i = lax.broadcasted_iota(jnp.int32, (tm, tm), 0)
    c_i = lax.broadcasted_iota(jnp.int32, (tm, tm), 1)
    lower = jnp.where(c_i < r_i, 1.0, 0.0).astype(BF16)
    rank = run_ref[0:1, :] + _dot(lower, sel.astype(BF16))
    run_new = run_ref[0:1, :] + jnp.sum(sel, axis=0, keepdims=True)
    run_ref[...] = jnp.broadcast_to(run_new, run_ref.shape)
    cnt_ref[...] = jnp.broadcast_to(run_new, cnt_ref.shape)

    e4 = jnp.zeros((tm, LANES), jnp.int32)
    w4 = jnp.zeros((tm, LANES), F32)
    p4 = jnp.zeros((tm, LANES), F32)
    for r in range(TOP_K):
        pick, idx = picks[r]
        pos = jnp.sum(jnp.where(pick, rank, 0.0), axis=1, keepdims=True)
        e4 = jnp.where(lane == r, idx.astype(jnp.int32), e4)
        w4 = jnp.where(lane == r, ex[r] / denom, w4)
        p4 = jnp.where(lane == r, pos, p4)
    e4_ref[...] = e4
    w4_ref[...] = w4
    p4_ref[...] = p4


def _merge_and_route(o_a, o_b, proj, x2, w_a, w_b, w_o, b_merge, mod, norm2_w, w_router, b_router, S):
    T = x2.shape[0]
    tm = MERGE_TM
    tiles_per_batch = S // tm
    full = lambda shape: pl.BlockSpec(shape, lambda i: tuple(0 for _ in shape))
    row = lambda width, col: pl.BlockSpec((tm, width), lambda i: (i, col))
    modv = lambda k: pl.BlockSpec((None, 1, D_MODEL), lambda i: (i // tiles_per_batch, 0, k))
    wr = jnp.zeros((D_MODEL, LANES), F32).at[:, :N_EXPERTS].set(w_router)
    wr_hi = wr.astype(BF16)
    wr_lo = (wr - wr_hi.astype(F32)).astype(BF16)
    br = jnp.zeros((1, LANES), F32).at[0, :N_EXPERTS].set(b_router)
    bm = b_merge.reshape(1, 2 * D_MODEL)
    return pl.pallas_call(
        _merge_kernel,
        out_shape=(
            jax.ShapeDtypeStruct((T, D_MODEL), F32),
            jax.ShapeDtypeStruct((T * SUBLANES, LANES), F32),
            jax.ShapeDtypeStruct((T, LANES), jnp.int32),
            jax.ShapeDtypeStruct((T, LANES), F32),
            jax.ShapeDtypeStruct((T, LANES), F32),
            jax.ShapeDtypeStruct((SUBLANES, LANES), F32),
        ),
        grid=(T // tm,),
        in_specs=[
            row(D_MODEL, 0), row(D_MODEL, 0),
            row(D_MODEL, COL_GATES // D_MODEL), row(D_MODEL, COL_GATES // D_MODEL + 1),
            row(D_MODEL, 0),
            full((D_MODEL, D_MODEL)), full((D_MODEL, D_MODEL)), full((D_MODEL, D_MODEL)),
            pl.BlockSpec((1, D_MODEL), lambda i: (0, 0)), pl.BlockSpec((1, D_MODEL), lambda i: (0, 1)),
            modv(2), modv(4), modv(3),
            full((1, D_MODEL)), full((D_MODEL, LANES)), full((D_MODEL, LANES)), full((1, LANES)),
        ],
        out_specs=(
            row(D_MODEL, 0),
            pl.BlockSpec((tm * SUBLANES, LANES), lambda i: (i, 0)),
            row(LANES, 0), row(LANES, 0), row(LANES, 0),
            full((SUBLANES, LANES)),
        ),
        scratch_shapes=[pltpu.VMEM((SUBLANES, LANES), F32)],
        compiler_params=_cparams("arbitrary"),
        name="merge_route",
    )(o_a, o_b, proj, proj, x2, w_a, w_b, w_o, bm, bm, mod, mod, mod,
      norm2_w.reshape(1, D_MODEL), wr_hi, wr_lo, br)


def _row_slab(ref, row):
    return ref.at[pl.ds(pl.multiple_of(row * SUBLANES, SUBLANES), SUBLANES), :]


def _wait_slabs(src_ref, dst_ref, sem, count):
    def wait(a, c):
        pltpu.make_async_copy(_row_slab(src_ref, 0), _row_slab(dst_ref, 0), sem).wait()
        return c

    lax.fori_loop(0, count, wait, 0, unroll=DMA_WAIT_UNROLL)


def _dispatch_kernel(dest_ref, h_ref, xs_in_ref, xs_ref, sem):
    del xs_in_ref
    tm = COMBINE_TM
    base = pl.program_id(0) * (tm * TOP_K)

    def start(t, c):
        src = _row_slab(h_ref, t)
        for kk in range(TOP_K):
            pltpu.make_async_copy(src, _row_slab(xs_ref, dest_ref[base + t * TOP_K + kk]), sem).start(
                priority=kk % DMA_PRIORITIES)
        return c

    lax.fori_loop(0, tm, start, 0, unroll=DMA_ISSUE_UNROLL)
    _wait_slabs(h_ref, xs_ref, sem, tm * TOP_K)


def _dispatch(dest, h2_slabs, P):
    T = h2_slabs.shape[0] // SUBLANES
    tm = COMBINE_TM
    xs0 = jnp.zeros((P * SUBLANES, LANES), F32)
    return pl.pallas_call(
        _dispatch_kernel,
        out_shape=jax.ShapeDtypeStruct((P * SUBLANES, LANES), F32),
        grid_spec=pltpu.PrefetchScalarGridSpec(
            num_scalar_prefetch=1,
            grid=(T // tm,),
            in_specs=[
                pl.BlockSpec((tm * SUBLANES, LANES), lambda i, dest: (i, 0)),
                pl.BlockSpec(memory_space=pl.ANY),
            ],
            out_specs=pl.BlockSpec(memory_space=pl.ANY),
            scratch_shapes=[pltpu.SemaphoreType.DMA],
        ),
        input_output_aliases={2: 0},
        compiler_params=_cparams("arbitrary"),
        name="moe_dispatch",
    )(dest, h2_slabs, xs0)


def _expert_kernel(be_ref, nu_ref, xs_ref, wi_ref, bi_ref, wo_ref, bo_ref, perm_ref, ys_ref,
                   x_ref, wi16_ref, wo16_ref):
    R = EXPERT_ROWS
    j = pl.program_id(0)
    used = j < nu_ref[0]
    new_expert = (j == 0) | (be_ref[j] != be_ref[jnp.maximum(j - 1, 0)])

    @pl.when(used & new_expert)
    def _():
        perm = perm_ref[...]
        for g in range(D_MODEL // LANES):
            w = wi_ref[:, 2 * LANES * g:2 * LANES * (g + 1)].astype(BF16)
            sep = _dot(w, perm).astype(BF16)
            wi16_ref[:, LANES * g:LANES * (g + 1)] = sep[:, :LANES]
            wi16_ref[:, D_MODEL + LANES * g:D_MODEL + LANES * (g + 1)] = sep[:, LANES:]
        wo16_ref[...] = wo_ref[...].astype(BF16)

    @pl.when(used)
    def _():
        for s in range(D_MODEL // LANES):
            x_ref[:, s * LANES:(s + 1) * LANES] = xs_ref[pl.ds(s, R, stride=SUBLANES), :].astype(BF16)
        hid = _dot(x_ref[...], wi16_ref[...]) + bi_ref[...]
        glu = jnp.minimum(hid[:, :D_MODEL], SWIGLU_LIMIT)
        lin = jnp.clip(hid[:, D_MODEL:], -SWIGLU_LIMIT, SWIGLU_LIMIT)
        act = glu * jax.nn.sigmoid(SWIGLU_ALPHA * glu) * (lin + 1.0)
        out = _dot(act.astype(BF16), wo16_ref[...]) + bo_ref[...]
        for s in range(D_MODEL // LANES):
            ys_ref[pl.ds(s, R, stride=SUBLANES), :] = out[:, s * LANES:(s + 1) * LANES]

    @pl.when(j >= nu_ref[0])
    def _():
        ys_ref[...] = jnp.zeros_like(ys_ref)


def _experts(block_expert, n_used, xs, w_in, b_in, w_out, b_out):
    R = EXPERT_ROWS
    n_blocks = xs.shape[0] // (R * SUBLANES)
    src = jnp.arange(2 * LANES)[:, None]
    dst = jnp.arange(2 * LANES)[None, :]
    perm = jnp.where(src == jnp.where(dst < LANES, 2 * dst, 2 * (dst - LANES) + 1), 1.0, 0.0).astype(BF16)
    return pl.pallas_call(
        _expert_kernel,
        out_shape=jax.ShapeDtypeStruct(xs.shape, F32),
        grid_spec=pltpu.PrefetchScalarGridSpec(
            num_scalar_prefetch=2,
            grid=(n_blocks,),
            in_specs=[
                pl.BlockSpec((R * SUBLANES, LANES), lambda j, be, nu: (jnp.minimum(j, jnp.maximum(nu[0] - 1, 0)), 0)),
                pl.BlockSpec((None, D_MODEL, 2 * D_MODEL), lambda j, be, nu: (be[j], 0, 0)),
                pl.BlockSpec((None, 1, 2 * D_MODEL), lambda j, be, nu: (be[j], 0, 0)),
                pl.BlockSpec((None, D_MODEL, D_MODEL), lambda j, be, nu: (be[j], 0, 0)),
                pl.BlockSpec((None, 1, D_MODEL), lambda j, be, nu: (be[j], 0, 0)),
                pl.BlockSpec((2 * LANES, 2 * LANES), lambda j, be, nu: (0, 0)),
            ],
            out_specs=pl.BlockSpec((R * SUBLANES, LANES), lambda j, be, nu: (j, 0)),
            scratch_shapes=[
                pltpu.VMEM((R, D_MODEL), BF16),
                pltpu.VMEM((D_MODEL, 2 * D_MODEL), BF16),
                pltpu.VMEM((D_MODEL, D_MODEL), BF16),
            ],
        ),
        compiler_params=pltpu.CompilerParams(
            dimension_semantics=("arbitrary",), vmem_limit_bytes=EXPERT_VMEM_LIMIT),
        name="moe_experts",
    )(block_expert, n_used, xs, w_in, b_in, w_out, b_out, perm)


def _combine_kernel(dest_ref, ys_ref, w4_ref, x1_ref, g2_ref, nw_ref, o_ref, buf_ref, sem):
    tm = COMBINE_TM
    base = pl.program_id(0) * (tm * TOP_K)

    def start(t, c):
        for kk in range(TOP_K):
            pltpu.make_async_copy(_row_slab(ys_ref, dest_ref[base + t * TOP_K + kk]),
                                  _row_slab(buf_ref.at[kk], t), sem).start(priority=kk % DMA_PRIORITIES)
        return c

    lax.fori_loop(0, tm, start, 0, unroll=DMA_ISSUE_UNROLL)
    _wait_slabs(ys_ref, buf_ref.at[0], sem, tm * TOP_K)

    w4 = w4_ref[...]
    g2 = g2_ref[...]
    parts = []
    ssq = jnp.zeros((tm, 1), F32)
    for s in range(D_MODEL // LANES):
        cols = slice(s * LANES, (s + 1) * LANES)
        y = jnp.zeros((tm, LANES), F32)
        for kk in range(TOP_K):
            y = y + buf_ref[kk, pl.ds(s, tm, stride=SUBLANES), :] * w4[:, kk:kk + 1]
        x2 = x1_ref[:, cols] + g2[:, cols] * y
        ssq = ssq + jnp.sum(x2 * x2, axis=1, keepdims=True)
        parts.append(x2)
    inv = lax.rsqrt(ssq / D_MODEL + RMS_EPS)
    nw = nw_ref[...]
    for s in range(D_MODEL // LANES):
        cols = slice(s * LANES, (s + 1) * LANES)
        o_ref[:, cols] = parts[s] * inv * nw[:, cols]


def _combine(dest, ys, w4, x1, mod, final_w, S):
    T = x1.shape[0]
    tm = COMBINE_TM
    tiles_per_batch = S // tm
    return pl.pallas_call(
        _combine_kernel,
        out_shape=jax.ShapeDtypeStruct((T, D_MODEL), F32),
        grid_spec=pltpu.PrefetchScalarGridSpec(
            num_scalar_prefetch=1,
            grid=(T // tm,),
            in_specs=[
                pl.BlockSpec(memory_space=pl.ANY),
                pl.BlockSpec((tm, LANES), lambda i, dest: (i, 0)),
                pl.BlockSpec((tm, D_MODEL), lambda i, dest: (i, 0)),
                pl.BlockSpec((None, 1, D_MODEL), lambda i, dest: (i // tiles_per_batch, 0, 5)),
                pl.BlockSpec((1, D_MODEL), lambda i, dest: (0, 0)),
            ],
            out_specs=pl.BlockSpec((tm, D_MODEL), lambda i, dest: (i, 0)),
            scratch_shapes=[
                pltpu.VMEM((TOP_K, tm * SUBLANES, LANES), F32),
                pltpu.SemaphoreType.DMA,
            ],
        ),
        compiler_params=_cparams("arbitrary"),
        name="moe_combine",
    )(dest, ys, w4, x1, mod, final_w.reshape(1, D_MODEL))


def _layer(x2, c_mod, B, S, norm1_w, w_in, gla_gate_w, gla_gate_b, gla_norm_w, w_o_moba, w_o_gla,
           b_merge, w_out, norm2_w, w_router, b_router, w_exp_in, b_exp_in, w_exp_out, b_exp_out):
    T = B * S
    glow_lo = N_MAIN
    glow_hi = N_MAIN + GLA_GATE_RANK
    w_proj = jnp.concatenate(
        [w_in[:, :glow_lo], w_in[:, glow_hi:], w_in[:, glow_lo:glow_hi],
         jnp.zeros((D_MODEL, LANES - GLA_GATE_RANK), F32)], axis=1).astype(BF16)
    proj = _input_projection(x2, norm1_w, c_mod, w_proj, S)

    tables = _rope_tables(S)
    kaug, vb, kmean = _moba_prep(proj, tables, B, S)
    NB = S // MOBA_BLOCK
    kmean = kmean[:, :NB].reshape(B, NB, MOBA_HEADS, MOBA_HEAD_DIM).transpose(0, 2, 1, 3)
    kmean = jnp.pad(kmean, ((0, 0), (0, 0), (0, LANES - NB), (0, 0)))
    qaug = _moba_qprep(proj, tables, kmean, B, S)
    o_a = _moba_attention(qaug, kaug, vb, B, S)
    o_b = _gla(proj, gla_gate_w, gla_gate_b, gla_norm_w, B, S)

    x1, h2, e4, w4, p4, cnt = _merge_and_route(
        o_a, o_b, proj, x2, w_o_moba.astype(BF16), w_o_gla.astype(BF16), w_out.astype(BF16),
        b_merge, c_mod, norm2_w, w_router, b_router, S)

    R = EXPERT_ROWS
    counts = cnt[0, :N_EXPERTS].astype(jnp.int32)
    padded = (counts + R - 1) // R * R
    padded_end = jnp.cumsum(padded)
    padded_start = padded_end - padded
    n_blocks = -(-(T * TOP_K + N_EXPERTS * (R - 1)) // R)
    P = n_blocks * R
    dest = (padded_start[e4[:, :TOP_K]] + p4[:, :TOP_K].astype(jnp.int32)).reshape(-1)
    block_row0 = jnp.arange(n_blocks, dtype=jnp.int32) * R
    block_expert = jnp.minimum(
        jnp.sum(padded_end[None, :] <= block_row0[:, None], axis=1), N_EXPERTS - 1).astype(jnp.int32)
    n_used = (padded_end[-1:] // R).astype(jnp.int32)

    xs = _dispatch(dest, h2, P)
    b_ei = jnp.concatenate([b_exp_in[:, 0::2], b_exp_in[:, 1::2]], axis=-1)[:, None, :]
    ys = _experts(block_expert, n_used, xs, w_exp_in, b_ei, w_exp_out, b_exp_out[:, None, :])
    return x1, ys, dest, w4


def kernel(x, c, w_ada, b_ada, norm1_w, w_in, gla_gate_w, gla_gate_b, gla_norm_w, w_o_moba, w_o_gla, b_merge, w_out, norm2_w, w_router, b_router, w_exp_in, b_exp_in, w_exp_out, b_exp_out, final_norm_w):
    B, S, _ = x.shape
    depth = w_ada.shape[0]
    assert depth == 1, "the combine kernel fuses the final norm, so a single layer is supported"
    x2 = x.reshape(B * S, D_MODEL)
    l = 0
    mod = _modulation(c, w_ada[l], b_ada[l])
    x1, ys, dest, w4 = _layer(
        x2, mod, B, S, norm1_w[l], w_in[l], gla_gate_w[l], gla_gate_b[l], gla_norm_w[l],
        w_o_moba[l], w_o_gla[l], b_merge[l], w_out[l], norm2_w[l], w_router[l], b_router[l],
        w_exp_in[l], b_exp_in[l], w_exp_out[l], b_exp_out[l])
    out = _combine(dest, ys, w4, x1, mod, final_norm_w, S)
    return out.reshape(B, S, D_MODEL)
```

```python
import functools

import jax
import jax.numpy as jnp
from jax import lax
from jax.experimental import pallas as pl
from jax.experimental.pallas import tpu as pltpu

F32 = jnp.float32
BF16 = jnp.bfloat16
HIGHEST = lax.Precision.HIGHEST

D_MODEL = 1024
MOBA_HEADS = 8
MOBA_HEAD_DIM = 128
MOBA_BLOCK = 256
MOBA_TOPK = 3
ROPE_DIM = MOBA_HEAD_DIM // 4
ROPE_THETA = 500000.0
GLA_HEADS = 4
GLA_KEY_DIM = D_MODEL // 2
GLA_VAL_DIM = D_MODEL
GLA_DK = GLA_KEY_DIM // GLA_HEADS
GLA_DV = GLA_VAL_DIM // GLA_HEADS
GLA_GATE_RANK = 16
GLA_GATE_NORMALIZER = 16.0
N_EXPERTS = 32
TOP_K = 4
SWIGLU_ALPHA = 1.702
SWIGLU_LIMIT = 7.0
RMS_EPS = 1e-5
LOG2E = 1.4426950408889634

LANES = 128
SUBLANES = 8
BF16_SUBLANES = 16
VMEM_LIMIT = 48 * 1024 * 1024
EXPERT_VMEM_LIMIT = 56 * 1024 * 1024

INPROJ_TM = 1024
INPROJ_TN = 1664
MOBA_GROUP = 4
MOBA_SUM_ROWS = BF16_SUBLANES
GLA_CHUNK = 64
GLA_SUB = 16
MERGE_TM = 256
EXPERT_ROWS = 256
COMBINE_TM = 256
MASK_NEG = -1e30
DMA_ISSUE_UNROLL = 4
DMA_WAIT_UNROLL = 16
DMA_PRIORITIES = 2

N_MAIN = 3 * D_MODEL + 2 * GLA_KEY_DIM + 2 * GLA_VAL_DIM
COL_GATES = N_MAIN
COL_GLOW = N_MAIN + 2 * D_MODEL
N_PROJ = COL_GLOW + LANES


def _dot(a, b, **kw):
    return jnp.dot(a, b, preferred_element_type=F32, **kw)


def _dot_nt(a, b, **kw):
    return lax.dot_general(a, b, (((1,), (1,)), ((), ())), preferred_element_type=F32, **kw)


def _truncate_to_bf16_bits(x, bitcast):
    return bitcast(bitcast(x, jnp.uint32) & jnp.uint32(0xFFFF0000), F32)


def _first_max_lane(g, lane_f):
    mx = jnp.max(g, axis=1, keepdims=True)
    return jnp.min(jnp.where(g == mx, lane_f, float(LANES)), axis=1, keepdims=True)


def _cparams(*sem):
    return pltpu.CompilerParams(dimension_semantics=sem, vmem_limit_bytes=VMEM_LIMIT)


def _mod_kernel(c_ref, w_ref, b_ref, o_ref):
    c = c_ref[...]
    ca = c * jax.nn.sigmoid(c)
    o_ref[...] = _dot(ca, w_ref[...], precision=HIGHEST) + b_ref[...]


def _modulation(c, w_ada, b_ada):
    B = c.shape[0]
    n = w_ada.shape[1]
    c8 = jnp.zeros((SUBLANES, D_MODEL), F32).at[:B].set(c)
    out = pl.pallas_call(
        _mod_kernel,
        out_shape=jax.ShapeDtypeStruct((SUBLANES, n), F32),
        grid=(n // D_MODEL,),
        in_specs=[
            pl.BlockSpec((SUBLANES, D_MODEL), lambda j: (0, 0)),
            pl.BlockSpec((D_MODEL, D_MODEL), lambda j: (0, j)),
            pl.BlockSpec((1, D_MODEL), lambda j: (0, j)),
        ],
        out_specs=pl.BlockSpec((SUBLANES, D_MODEL), lambda j: (0, j)),
        compiler_params=_cparams("arbitrary"),
        name="adaln_mod",
    )(c8, w_ada, b_ada.reshape(1, n))
    return out[:B].reshape(B, 1, n)


def _inproj_kernel(x_ref, nw_ref, sc_ref, sh_ref, w_ref, o_ref, h_ref):
    @pl.when(pl.program_id(1) == 0)
    def _():
        x = x_ref[...]
        ms = jnp.mean(x * x, axis=-1, keepdims=True)
        y = x * lax.rsqrt(ms + RMS_EPS) * nw_ref[...]
        h_ref[...] = (y * (1.0 + sc_ref[...]) + sh_ref[...]).astype(BF16)

    o_ref[...] = _dot(h_ref[...], w_ref[...])


def _input_projection(x2, norm_w, mod, w_proj, S):
    T = x2.shape[0]
    tiles_per_batch = S // INPROJ_TM
    return pl.pallas_call(
        _inproj_kernel,
        out_shape=jax.ShapeDtypeStruct((T, N_PROJ), F32),
        grid=(T // INPROJ_TM, N_PROJ // INPROJ_TN),
        in_specs=[
            pl.BlockSpec((INPROJ_TM, D_MODEL), lambda i, j: (i, 0)),
            pl.BlockSpec((1, D_MODEL), lambda i, j: (0, 0)),
            pl.BlockSpec((None, 1, D_MODEL), lambda i, j: (i // tiles_per_batch, 0, 1)),
            pl.BlockSpec((None, 1, D_MODEL), lambda i, j: (i // tiles_per_batch, 0, 0)),
            pl.BlockSpec((D_MODEL, INPROJ_TN), lambda i, j: (0, j)),
        ],
        out_specs=pl.BlockSpec((INPROJ_TM, INPROJ_TN), lambda i, j: (i, j)),
        scratch_shapes=[pltpu.VMEM((INPROJ_TM, D_MODEL), BF16)],
        compiler_params=_cparams("parallel", "arbitrary"),
        name="inproj",
    )(x2, norm_w.reshape(1, D_MODEL), mod, mod, w_proj)


def _rope_tables(S):
    half = ROPE_DIM // 2
    inv_freq = jnp.float32(ROPE_THETA) ** (-jnp.arange(half, dtype=jnp.float32) * 2.0 / ROPE_DIM)
    ang = jnp.arange(S, dtype=jnp.float32)[:, None] * inv_freq[None, :]
    cos, sin = jnp.cos(ang), jnp.sin(ang)
    ones = jnp.ones((S, MOBA_HEAD_DIM - ROPE_DIM), F32)
    zeros_hi = jnp.zeros((S, MOBA_HEAD_DIM - half), F32)
    zeros_lo = jnp.zeros((S, half), F32)
    cos_f = jnp.concatenate([cos, cos, ones], axis=1)
    sin_a = jnp.concatenate([-sin, zeros_hi], axis=1)
    sin_b = jnp.concatenate([zeros_lo, sin, ones * 0.0], axis=1)
    return cos_f, sin_a, sin_b


def _rope(x, cos_f, sin_a, sin_b):
    half = ROPE_DIM // 2
    up = pltpu.roll(x, MOBA_HEAD_DIM - half, 1)
    dn = pltpu.roll(x, half, 1)
    return x * cos_f + up * sin_a + dn * sin_b


def _kprep_kernel(k_ref, v_ref, cos_ref, sa_ref, sb_ref, kaug_ref, vb_ref, kmean_ref):
    n = pl.program_id(1)
    cos_f, sin_a, sin_b = cos_ref[...], sa_ref[...], sb_ref[...]
    lane = lax.broadcasted_iota(jnp.int32, (MOBA_BLOCK, LANES), 1)
    onehot = jnp.where(lane == n, 1.0, 0.0).astype(BF16)
    for h in range(MOBA_HEADS):
        cols = slice(h * MOBA_HEAD_DIM, (h + 1) * MOBA_HEAD_DIM)
        kr = _rope(k_ref[:, cols], cos_f, sin_a, sin_b)
        kmean_ref[:, cols] = jnp.mean(kr, axis=0, keepdims=True)
        kaug_ref[h, :, :MOBA_HEAD_DIM] = kr.astype(BF16)
        kaug_ref[h, :, MOBA_HEAD_DIM:] = onehot
        vb_ref[h, :MOBA_HEAD_DIM, :] = v_ref[:, cols].T.astype(BF16)
        vb_ref[h, MOBA_HEAD_DIM:, :] = jnp.ones((MOBA_SUM_ROWS, MOBA_BLOCK), BF16)


def _moba_prep(proj, tables, B, S):
    NB = S // MOBA_BLOCK
    NBP = NB
    src = lambda n: n
    tab_spec = pl.BlockSpec((MOBA_BLOCK, LANES), lambda b, n: (src(n), 0))
    return pl.pallas_call(
        _kprep_kernel,
        out_shape=(
            jax.ShapeDtypeStruct((B, MOBA_HEADS, NBP * MOBA_BLOCK, 2 * MOBA_HEAD_DIM), BF16),
            jax.ShapeDtypeStruct((B, MOBA_HEADS, MOBA_HEAD_DIM + MOBA_SUM_ROWS, NBP * MOBA_BLOCK), BF16),
            jax.ShapeDtypeStruct((B, NBP, 1, D_MODEL), F32),
        ),
        grid=(B, NBP),
        in_specs=[
            pl.BlockSpec((MOBA_BLOCK, D_MODEL), lambda b, n: (b * NB + src(n), 1)),
            pl.BlockSpec((MOBA_BLOCK, D_MODEL), lambda b, n: (b * NB + src(n), 2)),
            tab_spec, tab_spec, tab_spec,
        ],
        out_specs=(
            pl.BlockSpec((None, MOBA_HEADS, MOBA_BLOCK, 2 * MOBA_HEAD_DIM), lambda b, n: (b, 0, n, 0)),
            pl.BlockSpec((None, MOBA_HEADS, MOBA_HEAD_DIM + MOBA_SUM_ROWS, MOBA_BLOCK), lambda b, n: (b, 0, 0, n)),
            pl.BlockSpec((None, None, 1, D_MODEL), lambda b, n: (b, n, 0, 0)),
        ),
        compiler_params=_cparams("parallel", "parallel"),
        name="moba_kprep",
    )(proj, proj, *tables)


def _qprep_kernel(q_ref, cos_ref, sa_ref, sb_ref, kmean_ref, qaug_ref):
    i = pl.program_id(1)
    blk = MOBA_BLOCK
    scale = MOBA_HEAD_DIM ** -0.5
    cos_f, sin_a, sin_b = cos_ref[...], sa_ref[...], sb_ref[...]
    nbp = kmean_ref.shape[1]
    blk_id = lax.broadcasted_iota(jnp.int32, (nbp, blk), 0)
    blk_f = blk_id.astype(F32)
    neg_inf = jnp.float32(-jnp.inf)
    for h in range(MOBA_HEADS):
        cols = slice(h * MOBA_HEAD_DIM, (h + 1) * MOBA_HEAD_DIM)
        qr = _rope(q_ref[:, cols], cos_f, sin_a, sin_b)
        gate = _dot_nt(kmean_ref[h], qr, precision=HIGHEST)
        g = jnp.where(blk_id < i, gate, neg_inf)
        bias = jnp.where(blk_id == i, 0.0, MASK_NEG)
        for r in range(MOBA_TOPK):
            mx = jnp.max(g, axis=0, keepdims=True)
            first = jnp.min(jnp.where(g == mx, blk_f, float(nbp)), axis=0, keepdims=True)
            pick = blk_f == first
            unmask = jnp.where(i > r, 0.0, MASK_NEG)
            bias = jnp.where(pick, jnp.maximum(bias, unmask), bias)
            g = jnp.where(pick, neg_inf, g)
        qaug_ref[h, :MOBA_HEAD_DIM, :] = (qr * (scale * LOG2E)).T.astype(BF16)
        qaug_ref[h, MOBA_HEAD_DIM:MOBA_HEAD_DIM + nbp, :] = bias.astype(BF16)
        qaug_ref[h, MOBA_HEAD_DIM + nbp:, :] = jnp.zeros((MOBA_HEAD_DIM - nbp, blk), BF16)


def _moba_qprep(proj, tables, kmean, B, S):
    NB = S // MOBA_BLOCK
    tab_spec = pl.BlockSpec((MOBA_BLOCK, LANES), lambda b, i: (i, 0))
    return pl.pallas_call(
        _qprep_kernel,
        out_shape=jax.ShapeDtypeStruct((B, MOBA_HEADS, 2 * MOBA_HEAD_DIM, S), BF16),
        grid=(B, NB),
        in_specs=[
            pl.BlockSpec((MOBA_BLOCK, D_MODEL), lambda b, i: (b * NB + i, 0)),
            tab_spec, tab_spec, tab_spec,
            pl.BlockSpec((None, MOBA_HEADS, kmean.shape[2], MOBA_HEAD_DIM), lambda b, i: (b, 0, 0, 0)),
        ],
        out_specs=pl.BlockSpec((None, MOBA_HEADS, 2 * MOBA_HEAD_DIM, MOBA_BLOCK), lambda b, i: (b, 0, 0, i)),
        compiler_params=_cparams("parallel", "parallel"),
        name="moba_qprep",
    )(proj, *tables, kmean)


def _moba_kernel(q_ref, kaug_ref, v_ref, o_ref, m_ref, acc_ref, s_ref, mx_ref):
    n_full = pl.program_id(2)
    width = MOBA_GROUP * MOBA_BLOCK
    qt = q_ref[...]

    def scores(g):
        return _dot(kaug_ref[pl.ds(pl.multiple_of(g * width, width), width), :], qt)

    def prob_value(s, m, g):
        p = jnp.exp2(s - m).astype(BF16)
        return _dot(v_ref[:, pl.ds(pl.multiple_of(g * width, width), width)], p)

    def stage(g):
        s = scores(g)
        s_ref[...] = s
        mx_ref[...] = jnp.max(s, axis=0, keepdims=True)

    s_own = scores(n_full)
    stage(0)
    key = lax.broadcasted_iota(jnp.int32, (width, width), 0)
    qry = lax.broadcasted_iota(jnp.int32, (width, width), 1)
    s_own = jnp.where(key > qry, jnp.float32(-jnp.inf), s_own)
    m0 = jnp.max(s_own, axis=0, keepdims=True)
    m_ref[...] = m0
    acc_ref[...] = prob_value(s_own, m0, n_full)

    def body(g, carry):
        m_prev = m_ref[...]
        m_new = jnp.maximum(m_prev, mx_ref[...])
        alpha = jnp.exp2(m_prev - m_new)
        m_ref[...] = m_new
        pv = prob_value(s_ref[...], m_new, g)
        stage(jnp.minimum(g + 1, jnp.maximum(n_full - 1, 0)))
        acc_ref[...] = alpha * acc_ref[...] + pv
        return carry

    lax.fori_loop(0, n_full, body, 0)
    acc = acc_ref[...]
    out_t = acc[:MOBA_HEAD_DIM, :] / acc[MOBA_HEAD_DIM:MOBA_HEAD_DIM + 1, :]
    o_ref[...] = out_t.T.astype(o_ref.dtype)


def _moba_attention(qaug, kaug, vb, B, S):
    T = B * S
    tq = MOBA_GROUP * MOBA_BLOCK
    assert S % tq == 0
    NG = S // tq
    return pl.pallas_call(
        _moba_kernel,
        out_shape=jax.ShapeDtypeStruct((T, D_MODEL), BF16),
        grid=(B, MOBA_HEADS, NG),
        in_specs=[
            pl.BlockSpec((None, None, 2 * MOBA_HEAD_DIM, tq), lambda b, h, j: (b, h, 0, j)),
            pl.BlockSpec((None, None, S, 2 * MOBA_HEAD_DIM), lambda b, h, j: (b, h, 0, 0)),
            pl.BlockSpec((None, None, MOBA_HEAD_DIM + MOBA_SUM_ROWS, S), lambda b, h, j: (b, h, 0, 0)),
        ],
        out_specs=pl.BlockSpec((tq, MOBA_HEAD_DIM), lambda b, h, j: (b * NG + j, h)),
        scratch_shapes=[
            pltpu.VMEM((1, tq), F32),
            pltpu.VMEM((MOBA_HEAD_DIM + MOBA_SUM_ROWS, tq), F32),
            pltpu.VMEM((tq, tq), F32),
            pltpu.VMEM((1, tq), F32),
        ],
        compiler_params=_cparams("parallel", "parallel", "arbitrary"),
        name="moba_attn",
    )(qaug, kaug, vb)


def _gla_kernel(q_ref, k_ref, v_ref, gr_ref, gl_ref, gw_ref, gb_ref, nw_ref, o_ref, state_ref):
    C, SUB = GLA_CHUNK, GLA_SUB
    nsub = C // SUB

    @pl.when(pl.program_id(1) == 0)
    def _():
        state_ref[...] = jnp.zeros_like(state_ref)

    z = _dot(gl_ref[...].astype(BF16), gw_ref[...].astype(BF16)) + gb_ref[...]
    log_g = jax.nn.log_sigmoid(z) / GLA_GATE_NORMALIZER
    r_i = lax.broadcasted_iota(jnp.int32, (C, C), 0)
    c_i = lax.broadcasted_iota(jnp.int32, (C, C), 1)
    tril = jnp.where(c_i <= r_i, 1.0, 0.0).astype(F32)
    b_all = _dot(tril, log_g, precision=HIGHEST)

    row_c = lax.broadcasted_iota(jnp.int32, (C, GLA_DK), 0)
    row_s = lax.broadcasted_iota(jnp.int32, (SUB, GLA_DK), 0)
    lane_s = lax.broadcasted_iota(jnp.int32, (SUB, C), 1)
    neg_inf = jnp.float32(-jnp.inf)
    nw = nw_ref[...]

    for h in range(GLA_HEADS):
        kc = slice(h * GLA_DK, (h + 1) * GLA_DK)
        vc = slice(h * GLA_DV, (h + 1) * GLA_DV)
        b = b_all[:, kc]
        q = q_ref[:, kc] * (GLA_DK ** -0.5)
        k = k_ref[:, kc]
        v = v_ref[:, vc]
        v16 = v.astype(BF16)
        st = state_ref[h]
        b_last = b[C - 1:C, :]

        o = _dot_nt((q * jnp.exp(b)).astype(BF16), st.astype(BF16))

        s_rows = []
        for I in range(nsub):
            rs = slice(I * SUB, (I + 1) * SUB)
            q_i, k_i, b_i = q[rs], k[rs], b[rs]
            if I == 0:
                s_i = jnp.zeros((SUB, C), F32)
            else:
                ref_b = b[I * SUB - 1:I * SUB, :]
                qs = q_i * jnp.exp(b_i - ref_b)
                ks = k * jnp.exp(jnp.where(row_c < I * SUB, ref_b - b, neg_inf))
                s_i = _dot_nt(qs.astype(BF16), ks.astype(BF16))
            for j in range(SUB):
                diff = jnp.where(row_s >= j, b_i - b_i[j:j + 1, :], neg_inf)
                colv = jnp.sum(q_i * k_i[j:j + 1, :] * jnp.exp(diff), axis=1, keepdims=True)
                s_i = jnp.where(lane_s == I * SUB + j, colv, s_i)
            s_rows.append(s_i)
        s_full = jnp.concatenate(s_rows, axis=0)
        o = o + _dot(s_full.astype(BF16), v16)

        kd = k * jnp.exp(b_last - b)
        state_ref[h] = st * jnp.exp(b_last) + _dot(v.T.astype(BF16), kd.astype(BF16))

        ms = jnp.mean(o * o, axis=-1, keepdims=True)
        y = o * lax.rsqrt(ms + RMS_EPS) * nw
        g = gr_ref[:, vc]
        o_ref[:, vc] = (y * (g * jax.nn.sigmoid(g))).astype(o_ref.dtype)


def _gla(proj, gate_w, gate_b, norm_w, B, S):
    T = B * S
    NC = S // GLA_CHUNK
    gw = jnp.zeros((LANES, GLA_KEY_DIM), F32).at[:GLA_GATE_RANK].set(gate_w)
    q_blk = (3 * D_MODEL) // GLA_KEY_DIM
    v_blk = (3 * D_MODEL + 2 * GLA_KEY_DIM) // GLA_VAL_DIM
    return pl.pallas_call(
        _gla_kernel,
        out_shape=jax.ShapeDtypeStruct((T, GLA_VAL_DIM), BF16),
        grid=(B, NC),
        in_specs=[
            pl.BlockSpec((GLA_CHUNK, GLA_KEY_DIM), lambda b, c: (b * NC + c, q_blk)),
            pl.BlockSpec((GLA_CHUNK, GLA_KEY_DIM), lambda b, c: (b * NC + c, q_blk + 1)),
            pl.BlockSpec((GLA_CHUNK, GLA_VAL_DIM), lambda b, c: (b * NC + c, v_blk)),
            pl.BlockSpec((GLA_CHUNK, GLA_VAL_DIM), lambda b, c: (b * NC + c, v_blk + 1)),
            pl.BlockSpec((GLA_CHUNK, LANES), lambda b, c: (b * NC + c, COL_GLOW // LANES)),
            pl.BlockSpec((LANES, GLA_KEY_DIM), lambda b, c: (0, 0)),
            pl.BlockSpec((1, GLA_KEY_DIM), lambda b, c: (0, 0)),
            pl.BlockSpec((1, GLA_DV), lambda b, c: (0, 0)),
        ],
        out_specs=pl.BlockSpec((GLA_CHUNK, GLA_VAL_DIM), lambda b, c: (b * NC + c, 0)),
        scratch_shapes=[pltpu.VMEM((GLA_HEADS, GLA_DV, GLA_DK), F32)],
        compiler_params=_cparams("parallel", "arbitrary"),
        name="gla",
    )(proj, proj, proj, proj, proj, gw, gate_b.reshape(1, GLA_KEY_DIM), norm_w.reshape(1, GLA_DV))


def _merge_kernel(oa_ref, ob_ref, ga_ref, gb_ref, x_ref, wa_ref, wb_ref, wo_ref, bma_ref, bmb_ref,
                  g1_ref, sc2_ref, sh2_ref, nw_ref, wrh_ref, wrl_ref, br_ref,
                  x1_ref, h2_ref, e4_ref, w4_ref, p4_ref, cnt_ref, run_ref):
    tm = MERGE_TM

    @pl.when(pl.program_id(0) == 0)
    def _():
        run_ref[...] = jnp.zeros_like(run_ref)

    y_a = _dot(oa_ref[...], wa_ref[...])
    y_b = _dot(ob_ref[...], wb_ref[...])
    g_a = jax.nn.sigmoid(ga_ref[...] + bma_ref[...])
    g_b = jax.nn.sigmoid(gb_ref[...] + bmb_ref[...])
    mix = _dot((g_a * y_a + g_b * y_b).astype(BF16), wo_ref[...])
    x1 = x_ref[...] + g1_ref[...] * mix
    x1_ref[...] = x1

    ms = jnp.mean(x1 * x1, axis=-1, keepdims=True)
    h2 = x1 * lax.rsqrt(ms + RMS_EPS) * nw_ref[...] * (1.0 + sc2_ref[...]) + sh2_ref[...]
    for s in range(D_MODEL // LANES):
        h2_ref[pl.ds(s, tm, stride=SUBLANES), :] = h2[:, s * LANES:(s + 1) * LANES]

    h_top = _truncate_to_bf16_bits(h2, pltpu.bitcast)
    h_hi = h_top.astype(BF16)
    h_lo = (h2 - h_top).astype(BF16)
    logits = (_dot(h_hi, wrh_ref[...]) + (_dot(h_lo, wrh_ref[...]) + _dot(h_hi, wrl_ref[...]))) + br_ref[...]
    lane = lax.broadcasted_iota(jnp.int32, (tm, LANES), 1)
    lane_f = lane.astype(F32)
    neg_inf = jnp.float32(-jnp.inf)
    g = jnp.where(lane < N_EXPERTS, logits, neg_inf)
    picks, tops = [], []
    for _ in range(TOP_K):
        mx = jnp.max(g, axis=1, keepdims=True)
        idx = _first_max_lane(g, lane_f)
        pick = lane_f == idx
        picks.append((pick, idx))
        tops.append(mx)
        g = jnp.where(pick, neg_inf, g)
    ex = [jnp.exp(t - tops[0]) for t in tops]
    denom = ex[0] + ex[1] + ex[2] + ex[3]

    sel = sum(jnp.where(pick, 1.0, 0.0) for pick, _ in picks)
    r_i = lax.broadcasted_iota(jnp.int32, (tm, tm), 0)
    c_i = lax.broadcasted_iota(jnp.int32, (tm, tm), 1)
    lower = jnp.where(c_i < r_i, 1.0, 0.0).astype(BF16)
    rank = run_ref[0:1, :] + _dot(lower, sel.astype(BF16))
    run_new = run_ref[0:1, :] + jnp.sum(sel, axis=0, keepdims=True)
    run_ref[...] = jnp.broadcast_to(run_new, run_ref.shape)
    cnt_ref[...] = jnp.broadcast_to(run_new, cnt_ref.shape)

    e4 = jnp.zeros((tm, LANES), jnp.int32)
    w4 = jnp.zeros((tm, LANES), F32)
    p4 = jnp.zeros((tm, LANES), F32)
    for r in range(TOP_K):
        pick, idx = picks[r]
        pos = jnp.sum(jnp.where(pick, rank, 0.0), axis=1, keepdims=True)
        e4 = jnp.where(lane == r, idx.astype(jnp.int32), e4)
        w4 = jnp.where(lane == r, ex[r] / denom, w4)
        p4 = jnp.where(lane == r, pos, p4)
    e4_ref[...] = e4
    w4_ref[...] = w4
    p4_ref[...] = p4


def _merge_and_route(o_a, o_b, proj, x2, w_a, w_b, w_o, b_merge, mod, norm2_w, w_router, b_router, S):
    T = x2.shape[0]
    tm = MERGE_TM
    tiles_per_batch = S // tm
    full = lambda shape: pl.BlockSpec(shape, lambda i: tuple(0 for _ in shape))
    row = lambda width, col: pl.BlockSpec((tm, width), lambda i: (i, col))
    modv = lambda k: pl.BlockSpec((None, 1, D_MODEL), lambda i: (i // tiles_per_batch, 0, k))
    wr = jnp.zeros((D_MODEL, LANES), F32).at[:, :N_EXPERTS].set(w_router)
    wr_top = _truncate_to_bf16_bits(wr, lax.bitcast_convert_type)
    wr_hi = wr_top.astype(BF16)
    wr_lo = (wr - wr_top).astype(BF16)
    br = jnp.zeros((1, LANES), F32).at[0, :N_EXPERTS].set(b_router)
    bm = b_merge.reshape(1, 2 * D_MODEL)
    return pl.pallas_call(
        _merge_kernel,
        out_shape=(
            jax.ShapeDtypeStruct((T, D_MODEL), F32),
            jax.ShapeDtypeStruct((T * SUBLANES, LANES), F32),
            jax.ShapeDtypeStruct((T, LANES), jnp.int32),
            jax.ShapeDtypeStruct((T, LANES), F32),
            jax.ShapeDtypeStruct((T, LANES), F32),
            jax.ShapeDtypeStruct((SUBLANES, LANES), F32),
        ),
        grid=(T // tm,),
        in_specs=[
            row(D_MODEL, 0), row(D_MODEL, 0),
            row(D_MODEL, COL_GATES // D_MODEL), row(D_MODEL, COL_GATES // D_MODEL + 1),
            row(D_MODEL, 0),
            full((D_MODEL, D_MODEL)), full((D_MODEL, D_MODEL)), full((D_MODEL, D_MODEL)),
            pl.BlockSpec((1, D_MODEL), lambda i: (0, 0)), pl.BlockSpec((1, D_MODEL), lambda i: (0, 1)),
            modv(2), modv(4), modv(3),
            full((1, D_MODEL)), full((D_MODEL, LANES)), full((D_MODEL, LANES)), full((1, LANES)),
        ],
        out_specs=(
            row(D_MODEL, 0),
            pl.BlockSpec((tm * SUBLANES, LANES), lambda i: (i, 0)),
            row(LANES, 0), row(LANES, 0), row(LANES, 0),
            full((SUBLANES, LANES)),
        ),
        scratch_shapes=[pltpu.VMEM((SUBLANES, LANES), F32)],
        compiler_params=_cparams("arbitrary"),
        name="merge_route",
    )(o_a, o_b, proj, proj, x2, w_a, w_b, w_o, bm, bm, mod, mod, mod,
      norm2_w.reshape(1, D_MODEL), wr_hi, wr_lo, br)


def _row_slab(ref, row):
    return ref.at[pl.ds(pl.multiple_of(row * SUBLANES, SUBLANES), SUBLANES), :]


def _wait_slabs(src_ref, dst_ref, sem, count):
    def wait(a, c):
        pltpu.make_async_copy(_row_slab(src_ref, 0), _row_slab(dst_ref, 0), sem).wait()
        return c

    lax.fori_loop(0, count, wait, 0, unroll=DMA_WAIT_UNROLL)


def _dispatch_kernel(dest_ref, h_ref, xs_in_ref, xs_ref, sem):
    del xs_in_ref
    tm = COMBINE_TM
    base = pl.program_id(0) * (tm * TOP_K)

    def start(t, c):
        src = _row_slab(h_ref, t)
        for kk in range(TOP_K):
            pltpu.make_async_copy(src, _row_slab(xs_ref, dest_ref[base + t * TOP_K + kk]), sem).start(
                priority=kk % DMA_PRIORITIES)
        return c

    lax.fori_loop(0, tm, start, 0, unroll=DMA_ISSUE_UNROLL)
    _wait_slabs(h_ref, xs_ref, sem, tm * TOP_K)


def _dispatch(dest, h2_slabs, P):
    T = h2_slabs.shape[0] // SUBLANES
    tm = COMBINE_TM
    xs0 = jnp.zeros((P * SUBLANES, LANES), F32)
    return pl.pallas_call(
        _dispatch_kernel,
        out_shape=jax.ShapeDtypeStruct((P * SUBLANES, LANES), F32),
        grid_spec=pltpu.PrefetchScalarGridSpec(
            num_scalar_prefetch=1,
            grid=(T // tm,),
            in_specs=[
                pl.BlockSpec((tm * SUBLANES, LANES), lambda i, dest: (i, 0)),
                pl.BlockSpec(memory_space=pl.ANY),
            ],
            out_specs=pl.BlockSpec(memory_space=pl.ANY),
            scratch_shapes=[pltpu.SemaphoreType.DMA],
        ),
        input_output_aliases={2: 0},
        compiler_params=_cparams("arbitrary"),
        name="moe_dispatch",
    )(dest, h2_slabs, xs0)


def _expert_kernel(be_ref, nu_ref, xs_ref, wi_ref, bi_ref, wo_ref, bo_ref, perm_ref, ys_ref,
                   x_ref, wi16_ref, wo16_ref):
    R = EXPERT_ROWS
    j = pl.program_id(0)
    used = j < nu_ref[0]
    new_expert = (j == 0) | (be_ref[j] != be_ref[jnp.maximum(j - 1, 0)])

    @pl.when(used & new_expert)
    def _():
        perm = perm_ref[...]
        for g in range(D_MODEL // LANES):
            w = wi_ref[:, 2 * LANES * g:2 * LANES * (g + 1)].astype(BF16)
            sep = _dot(w, perm).astype(BF16)
            wi16_ref[:, LANES * g:LANES * (g + 1)] = sep[:, :LANES]
            wi16_ref[:, D_MODEL + LANES * g:D_MODEL + LANES * (g + 1)] = sep[:, LANES:]
        wo16_ref[...] = wo_ref[...].astype(BF16)

    @pl.when(used)
    def _():
        for s in range(D_MODEL // LANES):
            x_ref[:, s * LANES:(s + 1) * LANES] = xs_ref[pl.ds(s, R, stride=SUBLANES), :].astype(BF16)
        hid = _dot(x_ref[...], wi16_ref[...]) + bi_ref[...]
        glu = jnp.minimum(hid[:, :D_MODEL], SWIGLU_LIMIT)
        lin = jnp.clip(hid[:, D_MODEL:], -SWIGLU_LIMIT, SWIGLU_LIMIT)
        act = glu * jax.nn.sigmoid(SWIGLU_ALPHA * glu) * (lin + 1.0)
        out = _dot(act.astype(BF16), wo16_ref[...]) + bo_ref[...]
        for s in range(D_MODEL // LANES):
            ys_ref[pl.ds(s, R, stride=SUBLANES), :] = out[:, s * LANES:(s + 1) * LANES]

    @pl.when(j >= nu_ref[0])
    def _():
        ys_ref[...] = jnp.zeros_like(ys_ref)


def _experts(block_expert, n_used, xs, w_in, b_in, w_out, b_out):
    R = EXPERT_ROWS
    n_blocks = xs.shape[0] // (R * SUBLANES)
    src = jnp.arange(2 * LANES)[:, None]
    dst = jnp.arange(2 * LANES)[None, :]
    perm = jnp.where(src == jnp.where(dst < LANES, 2 * dst, 2 * (dst - LANES) + 1), 1.0, 0.0).astype(BF16)
    return pl.pallas_call(
        _expert_kernel,
        out_shape=jax.ShapeDtypeStruct(xs.shape, F32),
        grid_spec=pltpu.PrefetchScalarGridSpec(
            num_scalar_prefetch=2,
            grid=(n_blocks,),
            in_specs=[
                pl.BlockSpec((R * SUBLANES, LANES), lambda j, be, nu: (jnp.minimum(j, jnp.maximum(nu[0] - 1, 0)), 0)),
                pl.BlockSpec((None, D_MODEL, 2 * D_MODEL), lambda j, be, nu: (be[j], 0, 0)),
                pl.BlockSpec((None, 1, 2 * D_MODEL), lambda j, be, nu: (be[j], 0, 0)),
                pl.BlockSpec((None, D_MODEL, D_MODEL), lambda j, be, nu: (be[j], 0, 0)),
                pl.BlockSpec((None, 1, D_MODEL), lambda j, be, nu: (be[j], 0, 0)),
                pl.BlockSpec((2 * LANES, 2 * LANES), lambda j, be, nu: (0, 0)),
            ],
            out_specs=pl.BlockSpec((R * SUBLANES, LANES), lambda j, be, nu: (j, 0)),
            scratch_shapes=[
                pltpu.VMEM((R, D_MODEL), BF16),
                pltpu.VMEM((D_MODEL, 2 * D_MODEL), BF16),
                pltpu.VMEM((D_MODEL, D_MODEL), BF16),
            ],
        ),
        compiler_params=pltpu.CompilerParams(
            dimension_semantics=("arbitrary",), vmem_limit_bytes=EXPERT_VMEM_LIMIT),
        name="moe_experts",
    )(block_expert, n_used, xs, w_in, b_in, w_out, b_out, perm)


def _combine_kernel(dest_ref, ys_ref, w4_ref, x1_ref, g2_ref, nw_ref, o_ref, buf_ref, sem):
    tm = COMBINE_TM
    base = pl.program_id(0) * (tm * TOP_K)

    def start(t, c):
        for kk in range(TOP_K):
            pltpu.make_async_copy(_row_slab(ys_ref, dest_ref[base + t * TOP_K + kk]),
                                  _row_slab(buf_ref.at[kk], t), sem).start(priority=kk % DMA_PRIORITIES)
        return c

    lax.fori_loop(0, tm, start, 0, unroll=DMA_ISSUE_UNROLL)
    _wait_slabs(ys_ref, buf_ref.at[0], sem, tm * TOP_K)

    w4 = w4_ref[...]
    g2 = g2_ref[...]
    parts = []
    ssq = jnp.zeros((tm, 1), F32)
    for s in range(D_MODEL // LANES):
        cols = slice(s * LANES, (s + 1) * LANES)
        y = jnp.zeros((tm, LANES), F32)
        for kk in range(TOP_K):
            y = y + buf_ref[kk, pl.ds(s, tm, stride=SUBLANES), :] * w4[:, kk:kk + 1]
        x2 = x1_ref[:, cols] + g2[:, cols] * y
        ssq = ssq + jnp.sum(x2 * x2, axis=1, keepdims=True)
        parts.append(x2)
    inv = lax.rsqrt(ssq / D_MODEL + RMS_EPS)
    nw = nw_ref[...]
    for s in range(D_MODEL // LANES):
        cols = slice(s * LANES, (s + 1) * LANES)
        o_ref[:, cols] = parts[s] * inv * nw[:, cols]


def _combine(dest, ys, w4, x1, mod, final_w, S):
    T = x1.shape[0]
    tm = COMBINE_TM
    tiles_per_batch = S // tm
    return pl.pallas_call(
        _combine_kernel,
        out_shape=jax.ShapeDtypeStruct((T, D_MODEL), F32),
        grid_spec=pltpu.PrefetchScalarGridSpec(
            num_scalar_prefetch=1,
            grid=(T // tm,),
            in_specs=[
                pl.BlockSpec(memory_space=pl.ANY),
                pl.BlockSpec((tm, LANES), lambda i, dest: (i, 0)),
                pl.BlockSpec((tm, D_MODEL), lambda i, dest: (i, 0)),
                pl.BlockSpec((None, 1, D_MODEL), lambda i, dest: (i // tiles_per_batch, 0, 5)),
                pl.BlockSpec((1, D_MODEL), lambda i, dest: (0, 0)),
            ],
            out_specs=pl.BlockSpec((tm, D_MODEL), lambda i, dest: (i, 0)),
            scratch_shapes=[
                pltpu.VMEM((TOP_K, tm * SUBLANES, LANES), F32),
                pltpu.SemaphoreType.DMA,
            ],
        ),
        compiler_params=_cparams("arbitrary"),
        name="moe_combine",
    )(dest, ys, w4, x1, mod, final_w.reshape(1, D_MODEL))


def _layer(x2, c_mod, B, S, norm1_w, w_in, gla_gate_w, gla_gate_b, gla_norm_w, w_o_moba, w_o_gla,
           b_merge, w_out, norm2_w, w_router, b_router, w_exp_in, b_exp_in, w_exp_out, b_exp_out):
    T = B * S
    glow_lo = N_MAIN
    glow_hi = N_MAIN + GLA_GATE_RANK
    w_proj = jnp.concatenate(
        [w_in[:, :glow_lo], w_in[:, glow_hi:], w_in[:, glow_lo:glow_hi],
         jnp.zeros((D_MODEL, LANES - GLA_GATE_RANK), F32)], axis=1).astype(BF16)
    proj = _input_projection(x2, norm1_w, c_mod, w_proj, S)

    tables = _rope_tables(S)
    kaug, vb, kmean = _moba_prep(proj, tables, B, S)
    NB = S // MOBA_BLOCK
    kmean = kmean[:, :NB].reshape(B, NB, MOBA_HEADS, MOBA_HEAD_DIM).transpose(0, 2, 1, 3)
    kmean = jnp.pad(kmean, ((0, 0), (0, 0), (0, -NB % BF16_SUBLANES), (0, 0)))
    qaug = _moba_qprep(proj, tables, kmean, B, S)
    o_a = _moba_attention(qaug, kaug, vb, B, S)
    o_b = _gla(proj, gla_gate_w, gla_gate_b, gla_norm_w, B, S)

    x1, h2, e4, w4, p4, cnt = _merge_and_route(
        o_a, o_b, proj, x2, w_o_moba.astype(BF16), w_o_gla.astype(BF16), w_out.astype(BF16),
        b_merge, c_mod, norm2_w, w_router, b_router, S)

    R = EXPERT_ROWS
    counts = cnt[0, :N_EXPERTS].astype(jnp.int32)
    padded = (counts + R - 1) // R * R
    padded_end = jnp.cumsum(padded)
    padded_start = padded_end - padded
    n_blocks = -(-(T * TOP_K + N_EXPERTS * (R - 1)) // R)
    P = n_blocks * R
    dest = (padded_start[e4[:, :TOP_K]] + p4[:, :TOP_K].astype(jnp.int32)).reshape(-1)
    block_row0 = jnp.arange(n_blocks, dtype=jnp.int32) * R
    block_expert = jnp.minimum(
        jnp.sum(padded_end[None, :] <= block_row0[:, None], axis=1), N_EXPERTS - 1).astype(jnp.int32)
    n_used = (padded_end[-1:] // R).astype(jnp.int32)

    xs = _dispatch(dest, h2, P)
    b_ei = jnp.concatenate([b_exp_in[:, 0::2], b_exp_in[:, 1::2]], axis=-1)[:, None, :]
    ys = _experts(block_expert, n_used, xs, w_exp_in, b_ei, w_exp_out, b_exp_out[:, None, :])
    return x1, ys, dest, w4


def kernel(x, c, w_ada, b_ada, norm1_w, w_in, gla_gate_w, gla_gate_b, gla_norm_w, w_o_moba, w_o_gla, b_merge, w_out, norm2_w, w_router, b_router, w_exp_in, b_exp_in, w_exp_out, b_exp_out, final_norm_w):
    B, S, _ = x.shape
    depth = w_ada.shape[0]
    assert depth == 1, "the combine kernel fuses the final norm, so a single layer is supported"
    x2 = x.reshape(B * S, D_MODEL)
    l = 0
    mod = _modulation(c, w_ada[l], b_ada[l])
    x1, ys, dest, w4 = _layer(
        x2, mod, B, S, norm1_w[l], w_in[l], gla_gate_w[l], gla_gate_b[l], gla_norm_w[l],
        w_o_moba[l], w_o_gla[l], b_merge[l], w_out[l], norm2_w[l], w_router[l], b_router[l],
        w_exp_in[l], b_exp_in[l], w_exp_out[l], b_exp_out[l])
    out = _combine(dest, ys, w4, x1, mod, final_norm_w, S)
    return out.reshape(B, S, D_MODEL)
```

```python
import functools

import jax
import jax.numpy as jnp
from jax import lax
from jax.experimental import pallas as pl
from jax.experimental.pallas import tpu as pltpu

F32 = jnp.float32
BF16 = jnp.bfloat16
HIGHEST = lax.Precision.HIGHEST

D_MODEL = 1024
MOBA_HEADS = 8
MOBA_HEAD_DIM = 128
MOBA_BLOCK = 256
MOBA_TOPK = 3
ROPE_DIM = MOBA_HEAD_DIM // 4
ROPE_THETA = 500000.0
GLA_HEADS = 4
GLA_KEY_DIM = D_MODEL // 2
GLA_VAL_DIM = D_MODEL
GLA_DK = GLA_KEY_DIM // GLA_HEADS
GLA_DV = GLA_VAL_DIM // GLA_HEADS
GLA_GATE_RANK = 16
GLA_GATE_NORMALIZER = 16.0
N_EXPERTS = 32
TOP_K = 4
SWIGLU_ALPHA = 1.702
SWIGLU_LIMIT = 7.0
RMS_EPS = 1e-5
LOG2E = 1.4426950408889634

LANES = 128
SUBLANES = 8
BF16_SUBLANES = 16
VMEM_LIMIT = 48 * 1024 * 1024
EXPERT_VMEM_LIMIT = 56 * 1024 * 1024

INPROJ_TM = 1024
INPROJ_TN = 1664
MOBA_GROUP = 4
MOBA_SUM_ROWS = BF16_SUBLANES
GLA_CHUNK = 64
GLA_SUB = 16
GLA_DECAY_TM = 512
GLA_SAFE_EXPONENT = 80.0
MERGE_TM = 256
EXPERT_ROWS = 256
COMBINE_TM = 256
MASK_NEG = -1e30
DMA_ISSUE_UNROLL = 4
DMA_WAIT_UNROLL = 16
DMA_PRIORITIES = 2

N_MAIN = 3 * D_MODEL + 2 * GLA_KEY_DIM + 2 * GLA_VAL_DIM
COL_GATES = N_MAIN
COL_GLOW = N_MAIN + 2 * D_MODEL
N_PROJ = COL_GLOW + LANES


def _dot(a, b, **kw):
    return jnp.dot(a, b, preferred_element_type=F32, **kw)


def _dot_nt(a, b, **kw):
    return lax.dot_general(a, b, (((1,), (1,)), ((), ())), preferred_element_type=F32, **kw)


def _truncate_to_bf16_bits(x, bitcast):
    return bitcast(bitcast(x, jnp.uint32) & jnp.uint32(0xFFFF0000), F32)


def _first_max_lane(g, lane_f):
    mx = jnp.max(g, axis=1, keepdims=True)
    return jnp.min(jnp.where(g == mx, lane_f, float(LANES)), axis=1, keepdims=True)


def _cparams(*sem):
    return pltpu.CompilerParams(dimension_semantics=sem, vmem_limit_bytes=VMEM_LIMIT)


def _mod_kernel(c_ref, w_ref, b_ref, o_ref):
    c = c_ref[...]
    ca = c * jax.nn.sigmoid(c)
    o_ref[...] = _dot(ca, w_ref[...], precision=HIGHEST) + b_ref[...]


def _modulation(c, w_ada, b_ada):
    B = c.shape[0]
    n = w_ada.shape[1]
    c8 = jnp.zeros((SUBLANES, D_MODEL), F32).at[:B].set(c)
    out = pl.pallas_call(
        _mod_kernel,
        out_shape=jax.ShapeDtypeStruct((SUBLANES, n), F32),
        grid=(n // D_MODEL,),
        in_specs=[
            pl.BlockSpec((SUBLANES, D_MODEL), lambda j: (0, 0)),
            pl.BlockSpec((D_MODEL, D_MODEL), lambda j: (0, j)),
            pl.BlockSpec((1, D_MODEL), lambda j: (0, j)),
        ],
        out_specs=pl.BlockSpec((SUBLANES, D_MODEL), lambda j: (0, j)),
        compiler_params=_cparams("arbitrary"),
        name="adaln_mod",
    )(c8, w_ada, b_ada.reshape(1, n))
    return out[:B].reshape(B, 1, n)


def _inproj_kernel(x_ref, nw_ref, sc_ref, sh_ref, w_ref, o_ref, h_ref):
    @pl.when(pl.program_id(1) == 0)
    def _():
        x = x_ref[...]
        ms = jnp.mean(x * x, axis=-1, keepdims=True)
        y = x * lax.rsqrt(ms + RMS_EPS) * nw_ref[...]
        h_ref[...] = (y * (1.0 + sc_ref[...]) + sh_ref[...]).astype(BF16)

    o_ref[...] = _dot(h_ref[...], w_ref[...])


def _input_projection(x2, norm_w, mod, w_proj, S):
    T = x2.shape[0]
    tiles_per_batch = S // INPROJ_TM
    return pl.pallas_call(
        _inproj_kernel,
        out_shape=jax.ShapeDtypeStruct((T, N_PROJ), F32),
        grid=(T // INPROJ_TM, N_PROJ // INPROJ_TN),
        in_specs=[
            pl.BlockSpec((INPROJ_TM, D_MODEL), lambda i, j: (i, 0)),
            pl.BlockSpec((1, D_MODEL), lambda i, j: (0, 0)),
            pl.BlockSpec((None, 1, D_MODEL), lambda i, j: (i // tiles_per_batch, 0, 1)),
            pl.BlockSpec((None, 1, D_MODEL), lambda i, j: (i // tiles_per_batch, 0, 0)),
            pl.BlockSpec((D_MODEL, INPROJ_TN), lambda i, j: (0, j)),
        ],
        out_specs=pl.BlockSpec((INPROJ_TM, INPROJ_TN), lambda i, j: (i, j)),
        scratch_shapes=[pltpu.VMEM((INPROJ_TM, D_MODEL), BF16)],
        compiler_params=_cparams("parallel", "arbitrary"),
        name="inproj",
    )(x2, norm_w.reshape(1, D_MODEL), mod, mod, w_proj)


def _rope_tables(S):
    half = ROPE_DIM // 2
    inv_freq = jnp.float32(ROPE_THETA) ** (-jnp.arange(half, dtype=jnp.float32) * 2.0 / ROPE_DIM)
    ang = jnp.arange(S, dtype=jnp.float32)[:, None] * inv_freq[None, :]
    cos, sin = jnp.cos(ang), jnp.sin(ang)
    ones = jnp.ones((S, MOBA_HEAD_DIM - ROPE_DIM), F32)
    zeros_hi = jnp.zeros((S, MOBA_HEAD_DIM - half), F32)
    zeros_lo = jnp.zeros((S, half), F32)
    cos_f = jnp.concatenate([cos, cos, ones], axis=1)
    sin_a = jnp.concatenate([-sin, zeros_hi], axis=1)
    sin_b = jnp.concatenate([zeros_lo, sin, ones * 0.0], axis=1)
    return cos_f, sin_a, sin_b


def _rope(x, cos_f, sin_a, sin_b):
    half = ROPE_DIM // 2
    up = pltpu.roll(x, MOBA_HEAD_DIM - half, 1)
    dn = pltpu.roll(x, half, 1)
    return x * cos_f + up * sin_a + dn * sin_b


def _kprep_kernel(k_ref, v_ref, cos_ref, sa_ref, sb_ref, kaug_ref, vb_ref, kmean_ref):
    n = pl.program_id(1)
    cos_f, sin_a, sin_b = cos_ref[...], sa_ref[...], sb_ref[...]
    lane = lax.broadcasted_iota(jnp.int32, (MOBA_BLOCK, LANES), 1)
    onehot = jnp.where(lane == n, 1.0, 0.0).astype(BF16)
    for h in range(MOBA_HEADS):
        cols = slice(h * MOBA_HEAD_DIM, (h + 1) * MOBA_HEAD_DIM)
        kr = _rope(k_ref[:, cols], cos_f, sin_a, sin_b)
        kmean_ref[:, cols] = jnp.mean(kr, axis=0, keepdims=True)
        kaug_ref[h, :, :MOBA_HEAD_DIM] = kr.astype(BF16)
        kaug_ref[h, :, MOBA_HEAD_DIM:] = onehot
        vb_ref[h, :MOBA_HEAD_DIM, :] = v_ref[:, cols].T.astype(BF16)
        vb_ref[h, MOBA_HEAD_DIM:, :] = jnp.ones((MOBA_SUM_ROWS, MOBA_BLOCK), BF16)


def _moba_prep(proj, tables, B, S):
    NB = S // MOBA_BLOCK
    NBP = NB
    src = lambda n: n
    tab_spec = pl.BlockSpec((MOBA_BLOCK, LANES), lambda b, n: (src(n), 0))
    return pl.pallas_call(
        _kprep_kernel,
        out_shape=(
            jax.ShapeDtypeStruct((B, MOBA_HEADS, NBP * MOBA_BLOCK, 2 * MOBA_HEAD_DIM), BF16),
            jax.ShapeDtypeStruct((B, MOBA_HEADS, MOBA_HEAD_DIM + MOBA_SUM_ROWS, NBP * MOBA_BLOCK), BF16),
            jax.ShapeDtypeStruct((B, NBP, 1, D_MODEL), F32),
        ),
        grid=(B, NBP),
        in_specs=[
            pl.BlockSpec((MOBA_BLOCK, D_MODEL), lambda b, n: (b * NB + src(n), 1)),
            pl.BlockSpec((MOBA_BLOCK, D_MODEL), lambda b, n: (b * NB + src(n), 2)),
            tab_spec, tab_spec, tab_spec,
        ],
        out_specs=(
            pl.BlockSpec((None, MOBA_HEADS, MOBA_BLOCK, 2 * MOBA_HEAD_DIM), lambda b, n: (b, 0, n, 0)),
            pl.BlockSpec((None, MOBA_HEADS, MOBA_HEAD_DIM + MOBA_SUM_ROWS, MOBA_BLOCK), lambda b, n: (b, 0, 0, n)),
            pl.BlockSpec((None, None, 1, D_MODEL), lambda b, n: (b, n, 0, 0)),
        ),
        compiler_params=_cparams("parallel", "parallel"),
        name="moba_kprep",
    )(proj, proj, *tables)


def _qprep_kernel(q_ref, cos_ref, sa_ref, sb_ref, kmean_ref, qaug_ref):
    i = pl.program_id(1)
    blk = MOBA_BLOCK
    scale = MOBA_HEAD_DIM ** -0.5
    cos_f, sin_a, sin_b = cos_ref[...], sa_ref[...], sb_ref[...]
    nbp = kmean_ref.shape[1]
    blk_id = lax.broadcasted_iota(jnp.int32, (nbp, blk), 0)
    blk_f = blk_id.astype(F32)
    neg_inf = jnp.float32(-jnp.inf)
    for h in range(MOBA_HEADS):
        cols = slice(h * MOBA_HEAD_DIM, (h + 1) * MOBA_HEAD_DIM)
        qr = _rope(q_ref[:, cols], cos_f, sin_a, sin_b)
        gate = _dot_nt(kmean_ref[h], qr, precision=HIGHEST)
        g = jnp.where(blk_id < i, gate, neg_inf)
        bias = jnp.where(blk_id == i, 0.0, MASK_NEG)
        for r in range(MOBA_TOPK):
            mx = jnp.max(g, axis=0, keepdims=True)
            first = jnp.min(jnp.where(g == mx, blk_f, float(nbp)), axis=0, keepdims=True)
            pick = blk_f == first
            unmask = jnp.where(i > r, 0.0, MASK_NEG)
            bias = jnp.where(pick, jnp.maximum(bias, unmask), bias)
            g = jnp.where(pick, neg_inf, g)
        qaug_ref[h, :MOBA_HEAD_DIM, :] = (qr * (scale * LOG2E)).T.astype(BF16)
        qaug_ref[h, MOBA_HEAD_DIM:MOBA_HEAD_DIM + nbp, :] = bias.astype(BF16)
        qaug_ref[h, MOBA_HEAD_DIM + nbp:, :] = jnp.zeros((MOBA_HEAD_DIM - nbp, blk), BF16)


def _moba_qprep(proj, tables, kmean, B, S):
    NB = S // MOBA_BLOCK
    tab_spec = pl.BlockSpec((MOBA_BLOCK, LANES), lambda b, i: (i, 0))
    return pl.pallas_call(
        _qprep_kernel,
        out_shape=jax.ShapeDtypeStruct((B, MOBA_HEADS, 2 * MOBA_HEAD_DIM, S), BF16),
        grid=(B, NB),
        in_specs=[
            pl.BlockSpec((MOBA_BLOCK, D_MODEL), lambda b, i: (b * NB + i, 0)),
            tab_spec, tab_spec, tab_spec,
            pl.BlockSpec((None, MOBA_HEADS, kmean.shape[2], MOBA_HEAD_DIM), lambda b, i: (b, 0, 0, 0)),
        ],
        out_specs=pl.BlockSpec((None, MOBA_HEADS, 2 * MOBA_HEAD_DIM, MOBA_BLOCK), lambda b, i: (b, 0, 0, i)),
        compiler_params=_cparams("parallel", "parallel"),
        name="moba_qprep",
    )(proj, *tables, kmean)


def _moba_kernel(q_ref, kaug_ref, v_ref, o_ref, m_ref, acc_ref, s_ref, mx_ref):
    n_full = pl.program_id(2)
    width = MOBA_GROUP * MOBA_BLOCK
    qt = q_ref[...]

    def scores(g):
        return _dot(kaug_ref[pl.ds(pl.multiple_of(g * width, width), width), :], qt)

    def prob_value(s, m, g):
        p = jnp.exp2(s - m).astype(BF16)
        return _dot(v_ref[:, pl.ds(pl.multiple_of(g * width, width), width)], p)

    def stage(g):
        s = scores(g)
        s_ref[...] = s
        mx_ref[...] = jnp.max(s, axis=0, keepdims=True)

    s_own = scores(n_full)
    stage(0)
    key = lax.broadcasted_iota(jnp.int32, (width, width), 0)
    qry = lax.broadcasted_iota(jnp.int32, (width, width), 1)
    s_own = jnp.where(key > qry, jnp.float32(-jnp.inf), s_own)
    m0 = jnp.max(s_own, axis=0, keepdims=True)
    m_ref[...] = m0
    acc_ref[...] = prob_value(s_own, m0, n_full)

    def body(g, carry):
        m_prev = m_ref[...]
        m_new = jnp.maximum(m_prev, mx_ref[...])
        alpha = jnp.exp2(m_prev - m_new)
        m_ref[...] = m_new
        pv = prob_value(s_ref[...], m_new, g)
        stage(jnp.minimum(g + 1, jnp.maximum(n_full - 1, 0)))
        acc_ref[...] = alpha * acc_ref[...] + pv
        return carry

    lax.fori_loop(0, n_full, body, 0)
    acc = acc_ref[...]
    out_t = acc[:MOBA_HEAD_DIM, :] / acc[MOBA_HEAD_DIM:MOBA_HEAD_DIM + 1, :]
    o_ref[...] = out_t.T.astype(o_ref.dtype)


def _moba_attention(qaug, kaug, vb, B, S):
    T = B * S
    tq = MOBA_GROUP * MOBA_BLOCK
    assert S % tq == 0
    NG = S // tq
    return pl.pallas_call(
        _moba_kernel,
        out_shape=jax.ShapeDtypeStruct((T, D_MODEL), BF16),
        grid=(B, MOBA_HEADS, NG),
        in_specs=[
            pl.BlockSpec((None, None, 2 * MOBA_HEAD_DIM, tq), lambda b, h, j: (b, h, 0, j)),
            pl.BlockSpec((None, None, S, 2 * MOBA_HEAD_DIM), lambda b, h, j: (b, h, 0, 0)),
            pl.BlockSpec((None, None, MOBA_HEAD_DIM + MOBA_SUM_ROWS, S), lambda b, h, j: (b, h, 0, 0)),
        ],
        out_specs=pl.BlockSpec((tq, MOBA_HEAD_DIM), lambda b, h, j: (b * NG + j, h)),
        scratch_shapes=[
            pltpu.VMEM((1, tq), F32),
            pltpu.VMEM((MOBA_HEAD_DIM + MOBA_SUM_ROWS, tq), F32),
            pltpu.VMEM((tq, tq), F32),
            pltpu.VMEM((1, tq), F32),
        ],
        compiler_params=_cparams("parallel", "parallel", "arbitrary"),
        name="moba_attn",
    )(qaug, kaug, vb)


def _gla_decay_kernel(gl_ref, gw_ref, gb_ref, b_ref, peak_ref):
    tm, C = GLA_DECAY_TM, GLA_CHUNK
    z = _dot(gl_ref[...].astype(BF16), gw_ref[...].astype(BF16)) + gb_ref[...]
    log_g = jax.nn.log_sigmoid(z) / GLA_GATE_NORMALIZER
    r_i = lax.broadcasted_iota(jnp.int32, (tm, tm), 0)
    c_i = lax.broadcasted_iota(jnp.int32, (tm, tm), 1)
    tri = jnp.where((c_i <= r_i) & (c_i >= (r_i // C) * C), 1.0, 0.0).astype(BF16)
    hi = _truncate_to_bf16_bits(log_g, pltpu.bitcast)
    rest = log_g - hi
    mid = _truncate_to_bf16_bits(rest, pltpu.bitcast)
    lo = rest - mid
    b_ref[...] = (_dot(tri, hi.astype(BF16)) + _dot(tri, mid.astype(BF16))) + _dot(tri, lo.astype(BF16))
    steps = -log_g
    for c in range(tm // C):
        pk = jnp.max(jnp.max(steps[c * C:(c + 1) * C, :], axis=1, keepdims=True), axis=0, keepdims=True)
        peak_ref[c:c + 1, :] = jnp.broadcast_to(pk, (1, LANES))


def _gla_decay(proj, gate_w, gate_b):
    T = proj.shape[0]
    tm = GLA_DECAY_TM
    gw = jnp.zeros((LANES, GLA_KEY_DIM), F32).at[:GLA_GATE_RANK].set(gate_w)
    return pl.pallas_call(
        _gla_decay_kernel,
        out_shape=(
            jax.ShapeDtypeStruct((T, GLA_KEY_DIM), F32),
            jax.ShapeDtypeStruct((T // GLA_CHUNK, LANES), F32),
        ),
        grid=(T // tm,),
        in_specs=[
            pl.BlockSpec((tm, LANES), lambda i: (i, COL_GLOW // LANES)),
            pl.BlockSpec((LANES, GLA_KEY_DIM), lambda i: (0, 0)),
            pl.BlockSpec((1, GLA_KEY_DIM), lambda i: (0, 0)),
        ],
        out_specs=(
            pl.BlockSpec((tm, GLA_KEY_DIM), lambda i: (i, 0)),
            pl.BlockSpec((tm // GLA_CHUNK, LANES), lambda i: (i, 0)),
        ),
        compiler_params=_cparams("parallel"),
        name="gla_decay",
    )(proj, gw, gate_b.reshape(1, GLA_KEY_DIM))


def _gla_kernel(flag_ref, q_ref, k_ref, v_ref, gr_ref, b_ref, nw_ref, o_ref, state_ref):
    C, SUB = GLA_CHUNK, GLA_SUB
    nsub = C // SUB

    @pl.when(pl.program_id(1) == 0)
    def _():
        state_ref[...] = jnp.zeros_like(state_ref)

    r_i = lax.broadcasted_iota(jnp.int32, (C, C), 0)
    c_i = lax.broadcasted_iota(jnp.int32, (C, C), 1)
    b_all = b_ref[...]

    row_c = lax.broadcasted_iota(jnp.int32, (C, GLA_DK), 0)
    row_s = lax.broadcasted_iota(jnp.int32, (SUB, GLA_DK), 0)
    lane_s = lax.broadcasted_iota(jnp.int32, (SUB, C), 1)
    neg_inf = jnp.float32(-jnp.inf)
    nw = nw_ref[...]

    def scores_bounded(q, k, b):
        s_rows = []
        for I in range(nsub):
            rs = slice(I * SUB, (I + 1) * SUB)
            ref_b = b[I * SUB:I * SUB + 1, :]
            qs = q[rs] * jnp.exp(b[rs] - ref_b)
            ks = k * jnp.exp(jnp.where(row_c < (I + 1) * SUB, ref_b - b, neg_inf))
            s_rows.append(_dot_nt(qs.astype(BF16), ks.astype(BF16)))
        return jnp.where(c_i <= r_i, jnp.concatenate(s_rows, axis=0), 0.0)

    def scores_any(q, k, b):
        s_rows = []
        for I in range(nsub):
            rs = slice(I * SUB, (I + 1) * SUB)
            q_i, k_i, b_i = q[rs], k[rs], b[rs]
            if I == 0:
                s_i = jnp.zeros((SUB, C), F32)
            else:
                ref_b = b[I * SUB - 1:I * SUB, :]
                qs = q_i * jnp.exp(b_i - ref_b)
                ks = k * jnp.exp(jnp.where(row_c < I * SUB, ref_b - b, neg_inf))
                s_i = _dot_nt(qs.astype(BF16), ks.astype(BF16))
            for j in range(SUB):
                diff = jnp.where(row_s >= j, b_i - b_i[j:j + 1, :], neg_inf)
                colv = jnp.sum(q_i * k_i[j:j + 1, :] * jnp.exp(diff), axis=1, keepdims=True)
                s_i = jnp.where(lane_s == I * SUB + j, colv, s_i)
            s_rows.append(s_i)
        return jnp.concatenate(s_rows, axis=0)

    def all_heads(scores):
        for h in range(GLA_HEADS):
            kc = slice(h * GLA_DK, (h + 1) * GLA_DK)
            vc = slice(h * GLA_DV, (h + 1) * GLA_DV)
            b = b_all[:, kc]
            q = q_ref[:, kc] * (GLA_DK ** -0.5)
            k = k_ref[:, kc]
            v = v_ref[:, vc]
            st = state_ref[h]
            b_last = b[C - 1:C, :]

            o = _dot_nt((q * jnp.exp(b)).astype(BF16), st.astype(BF16))
            o = o + _dot(scores(q, k, b).astype(BF16), v.astype(BF16))

            kd = k * jnp.exp(b_last - b)
            state_ref[h] = st * jnp.exp(b_last) + _dot(v.T.astype(BF16), kd.astype(BF16))

            ms = jnp.mean(o * o, axis=-1, keepdims=True)
            y = o * lax.rsqrt(ms + RMS_EPS) * nw
            g = gr_ref[:, vc]
            o_ref[:, vc] = (y * (g * jax.nn.sigmoid(g))).astype(o_ref.dtype)

    bounded = flag_ref[pl.program_id(0) * pl.num_programs(1) + pl.program_id(1)] != 0
    lax.cond(bounded, lambda: all_heads(scores_bounded), lambda: all_heads(scores_any))


def _gla(proj, gate_w, gate_b, norm_w, B, S):
    T = B * S
    NC = S // GLA_CHUNK
    b_all, peak = _gla_decay(proj, gate_w, gate_b)
    bounded = ((GLA_SUB - 1) * peak[:, 0] < GLA_SAFE_EXPONENT).astype(jnp.int32)
    q_blk = (3 * D_MODEL) // GLA_KEY_DIM
    v_blk = (3 * D_MODEL + 2 * GLA_KEY_DIM) // GLA_VAL_DIM
    rows = lambda width, col: pl.BlockSpec((GLA_CHUNK, width), lambda b, c, flags: (b * NC + c, col))
    return pl.pallas_call(
        _gla_kernel,
        out_shape=jax.ShapeDtypeStruct((T, GLA_VAL_DIM), BF16),
        grid_spec=pltpu.PrefetchScalarGridSpec(
            num_scalar_prefetch=1,
            grid=(B, NC),
            in_specs=[
                rows(GLA_KEY_DIM, q_blk), rows(GLA_KEY_DIM, q_blk + 1),
                rows(GLA_VAL_DIM, v_blk), rows(GLA_VAL_DIM, v_blk + 1),
                rows(GLA_KEY_DIM, 0),
                pl.BlockSpec((1, GLA_DV), lambda b, c, flags: (0, 0)),
            ],
            out_specs=rows(GLA_VAL_DIM, 0),
            scratch_shapes=[pltpu.VMEM((GLA_HEADS, GLA_DV, GLA_DK), F32)],
        ),
        compiler_params=_cparams("parallel", "arbitrary"),
        name="gla",
    )(bounded, proj, proj, proj, proj, b_all, norm_w.reshape(1, GLA_DV))


def _merge_kernel(oa_ref, ob_ref, ga_ref, gb_ref, x_ref, wa_ref, wb_ref, wo_ref, bma_ref, bmb_ref,
                  g1_ref, sc2_ref, sh2_ref, nw_ref, wrh_ref, wrl_ref, br_ref,
                  x1_ref, h2_ref, e4_ref, w4_ref, p4_ref, cnt_ref, run_ref):
    tm = MERGE_TM

    @pl.when(pl.program_id(0) == 0)
    def _():
        run_ref[...] = jnp.zeros_like(run_ref)

    y_a = _dot(oa_ref[...], wa_ref[...])
    y_b = _dot(ob_ref[...], wb_ref[...])
    g_a = jax.nn.sigmoid(ga_ref[...] + bma_ref[...])
    g_b = jax.nn.sigmoid(gb_ref[...] + bmb_ref[...])
    mix = _dot((g_a * y_a + g_b * y_b).astype(BF16), wo_ref[...])
    x1 = x_ref[...] + g1_ref[...] * mix
    x1_ref[...] = x1

    ms = jnp.mean(x1 * x1, axis=-1, keepdims=True)
    h2 = x1 * lax.rsqrt(ms + RMS_EPS) * nw_ref[...] * (1.0 + sc2_ref[...]) + sh2_ref[...]
    for s in range(D_MODEL // LANES):
        h2_ref[pl.ds(s, tm, stride=SUBLANES), :] = h2[:, s * LANES:(s + 1) * LANES]

    h_top = _truncate_to_bf16_bits(h2, pltpu.bitcast)
    h_hi = h_top.astype(BF16)
    h_lo = (h2 - h_top).astype(BF16)
    logits = (_dot(h_hi, wrh_ref[...]) + (_dot(h_lo, wrh_ref[...]) + _dot(h_hi, wrl_ref[...]))) + br_ref[...]
    lane = lax.broadcasted_iota(jnp.int32, (tm, LANES), 1)
    lane_f = lane.astype(F32)
    neg_inf = jnp.float32(-jnp.inf)
    g = jnp.where(lane < N_EXPERTS, logits, neg_inf)
    picks, tops = [], []
    for _ in range(TOP_K):
        mx = jnp.max(g, axis=1, keepdims=True)
        idx = _first_max_lane(g, lane_f)
        pick = lane_f == idx
        picks.append((pick, idx))
        tops.append(mx)
        g = jnp.where(pick, neg_inf, g)
    ex = [jnp.exp(t - tops[0]) for t in tops]
    denom = ex[0] + ex[1] + ex[2] + ex[3]

    sel = sum(jnp.where(pick, 1.0, 0.0) for pick, _ in picks)
    r_i = lax.broadcasted_iota(jnp.int32, (tm, tm), 0)
    c_i = lax.broadcasted_iota(jnp.int32, (tm, tm), 1)
    lower = jnp.where(c_i < r_i, 1.0, 0.0).astype(BF16)
    rank = run_ref[0:1, :] + _dot(lower, sel.astype(BF16))
    run_new = run_ref[0:1, :] + jnp.sum(sel, axis=0, keepdims=True)
    run_ref[...] = jnp.broadcast_to(run_new, run_ref.shape)
    cnt_ref[...] = jnp.broadcast_to(run_new, cnt_ref.shape)

    e4 = jnp.zeros((tm, LANES), jnp.int32)
    w4 = jnp.zeros((tm, LANES), F32)
    p4 = jnp.zeros((tm, LANES), F32)
    for r in range(TOP_K):
        pick, idx = picks[r]
        pos = jnp.sum(jnp.where(pick, rank, 0.0), axis=1, keepdims=True)
        e4 = jnp.where(lane == r, idx.astype(jnp.int32), e4)
        w4 = jnp.where(lane == r, ex[r] / denom, w4)
        p4 = jnp.where(lane == r, pos, p4)
    e4_ref[...] = e4
    w4_ref[...] = w4
    p4_ref[...] = p4


def _merge_and_route(o_a, o_b, proj, x2, w_a, w_b, w_o, b_merge, mod, norm2_w, w_router, b_router, S):
    T = x2.shape[0]
    tm = MERGE_TM
    tiles_per_batch = S // tm
    full = lambda shape: pl.BlockSpec(shape, lambda i: tuple(0 for _ in shape))
    row = lambda width, col: pl.BlockSpec((tm, width), lambda i: (i, col))
    modv = lambda k: pl.BlockSpec((None, 1, D_MODEL), lambda i: (i // tiles_per_batch, 0, k))
    wr = jnp.zeros((D_MODEL, LANES), F32).at[:, :N_EXPERTS].set(w_router)
    wr_top = _truncate_to_bf16_bits(wr, lax.bitcast_convert_type)
    wr_hi = wr_top.astype(BF16)
    wr_lo = (wr - wr_top).astype(BF16)
    br = jnp.zeros((1, LANES), F32).at[0, :N_EXPERTS].set(b_router)
    bm = b_merge.reshape(1, 2 * D_MODEL)
    return pl.pallas_call(
        _merge_kernel,
        out_shape=(
            jax.ShapeDtypeStruct((T, D_MODEL), F32),
            jax.ShapeDtypeStruct((T * SUBLANES, LANES), F32),
            jax.ShapeDtypeStruct((T, LANES), jnp.int32),
            jax.ShapeDtypeStruct((T, LANES), F32),
            jax.ShapeDtypeStruct((T, LANES), F32),
            jax.ShapeDtypeStruct((SUBLANES, LANES), F32),
        ),
        grid=(T // tm,),
        in_specs=[
            row(D_MODEL, 0), row(D_MODEL, 0),
            row(D_MODEL, COL_GATES // D_MODEL), row(D_MODEL, COL_GATES // D_MODEL + 1),
            row(D_MODEL, 0),
            full((D_MODEL, D_MODEL)), full((D_MODEL, D_MODEL)), full((D_MODEL, D_MODEL)),
            pl.BlockSpec((1, D_MODEL), lambda i: (0, 0)), pl.BlockSpec((1, D_MODEL), lambda i: (0, 1)),
            modv(2), modv(4), modv(3),
            full((1, D_MODEL)), full((D_MODEL, LANES)), full((D_MODEL, LANES)), full((1, LANES)),
        ],
        out_specs=(
            row(D_MODEL, 0),
            pl.BlockSpec((tm * SUBLANES, LANES), lambda i: (i, 0)),
            row(LANES, 0), row(LANES, 0), row(LANES, 0),
            full((SUBLANES, LANES)),
        ),
        scratch_shapes=[pltpu.VMEM((SUBLANES, LANES), F32)],
        compiler_params=_cparams("arbitrary"),
        name="merge_route",
    )(o_a, o_b, proj, proj, x2, w_a, w_b, w_o, bm, bm, mod, mod, mod,
      norm2_w.reshape(1, D_MODEL), wr_hi, wr_lo, br)


def _row_slab(ref, row):
    return ref.at[pl.ds(pl.multiple_of(row * SUBLANES, SUBLANES), SUBLANES), :]


def _wait_slabs(src_ref, dst_ref, sem, count):
    def wait(a, c):
        pltpu.make_async_copy(_row_slab(src_ref, 0), _row_slab(dst_ref, 0), sem).wait()
        return c

    lax.fori_loop(0, count, wait, 0, unroll=DMA_WAIT_UNROLL)


def _dispatch_kernel(dest_ref, pad_lo_ref, pad_hi_ref, nu_ref, h_ref, xs_ref, zero_ref, sem, pad_sem, blk_sem):
    tm = COMBINE_TM
    base = pl.program_id(0) * (tm * TOP_K)
    blk_rows = EXPERT_ROWS * SUBLANES
    n_blocks = xs_ref.shape[0] // blk_rows

    @pl.when(pl.program_id(0) == 0)
    def _():
        zero_ref[...] = jnp.zeros_like(zero_ref)
        zero_slab = zero_ref.at[pl.ds(0, SUBLANES), :]

        def block_copy(j):
            dst = xs_ref.at[pl.ds(pl.multiple_of(j * blk_rows, blk_rows), blk_rows), :]
            return pltpu.make_async_copy(zero_ref, dst, blk_sem)

        def fill(r, c):
            pltpu.make_async_copy(zero_slab, _row_slab(xs_ref, r), pad_sem).start()
            return c

        def drain(r, c):
            pltpu.make_async_copy(zero_slab, _row_slab(xs_ref, 0), pad_sem).wait()
            return c

        def fill_block(j, c):
            block_copy(j).start()
            return c

        def drain_block(j, c):
            block_copy(j).wait()
            return c

        for e in range(N_EXPERTS):
            lax.fori_loop(pad_lo_ref[e], pad_hi_ref[e], fill, 0)
        lax.fori_loop(nu_ref[0], n_blocks, fill_block, 0)
        for e in range(N_EXPERTS):
            lax.fori_loop(pad_lo_ref[e], pad_hi_ref[e], drain, 0)
        lax.fori_loop(nu_ref[0], n_blocks, drain_block, 0)

    def start(t, c):
        src = _row_slab(h_ref, t)
        for kk in range(TOP_K):
            pltpu.make_async_copy(src, _row_slab(xs_ref, dest_ref[base + t * TOP_K + kk]), sem).start(
                priority=kk % DMA_PRIORITIES)
        return c

    lax.fori_loop(0, tm, start, 0, unroll=DMA_ISSUE_UNROLL)
    _wait_slabs(h_ref, xs_ref, sem, tm * TOP_K)


def _dispatch(dest, pad_lo, pad_hi, n_used, h2_slabs, P):
    T = h2_slabs.shape[0] // SUBLANES
    tm = COMBINE_TM
    return pl.pallas_call(
        _dispatch_kernel,
        out_shape=jax.ShapeDtypeStruct((P * SUBLANES, LANES), F32),
        grid_spec=pltpu.PrefetchScalarGridSpec(
            num_scalar_prefetch=4,
            grid=(T // tm,),
            in_specs=[pl.BlockSpec((tm * SUBLANES, LANES), lambda i, *_: (i, 0))],
            out_specs=pl.BlockSpec(memory_space=pl.ANY),
            scratch_shapes=[
                pltpu.VMEM((EXPERT_ROWS * SUBLANES, LANES), F32),
                pltpu.SemaphoreType.DMA,
                pltpu.SemaphoreType.DMA,
                pltpu.SemaphoreType.DMA,
            ],
        ),
        compiler_params=_cparams("arbitrary"),
        name="moe_dispatch",
    )(dest, pad_lo, pad_hi, n_used, h2_slabs)


def _expert_kernel(be_ref, nu_ref, xs_ref, wi_ref, bi_ref, wo_ref, bo_ref, perm_ref, ys_ref,
                   x_ref, wi16_ref, wo16_ref):
    R = EXPERT_ROWS
    j = pl.program_id(0)
    used = j < nu_ref[0]
    new_expert = (j == 0) | (be_ref[j] != be_ref[jnp.maximum(j - 1, 0)])

    @pl.when(used & new_expert)
    def _():
        perm = perm_ref[...]
        for g in range(D_MODEL // LANES):
            w = wi_ref[:, 2 * LANES * g:2 * LANES * (g + 1)].astype(BF16)
            sep = _dot(w, perm).astype(BF16)
            wi16_ref[:, LANES * g:LANES * (g + 1)] = sep[:, :LANES]
            wi16_ref[:, D_MODEL + LANES * g:D_MODEL + LANES * (g + 1)] = sep[:, LANES:]
        wo16_ref[...] = wo_ref[...].astype(BF16)

    @pl.when(used)
    def _():
        for s in range(D_MODEL // LANES):
            x_ref[:, s * LANES:(s + 1) * LANES] = xs_ref[pl.ds(s, R, stride=SUBLANES), :].astype(BF16)
        hid = _dot(x_ref[...], wi16_ref[...]) + bi_ref[...]
        glu = jnp.minimum(hid[:, :D_MODEL], SWIGLU_LIMIT)
        lin = jnp.clip(hid[:, D_MODEL:], -SWIGLU_LIMIT, SWIGLU_LIMIT)
        act = glu * jax.nn.sigmoid(SWIGLU_ALPHA * glu) * (lin + 1.0)
        out = _dot(act.astype(BF16), wo16_ref[...]) + bo_ref[...]
        for s in range(D_MODEL // LANES):
            ys_ref[pl.ds(s, R, stride=SUBLANES), :] = out[:, s * LANES:(s + 1) * LANES]

    @pl.when(j >= nu_ref[0])
    def _():
        ys_ref[...] = jnp.zeros_like(ys_ref)


def _experts(block_expert, n_used, xs, w_in, b_in, w_out, b_out):
    R = EXPERT_ROWS
    n_blocks = xs.shape[0] // (R * SUBLANES)
    src = jnp.arange(2 * LANES)[:, None]
    dst = jnp.arange(2 * LANES)[None, :]
    perm = jnp.where(src == jnp.where(dst < LANES, 2 * dst, 2 * (dst - LANES) + 1), 1.0, 0.0).astype(BF16)
    return pl.pallas_call(
        _expert_kernel,
        out_shape=jax.ShapeDtypeStruct(xs.shape, F32),
        grid_spec=pltpu.PrefetchScalarGridSpec(
            num_scalar_prefetch=2,
            grid=(n_blocks,),
            in_specs=[
                pl.BlockSpec((R * SUBLANES, LANES), lambda j, be, nu: (jnp.minimum(j, jnp.maximum(nu[0] - 1, 0)), 0)),
                pl.BlockSpec((None, D_MODEL, 2 * D_MODEL), lambda j, be, nu: (be[j], 0, 0)),
                pl.BlockSpec((None, 1, 2 * D_MODEL), lambda j, be, nu: (be[j], 0, 0)),
                pl.BlockSpec((None, D_MODEL, D_MODEL), lambda j, be, nu: (be[j], 0, 0)),
                pl.BlockSpec((None, 1, D_MODEL), lambda j, be, nu: (be[j], 0, 0)),
                pl.BlockSpec((2 * LANES, 2 * LANES), lambda j, be, nu: (0, 0)),
            ],
            out_specs=pl.BlockSpec((R * SUBLANES, LANES), lambda j, be, nu: (j, 0)),
            scratch_shapes=[
                pltpu.VMEM((R, D_MODEL), BF16),
                pltpu.VMEM((D_MODEL, 2 * D_MODEL), BF16),
                pltpu.VMEM((D_MODEL, D_MODEL), BF16),
            ],
        ),
        compiler_params=pltpu.CompilerParams(
            dimension_semantics=("arbitrary",), vmem_limit_bytes=EXPERT_VMEM_LIMIT),
        name="moe_experts",
    )(block_expert, n_used, xs, w_in, b_in, w_out, b_out, perm)


def _combine_kernel(dest_ref, ys_ref, w4_ref, x1_ref, g2_ref, nw_ref, o_ref, buf_ref, sem):
    tm = COMBINE_TM
    step = pl.program_id(0)
    slot = step % 2

    def gather(tile, into):
        base = tile * (tm * TOP_K)

        def start(t, c):
            for kk in range(TOP_K):
                pltpu.make_async_copy(_row_slab(ys_ref, dest_ref[base + t * TOP_K + kk]),
                                      _row_slab(buf_ref.at[into, kk], t),
                                      sem.at[into]).start(priority=kk % DMA_PRIORITIES)
            return c

        lax.fori_loop(0, tm, start, 0, unroll=DMA_ISSUE_UNROLL)

    @pl.when(step == 0)
    def _():
        gather(0, 0)

    @pl.when(step + 1 < pl.num_programs(0))
    def _():
        gather(step + 1, 1 - slot)

    _wait_slabs(ys_ref, buf_ref.at[slot, 0], sem.at[slot], tm * TOP_K)

    w4 = w4_ref[...]
    g2 = g2_ref[...]
    parts = []
    ssq = jnp.zeros((tm, 1), F32)
    for s in range(D_MODEL // LANES):
        cols = slice(s * LANES, (s + 1) * LANES)
        y = jnp.zeros((tm, LANES), F32)
        for kk in range(TOP_K):
            y = y + buf_ref[slot, kk, pl.ds(s, tm, stride=SUBLANES), :] * w4[:, kk:kk + 1]
        x2 = x1_ref[:, cols] + g2[:, cols] * y
        ssq = ssq + jnp.sum(x2 * x2, axis=1, keepdims=True)
        parts.append(x2)
    inv = lax.rsqrt(ssq / D_MODEL + RMS_EPS)
    nw = nw_ref[...]
    for s in range(D_MODEL // LANES):
        cols = slice(s * LANES, (s + 1) * LANES)
        o_ref[:, cols] = parts[s] * inv * nw[:, cols]


def _combine(dest, ys, w4, x1, mod, final_w, S):
    T = x1.shape[0]
    tm = COMBINE_TM
    tiles_per_batch = S // tm
    return pl.pallas_call(
        _combine_kernel,
        out_shape=jax.ShapeDtypeStruct((T, D_MODEL), F32),
        grid_spec=pltpu.PrefetchScalarGridSpec(
            num_scalar_prefetch=1,
            grid=(T // tm,),
            in_specs=[
                pl.BlockSpec(memory_space=pl.ANY),
                pl.BlockSpec((tm, LANES), lambda i, dest: (i, 0)),
                pl.BlockSpec((tm, D_MODEL), lambda i, dest: (i, 0)),
                pl.BlockSpec((None, 1, D_MODEL), lambda i, dest: (i // tiles_per_batch, 0, 5)),
                pl.BlockSpec((1, D_MODEL), lambda i, dest: (0, 0)),
            ],
            out_specs=pl.BlockSpec((tm, D_MODEL), lambda i, dest: (i, 0)),
            scratch_shapes=[
                pltpu.VMEM((2, TOP_K, tm * SUBLANES, LANES), F32),
                pltpu.SemaphoreType.DMA((2,)),
            ],
        ),
        compiler_params=_cparams("arbitrary"),
        name="moe_combine",
    )(dest, ys, w4, x1, mod, final_w.reshape(1, D_MODEL))


def _layer(x2, c_mod, B, S, norm1_w, w_in, gla_gate_w, gla_gate_b, gla_norm_w, w_o_moba, w_o_gla,
           b_merge, w_out, norm2_w, w_router, b_router, w_exp_in, b_exp_in, w_exp_out, b_exp_out):
    T = B * S
    glow_lo = N_MAIN
    glow_hi = N_MAIN + GLA_GATE_RANK
    w_proj = jnp.concatenate(
        [w_in[:, :glow_lo], w_in[:, glow_hi:], w_in[:, glow_lo:glow_hi],
         jnp.zeros((D_MODEL, LANES - GLA_GATE_RANK), F32)], axis=1).astype(BF16)
    proj = _input_projection(x2, norm1_w, c_mod, w_proj, S)

    tables = _rope_tables(S)
    kaug, vb, kmean = _moba_prep(proj, tables, B, S)
    NB = S // MOBA_BLOCK
    kmean = kmean[:, :NB].reshape(B, NB, MOBA_HEADS, MOBA_HEAD_DIM).transpose(0, 2, 1, 3)
    kmean = jnp.pad(kmean, ((0, 0), (0, 0), (0, -NB % BF16_SUBLANES), (0, 0)))
    qaug = _moba_qprep(proj, tables, kmean, B, S)
    o_a = _moba_attention(qaug, kaug, vb, B, S)
    o_b = _gla(proj, gla_gate_w, gla_gate_b, gla_norm_w, B, S)

    x1, h2, e4, w4, p4, cnt = _merge_and_route(
        o_a, o_b, proj, x2, w_o_moba.astype(BF16), w_o_gla.astype(BF16), w_out.astype(BF16),
        b_merge, c_mod, norm2_w, w_router, b_router, S)

    R = EXPERT_ROWS
    counts = cnt[0, :N_EXPERTS].astype(jnp.int32)
    padded = (counts + R - 1) // R * R
    padded_end = jnp.cumsum(padded)
    padded_start = padded_end - padded
    n_blocks = -(-(T * TOP_K + N_EXPERTS * (R - 1)) // R)
    P = n_blocks * R
    dest = (padded_start[e4[:, :TOP_K]] + p4[:, :TOP_K].astype(jnp.int32)).reshape(-1)
    block_row0 = jnp.arange(n_blocks, dtype=jnp.int32) * R
    block_expert = jnp.minimum(
        jnp.sum(padded_end[None, :] <= block_row0[:, None], axis=1), N_EXPERTS - 1).astype(jnp.int32)
    n_used = (padded_end[-1:] // R).astype(jnp.int32)

    xs = _dispatch(dest, padded_start + counts, padded_end, n_used, h2, P)
    b_ei = jnp.concatenate([b_exp_in[:, 0::2], b_exp_in[:, 1::2]], axis=-1)[:, None, :]
    ys = _experts(block_expert, n_used, xs, w_exp_in, b_ei, w_exp_out, b_exp_out[:, None, :])
    return x1, ys, dest, w4


def kernel(x, c, w_ada, b_ada, norm1_w, w_in, gla_gate_w, gla_gate_b, gla_norm_w, w_o_moba, w_o_gla, b_merge, w_out, norm2_w, w_router, b_router, w_exp_in, b_exp_in, w_exp_out, b_exp_out, final_norm_w):
    B, S, _ = x.shape
    depth = w_ada.shape[0]
    assert depth == 1, "the combine kernel fuses the final norm, so a single layer is supported"
    x2 = x.reshape(B * S, D_MODEL)
    l = 0
    mod = _modulation(c, w_ada[l], b_ada[l])
    x1, ys, dest, w4 = _layer(
        x2, mod, B, S, norm1_w[l], w_in[l], gla_gate_w[l], gla_gate_b[l], gla_norm_w[l],
        w_o_moba[l], w_o_gla[l], b_merge[l], w_out[l], norm2_w[l], w_router[l], b_router[l],
        w_exp_in[l], b_exp_in[l], w_exp_out[l], b_exp_out[l])
    out = _combine(dest, ys, w4, x1, mod, final_norm_w, S)
    return out.reshape(B, S, D_MODEL)
```

```python
import functools

import jax
import jax.numpy as jnp
from jax import lax
from jax.experimental import pallas as pl
from jax.experimental.pallas import tpu as pltpu

F32 = jnp.float32
BF16 = jnp.bfloat16
HIGHEST = lax.Precision.HIGHEST

D_MODEL = 1024
MOBA_HEADS = 8
MOBA_HEAD_DIM = 128
MOBA_BLOCK = 256
MOBA_TOPK = 3
ROPE_DIM = MOBA_HEAD_DIM // 4
ROPE_THETA = 500000.0
GLA_HEADS = 4
GLA_KEY_DIM = D_MODEL // 2
GLA_VAL_DIM = D_MODEL
GLA_DK = GLA_KEY_DIM // GLA_HEADS
GLA_DV = GLA_VAL_DIM // GLA_HEADS
GLA_GATE_RANK = 16
GLA_GATE_NORMALIZER = 16.0
N_EXPERTS = 32
TOP_K = 4
SWIGLU_ALPHA = 1.702
SWIGLU_LIMIT = 7.0
RMS_EPS = 1e-5
LOG2E = 1.4426950408889634

LANES = 128
SUBLANES = 8
BF16_SUBLANES = 16
VMEM_LIMIT = 48 * 1024 * 1024
EXPERT_VMEM_LIMIT = 56 * 1024 * 1024

INPROJ_TM = 1024
INPROJ_TN = 1664
MOBA_GROUP = 4
MOBA_SUM_ROWS = BF16_SUBLANES
GLA_CHUNK = 64
GLA_SUB = 16
GLA_DECAY_TM = 512
GLA_SAFE_EXPONENT = 80.0
MERGE_TM = 512
EXPERT_ROWS = 256
COMBINE_TM = 256
MASK_NEG = -1e30
DMA_ISSUE_UNROLL = 4
DMA_WAIT_UNROLL = 16
DMA_PRIORITIES = 2

N_MAIN = 3 * D_MODEL + 2 * GLA_KEY_DIM + 2 * GLA_VAL_DIM
COL_GATES = N_MAIN
COL_GLOW = N_MAIN + 2 * D_MODEL
N_PROJ = COL_GLOW + LANES


def _dot(a, b, **kw):
    return jnp.dot(a, b, preferred_element_type=F32, **kw)


def _dot_nt(a, b, **kw):
    return lax.dot_general(a, b, (((1,), (1,)), ((), ())), preferred_element_type=F32, **kw)


def _truncate_to_bf16_bits(x, bitcast):
    return bitcast(bitcast(x, jnp.uint32) & jnp.uint32(0xFFFF0000), F32)


def _first_max_lane(g, lane_f):
    mx = jnp.max(g, axis=1, keepdims=True)
    return jnp.min(jnp.where(g == mx, lane_f, float(LANES)), axis=1, keepdims=True)


def _cparams(*sem):
    return pltpu.CompilerParams(dimension_semantics=sem, vmem_limit_bytes=VMEM_LIMIT)


def _mod_kernel(c_ref, w_ref, b_ref, o_ref):
    c = c_ref[...]
    ca = c * jax.nn.sigmoid(c)
    o_ref[...] = _dot(ca, w_ref[...], precision=HIGHEST) + b_ref[...]


def _modulation(c, w_ada, b_ada):
    B = c.shape[0]
    n = w_ada.shape[1]
    c8 = jnp.zeros((SUBLANES, D_MODEL), F32).at[:B].set(c)
    out = pl.pallas_call(
        _mod_kernel,
        out_shape=jax.ShapeDtypeStruct((SUBLANES, n), F32),
        grid=(n // D_MODEL,),
        in_specs=[
            pl.BlockSpec((SUBLANES, D_MODEL), lambda j: (0, 0)),
            pl.BlockSpec((D_MODEL, D_MODEL), lambda j: (0, j)),
            pl.BlockSpec((1, D_MODEL), lambda j: (0, j)),
        ],
        out_specs=pl.BlockSpec((SUBLANES, D_MODEL), lambda j: (0, j)),
        compiler_params=_cparams("arbitrary"),
        name="adaln_mod",
    )(c8, w_ada, b_ada.reshape(1, n))
    return out[:B].reshape(B, 1, n)


def _inproj_kernel(x_ref, nw_ref, sc_ref, sh_ref, w_ref, o_ref, h_ref):
    @pl.when(pl.program_id(1) == 0)
    def _():
        x = x_ref[...]
        ms = jnp.mean(x * x, axis=-1, keepdims=True)
        y = x * lax.rsqrt(ms + RMS_EPS) * nw_ref[...]
        h_ref[...] = (y * (1.0 + sc_ref[...]) + sh_ref[...]).astype(BF16)

    o_ref[...] = _dot(h_ref[...], w_ref[...])


def _input_projection(x2, norm_w, mod, w_proj, S):
    T = x2.shape[0]
    tiles_per_batch = S // INPROJ_TM
    return pl.pallas_call(
        _inproj_kernel,
        out_shape=jax.ShapeDtypeStruct((T, N_PROJ), F32),
        grid=(T // INPROJ_TM, N_PROJ // INPROJ_TN),
        in_specs=[
            pl.BlockSpec((INPROJ_TM, D_MODEL), lambda i, j: (i, 0)),
            pl.BlockSpec((1, D_MODEL), lambda i, j: (0, 0)),
            pl.BlockSpec((None, 1, D_MODEL), lambda i, j: (i // tiles_per_batch, 0, 1)),
            pl.BlockSpec((None, 1, D_MODEL), lambda i, j: (i // tiles_per_batch, 0, 0)),
            pl.BlockSpec((D_MODEL, INPROJ_TN), lambda i, j: (0, j)),
        ],
        out_specs=pl.BlockSpec((INPROJ_TM, INPROJ_TN), lambda i, j: (i, j)),
        scratch_shapes=[pltpu.VMEM((INPROJ_TM, D_MODEL), BF16)],
        compiler_params=_cparams("parallel", "arbitrary"),
        name="inproj",
    )(x2, norm_w.reshape(1, D_MODEL), mod, mod, w_proj)


def _rope_tables(S):
    half = ROPE_DIM // 2
    inv_freq = jnp.float32(ROPE_THETA) ** (-jnp.arange(half, dtype=jnp.float32) * 2.0 / ROPE_DIM)
    ang = jnp.arange(S, dtype=jnp.float32)[:, None] * inv_freq[None, :]
    cos, sin = jnp.cos(ang), jnp.sin(ang)
    ones = jnp.ones((S, MOBA_HEAD_DIM - ROPE_DIM), F32)
    zeros_hi = jnp.zeros((S, MOBA_HEAD_DIM - half), F32)
    zeros_lo = jnp.zeros((S, half), F32)
    cos_f = jnp.concatenate([cos, cos, ones], axis=1)
    sin_a = jnp.concatenate([-sin, zeros_hi], axis=1)
    sin_b = jnp.concatenate([zeros_lo, sin, ones * 0.0], axis=1)
    return cos_f, sin_a, sin_b


def _rope(x, cos_f, sin_a, sin_b):
    half = ROPE_DIM // 2
    up = pltpu.roll(x, MOBA_HEAD_DIM - half, 1)
    dn = pltpu.roll(x, half, 1)
    return x * cos_f + up * sin_a + dn * sin_b


def _kprep_kernel(k_ref, v_ref, cos_ref, sa_ref, sb_ref, kaug_ref, vb_ref, kmean_ref):
    n = pl.program_id(1)
    cos_f, sin_a, sin_b = cos_ref[...], sa_ref[...], sb_ref[...]
    lane = lax.broadcasted_iota(jnp.int32, (MOBA_BLOCK, LANES), 1)
    onehot = jnp.where(lane == n, 1.0, 0.0).astype(BF16)
    for h in range(MOBA_HEADS):
        cols = slice(h * MOBA_HEAD_DIM, (h + 1) * MOBA_HEAD_DIM)
        kr = _rope(k_ref[:, cols], cos_f, sin_a, sin_b)
        kmean_ref[:, cols] = jnp.mean(kr, axis=0, keepdims=True)
        kaug_ref[h, :, :MOBA_HEAD_DIM] = kr.astype(BF16)
        kaug_ref[h, :, MOBA_HEAD_DIM:] = onehot
        vb_ref[h, :MOBA_HEAD_DIM, :] = v_ref[:, cols].T.astype(BF16)
        vb_ref[h, MOBA_HEAD_DIM:, :] = jnp.ones((MOBA_SUM_ROWS, MOBA_BLOCK), BF16)


def _moba_prep(proj, tables, B, S):
    NB = S // MOBA_BLOCK
    NBP = NB
    src = lambda n: n
    tab_spec = pl.BlockSpec((MOBA_BLOCK, LANES), lambda b, n: (src(n), 0))
    return pl.pallas_call(
        _kprep_kernel,
        out_shape=(
            jax.ShapeDtypeStruct((B, MOBA_HEADS, NBP * MOBA_BLOCK, 2 * MOBA_HEAD_DIM), BF16),
            jax.ShapeDtypeStruct((B, MOBA_HEADS, MOBA_HEAD_DIM + MOBA_SUM_ROWS, NBP * MOBA_BLOCK), BF16),
            jax.ShapeDtypeStruct((B, NBP, 1, D_MODEL), F32),
        ),
        grid=(B, NBP),
        in_specs=[
            pl.BlockSpec((MOBA_BLOCK, D_MODEL), lambda b, n: (b * NB + src(n), 1)),
            pl.BlockSpec((MOBA_BLOCK, D_MODEL), lambda b, n: (b * NB + src(n), 2)),
            tab_spec, tab_spec, tab_spec,
        ],
        out_specs=(
            pl.BlockSpec((None, MOBA_HEADS, MOBA_BLOCK, 2 * MOBA_HEAD_DIM), lambda b, n: (b, 0, n, 0)),
            pl.BlockSpec((None, MOBA_HEADS, MOBA_HEAD_DIM + MOBA_SUM_ROWS, MOBA_BLOCK), lambda b, n: (b, 0, 0, n)),
            pl.BlockSpec((None, None, 1, D_MODEL), lambda b, n: (b, n, 0, 0)),
        ),
        compiler_params=_cparams("parallel", "parallel"),
        name="moba_kprep",
    )(proj, proj, *tables)


def _qprep_kernel(q_ref, cos_ref, sa_ref, sb_ref, kmean_ref, qaug_ref):
    i = pl.program_id(1)
    blk = MOBA_BLOCK
    scale = MOBA_HEAD_DIM ** -0.5
    cos_f, sin_a, sin_b = cos_ref[...], sa_ref[...], sb_ref[...]
    nbp = kmean_ref.shape[1]
    blk_id = lax.broadcasted_iota(jnp.int32, (nbp, blk), 0)
    blk_f = blk_id.astype(F32)
    neg_inf = jnp.float32(-jnp.inf)
    for h in range(MOBA_HEADS):
        cols = slice(h * MOBA_HEAD_DIM, (h + 1) * MOBA_HEAD_DIM)
        qr = _rope(q_ref[:, cols], cos_f, sin_a, sin_b)
        gate = _dot_nt(kmean_ref[h], qr, precision=HIGHEST)
        g = jnp.where(blk_id < i, gate, neg_inf)
        bias = jnp.where(blk_id == i, 0.0, MASK_NEG)
        for r in range(MOBA_TOPK):
            mx = jnp.max(g, axis=0, keepdims=True)
            first = jnp.min(jnp.where(g == mx, blk_f, float(nbp)), axis=0, keepdims=True)
            pick = blk_f == first
            unmask = jnp.where(i > r, 0.0, MASK_NEG)
            bias = jnp.where(pick, jnp.maximum(bias, unmask), bias)
            g = jnp.where(pick, neg_inf, g)
        qaug_ref[h, :MOBA_HEAD_DIM, :] = (qr * (scale * LOG2E)).T.astype(BF16)
        qaug_ref[h, MOBA_HEAD_DIM:MOBA_HEAD_DIM + nbp, :] = bias.astype(BF16)
        qaug_ref[h, MOBA_HEAD_DIM + nbp:, :] = jnp.zeros((MOBA_HEAD_DIM - nbp, blk), BF16)


def _moba_qprep(proj, tables, kmean, B, S):
    NB = S // MOBA_BLOCK
    tab_spec = pl.BlockSpec((MOBA_BLOCK, LANES), lambda b, i: (i, 0))
    return pl.pallas_call(
        _qprep_kernel,
        out_shape=jax.ShapeDtypeStruct((B, MOBA_HEADS, 2 * MOBA_HEAD_DIM, S), BF16),
        grid=(B, NB),
        in_specs=[
            pl.BlockSpec((MOBA_BLOCK, D_MODEL), lambda b, i: (b * NB + i, 0)),
            tab_spec, tab_spec, tab_spec,
            pl.BlockSpec((None, MOBA_HEADS, kmean.shape[2], MOBA_HEAD_DIM), lambda b, i: (b, 0, 0, 0)),
        ],
        out_specs=pl.BlockSpec((None, MOBA_HEADS, 2 * MOBA_HEAD_DIM, MOBA_BLOCK), lambda b, i: (b, 0, 0, i)),
        compiler_params=_cparams("parallel", "parallel"),
        name="moba_qprep",
    )(proj, *tables, kmean)


def _moba_kernel(q_ref, kaug_ref, v_ref, o_ref, m_ref, acc_ref, s_ref, mx_ref):
    n_full = pl.program_id(2)
    width = MOBA_GROUP * MOBA_BLOCK
    qt = q_ref[...]

    def scores(g):
        return _dot(kaug_ref[pl.ds(pl.multiple_of(g * width, width), width), :], qt)

    def prob_value(s, m, g):
        p = jnp.exp2(s - m).astype(BF16)
        return _dot(v_ref[:, pl.ds(pl.multiple_of(g * width, width), width)], p)

    def stage(g):
        s = scores(g)
        s_ref[...] = s
        mx_ref[...] = jnp.max(s, axis=0, keepdims=True)

    s_own = scores(n_full)
    stage(0)
    key = lax.broadcasted_iota(jnp.int32, (width, width), 0)
    qry = lax.broadcasted_iota(jnp.int32, (width, width), 1)
    s_own = jnp.where(key > qry, jnp.float32(-jnp.inf), s_own)
    m0 = jnp.max(s_own, axis=0, keepdims=True)
    m_ref[...] = m0
    acc_ref[...] = prob_value(s_own, m0, n_full)

    def body(g, carry):
        m_prev = m_ref[...]
        m_new = jnp.maximum(m_prev, mx_ref[...])
        alpha = jnp.exp2(m_prev - m_new)
        m_ref[...] = m_new
        pv = prob_value(s_ref[...], m_new, g)
        stage(jnp.minimum(g + 1, jnp.maximum(n_full - 1, 0)))
        acc_ref[...] = alpha * acc_ref[...] + pv
        return carry

    lax.fori_loop(0, n_full, body, 0)
    acc = acc_ref[...]
    out_t = acc[:MOBA_HEAD_DIM, :] / acc[MOBA_HEAD_DIM:MOBA_HEAD_DIM + 1, :]
    o_ref[...] = out_t.T.astype(o_ref.dtype)


def _moba_attention(qaug, kaug, vb, B, S):
    T = B * S
    tq = MOBA_GROUP * MOBA_BLOCK
    assert S % tq == 0
    NG = S // tq
    return pl.pallas_call(
        _moba_kernel,
        out_shape=jax.ShapeDtypeStruct((T, D_MODEL), BF16),
        grid=(B, MOBA_HEADS, NG),
        in_specs=[
            pl.BlockSpec((None, None, 2 * MOBA_HEAD_DIM, tq), lambda b, h, j: (b, h, 0, j)),
            pl.BlockSpec((None, None, S, 2 * MOBA_HEAD_DIM), lambda b, h, j: (b, h, 0, 0)),
            pl.BlockSpec((None, None, MOBA_HEAD_DIM + MOBA_SUM_ROWS, S), lambda b, h, j: (b, h, 0, 0)),
        ],
        out_specs=pl.BlockSpec((tq, MOBA_HEAD_DIM), lambda b, h, j: (b * NG + j, h)),
        scratch_shapes=[
            pltpu.VMEM((1, tq), F32),
            pltpu.VMEM((MOBA_HEAD_DIM + MOBA_SUM_ROWS, tq), F32),
            pltpu.VMEM((tq, tq), F32),
            pltpu.VMEM((1, tq), F32),
        ],
        compiler_params=_cparams("parallel", "parallel", "arbitrary"),
        name="moba_attn",
    )(qaug, kaug, vb)


def _gla_decay_kernel(gl_ref, gw_ref, gb_ref, b_ref, peak_ref):
    tm, C = GLA_DECAY_TM, GLA_CHUNK
    z = _dot(gl_ref[...].astype(BF16), gw_ref[...].astype(BF16)) + gb_ref[...]
    log_g = jax.nn.log_sigmoid(z) / GLA_GATE_NORMALIZER
    r_i = lax.broadcasted_iota(jnp.int32, (tm, tm), 0)
    c_i = lax.broadcasted_iota(jnp.int32, (tm, tm), 1)
    tri = jnp.where((c_i <= r_i) & (c_i >= (r_i // C) * C), 1.0, 0.0).astype(BF16)
    hi = _truncate_to_bf16_bits(log_g, pltpu.bitcast)
    rest = log_g - hi
    mid = _truncate_to_bf16_bits(rest, pltpu.bitcast)
    lo = rest - mid
    b_ref[...] = (_dot(tri, hi.astype(BF16)) + _dot(tri, mid.astype(BF16))) + _dot(tri, lo.astype(BF16))
    steps = -log_g
    for c in range(tm // C):
        pk = jnp.max(jnp.max(steps[c * C:(c + 1) * C, :], axis=1, keepdims=True), axis=0, keepdims=True)
        peak_ref[c:c + 1, :] = jnp.broadcast_to(pk, (1, LANES))


def _gla_decay(proj, gate_w, gate_b):
    T = proj.shape[0]
    tm = GLA_DECAY_TM
    gw = jnp.zeros((LANES, GLA_KEY_DIM), F32).at[:GLA_GATE_RANK].set(gate_w)
    return pl.pallas_call(
        _gla_decay_kernel,
        out_shape=(
            jax.ShapeDtypeStruct((T, GLA_KEY_DIM), F32),
            jax.ShapeDtypeStruct((T // GLA_CHUNK, LANES), F32),
        ),
        grid=(T // tm,),
        in_specs=[
            pl.BlockSpec((tm, LANES), lambda i: (i, COL_GLOW // LANES)),
            pl.BlockSpec((LANES, GLA_KEY_DIM), lambda i: (0, 0)),
            pl.BlockSpec((1, GLA_KEY_DIM), lambda i: (0, 0)),
        ],
        out_specs=(
            pl.BlockSpec((tm, GLA_KEY_DIM), lambda i: (i, 0)),
            pl.BlockSpec((tm // GLA_CHUNK, LANES), lambda i: (i, 0)),
        ),
        compiler_params=_cparams("parallel"),
        name="gla_decay",
    )(proj, gw, gate_b.reshape(1, GLA_KEY_DIM))


def _gla_kernel(flag_ref, q_ref, k_ref, v_ref, gr_ref, b_ref, nw_ref, o_ref, state_ref):
    C, SUB = GLA_CHUNK, GLA_SUB
    nsub = C // SUB

    @pl.when(pl.program_id(1) == 0)
    def _():
        state_ref[...] = jnp.zeros_like(state_ref)

    r_i = lax.broadcasted_iota(jnp.int32, (C, C), 0)
    c_i = lax.broadcasted_iota(jnp.int32, (C, C), 1)
    b_all = b_ref[...]

    row_c = lax.broadcasted_iota(jnp.int32, (C, GLA_DK), 0)
    row_s = lax.broadcasted_iota(jnp.int32, (SUB, GLA_DK), 0)
    lane_s = lax.broadcasted_iota(jnp.int32, (SUB, C), 1)
    neg_inf = jnp.float32(-jnp.inf)
    nw = nw_ref[...]

    def scores_bounded(q, k, b):
        s_rows = []
        for I in range(nsub):
            rs = slice(I * SUB, (I + 1) * SUB)
            ref_b = b[I * SUB:I * SUB + 1, :]
            qs = q[rs] * jnp.exp(b[rs] - ref_b)
            ks = k * jnp.exp(jnp.where(row_c < (I + 1) * SUB, ref_b - b, neg_inf))
            s_rows.append(_dot_nt(qs.astype(BF16), ks.astype(BF16)))
        return jnp.where(c_i <= r_i, jnp.concatenate(s_rows, axis=0), 0.0)

    def scores_any(q, k, b):
        s_rows = []
        for I in range(nsub):
            rs = slice(I * SUB, (I + 1) * SUB)
            q_i, k_i, b_i = q[rs], k[rs], b[rs]
            if I == 0:
                s_i = jnp.zeros((SUB, C), F32)
            else:
                ref_b = b[I * SUB - 1:I * SUB, :]
                qs = q_i * jnp.exp(b_i - ref_b)
                ks = k * jnp.exp(jnp.where(row_c < I * SUB, ref_b - b, neg_inf))
                s_i = _dot_nt(qs.astype(BF16), ks.astype(BF16))
            for j in range(SUB):
                diff = jnp.where(row_s >= j, b_i - b_i[j:j + 1, :], neg_inf)
                colv = jnp.sum(q_i * k_i[j:j + 1, :] * jnp.exp(diff), axis=1, keepdims=True)
                s_i = jnp.where(lane_s == I * SUB + j, colv, s_i)
            s_rows.append(s_i)
        return jnp.concatenate(s_rows, axis=0)

    def all_heads(scores):
        for h in range(GLA_HEADS):
            kc = slice(h * GLA_DK, (h + 1) * GLA_DK)
            vc = slice(h * GLA_DV, (h + 1) * GLA_DV)
            b = b_all[:, kc]
            q = q_ref[:, kc] * (GLA_DK ** -0.5)
            k = k_ref[:, kc]
            v = v_ref[:, vc]
            st = state_ref[h]
            b_last = b[C - 1:C, :]

            o = _dot_nt((q * jnp.exp(b)).astype(BF16), st.astype(BF16))
            o = o + _dot(scores(q, k, b).astype(BF16), v.astype(BF16))

            kd = k * jnp.exp(b_last - b)
            state_ref[h] = st * jnp.exp(b_last) + _dot(v.T.astype(BF16), kd.astype(BF16))

            ms = jnp.mean(o * o, axis=-1, keepdims=True)
            y = o * lax.rsqrt(ms + RMS_EPS) * nw
            g = gr_ref[:, vc]
            o_ref[:, vc] = (y * (g * jax.nn.sigmoid(g))).astype(o_ref.dtype)

    bounded = flag_ref[pl.program_id(0) * pl.num_programs(1) + pl.program_id(1)] != 0
    lax.cond(bounded, lambda: all_heads(scores_bounded), lambda: all_heads(scores_any))


def _gla(proj, gate_w, gate_b, norm_w, B, S):
    T = B * S
    NC = S // GLA_CHUNK
    b_all, peak = _gla_decay(proj, gate_w, gate_b)
    bounded = ((GLA_SUB - 1) * peak[:, 0] < GLA_SAFE_EXPONENT).astype(jnp.int32)
    q_blk = (3 * D_MODEL) // GLA_KEY_DIM
    v_blk = (3 * D_MODEL + 2 * GLA_KEY_DIM) // GLA_VAL_DIM
    rows = lambda width, col: pl.BlockSpec((GLA_CHUNK, width), lambda b, c, flags: (b * NC + c, col))
    return pl.pallas_call(
        _gla_kernel,
        out_shape=jax.ShapeDtypeStruct((T, GLA_VAL_DIM), BF16),
        grid_spec=pltpu.PrefetchScalarGridSpec(
            num_scalar_prefetch=1,
            grid=(B, NC),
            in_specs=[
                rows(GLA_KEY_DIM, q_blk), rows(GLA_KEY_DIM, q_blk + 1),
                rows(GLA_VAL_DIM, v_blk), rows(GLA_VAL_DIM, v_blk + 1),
                rows(GLA_KEY_DIM, 0),
                pl.BlockSpec((1, GLA_DV), lambda b, c, flags: (0, 0)),
            ],
            out_specs=rows(GLA_VAL_DIM, 0),
            scratch_shapes=[pltpu.VMEM((GLA_HEADS, GLA_DV, GLA_DK), F32)],
        ),
        compiler_params=_cparams("parallel", "arbitrary"),
        name="gla",
    )(bounded, proj, proj, proj, proj, b_all, norm_w.reshape(1, GLA_DV))


def _merge_kernel(oa_ref, ob_ref, ga_ref, gb_ref, x_ref, wa_ref, wb_ref, wo_ref, bma_ref, bmb_ref,
                  g1_ref, sc2_ref, sh2_ref, nw_ref, wrh_ref, wrl_ref, br_ref,
                  x1_ref, h2_ref, e4_ref, w4_ref, p4_ref, cnt_ref, run_ref):
    tm = MERGE_TM

    @pl.when(pl.program_id(0) == 0)
    def _():
        run_ref[...] = jnp.zeros_like(run_ref)

    y_a = _dot(oa_ref[...], wa_ref[...])
    y_b = _dot(ob_ref[...], wb_ref[...])
    g_a = jax.nn.sigmoid(ga_ref[...] + bma_ref[...])
    g_b = jax.nn.sigmoid(gb_ref[...] + bmb_ref[...])
    mix = _dot((g_a * y_a + g_b * y_b).astype(BF16), wo_ref[...])
    x1 = x_ref[...] + g1_ref[...] * mix
    x1_ref[...] = x1

    ms = jnp.mean(x1 * x1, axis=-1, keepdims=True)
    h2 = x1 * lax.rsqrt(ms + RMS_EPS) * nw_ref[...] * (1.0 + sc2_ref[...]) + sh2_ref[...]
    for s in range(D_MODEL // LANES):
        h2_ref[pl.ds(s, tm, stride=SUBLANES), :] = h2[:, s * LANES:(s + 1) * LANES]

    h_top = _truncate_to_bf16_bits(h2, pltpu.bitcast)
    h_hi = h_top.astype(BF16)
    h_lo = (h2 - h_top).astype(BF16)
    logits = (_dot(h_hi, wrh_ref[...]) + (_dot(h_lo, wrh_ref[...]) + _dot(h_hi, wrl_ref[...]))) + br_ref[...]
    lane = lax.broadcasted_iota(jnp.int32, (tm, LANES), 1)
    lane_f = lane.astype(F32)
    neg_inf = jnp.float32(-jnp.inf)
    g = jnp.where(lane < N_EXPERTS, logits, neg_inf)
    picks, tops = [], []
    for _ in range(TOP_K):
        mx = jnp.max(g, axis=1, keepdims=True)
        idx = _first_max_lane(g, lane_f)
        pick = lane_f == idx
        picks.append((pick, idx))
        tops.append(mx)
        g = jnp.where(pick, neg_inf, g)
    ex = [jnp.exp(t - tops[0]) for t in tops]
    denom = ex[0] + ex[1] + ex[2] + ex[3]

    sel = sum(jnp.where(pick, 1.0, 0.0) for pick, _ in picks)
    r_i = lax.broadcasted_iota(jnp.int32, (tm, tm), 0)
    c_i = lax.broadcasted_iota(jnp.int32, (tm, tm), 1)
    lower = jnp.where(c_i < r_i, 1.0, 0.0).astype(BF16)
    rank = run_ref[0:1, :] + _dot(lower, sel.astype(BF16))
    run_new = run_ref[0:1, :] + jnp.sum(sel, axis=0, keepdims=True)
    run_ref[...] = jnp.broadcast_to(run_new, run_ref.shape)
    cnt_ref[...] = jnp.broadcast_to(run_new, cnt_ref.shape)

    e4 = jnp.zeros((tm, LANES), jnp.int32)
    w4 = jnp.zeros((tm, LANES), F32)
    p4 = jnp.zeros((tm, LANES), F32)
    for r in range(TOP_K):
        pick, idx = picks[r]
        pos = jnp.sum(jnp.where(pick, rank, 0.0), axis=1, keepdims=True)
        e4 = jnp.where(lane == r, idx.astype(jnp.int32), e4)
        w4 = jnp.where(lane == r, ex[r] / denom, w4)
        p4 = jnp.where(lane == r, pos, p4)
    e4_ref[...] = e4
    w4_ref[...] = w4
    p4_ref[...] = p4


def _merge_and_route(o_a, o_b, proj, x2, w_a, w_b, w_o, b_merge, mod, norm2_w, w_router, b_router, S):
    T = x2.shape[0]
    tm = MERGE_TM
    tiles_per_batch = S // tm
    full = lambda shape: pl.BlockSpec(shape, lambda i: tuple(0 for _ in shape))
    row = lambda width, col: pl.BlockSpec((tm, width), lambda i: (i, col))
    modv = lambda k: pl.BlockSpec((None, 1, D_MODEL), lambda i: (i // tiles_per_batch, 0, k))
    wr = jnp.zeros((D_MODEL, LANES), F32).at[:, :N_EXPERTS].set(w_router)
    wr_top = _truncate_to_bf16_bits(wr, lax.bitcast_convert_type)
    wr_hi = wr_top.astype(BF16)
    wr_lo = (wr - wr_top).astype(BF16)
    br = jnp.zeros((1, LANES), F32).at[0, :N_EXPERTS].set(b_router)
    bm = b_merge.reshape(1, 2 * D_MODEL)
    return pl.pallas_call(
        _merge_kernel,
        out_shape=(
            jax.ShapeDtypeStruct((T, D_MODEL), F32),
            jax.ShapeDtypeStruct((T * SUBLANES, LANES), F32),
            jax.ShapeDtypeStruct((T, LANES), jnp.int32),
            jax.ShapeDtypeStruct((T, LANES), F32),
            jax.ShapeDtypeStruct((T, LANES), F32),
            jax.ShapeDtypeStruct((SUBLANES, LANES), F32),
        ),
        grid=(T // tm,),
        in_specs=[
            row(D_MODEL, 0), row(D_MODEL, 0),
            row(D_MODEL, COL_GATES // D_MODEL), row(D_MODEL, COL_GATES // D_MODEL + 1),
            row(D_MODEL, 0),
            full((D_MODEL, D_MODEL)), full((D_MODEL, D_MODEL)), full((D_MODEL, D_MODEL)),
            pl.BlockSpec((1, D_MODEL), lambda i: (0, 0)), pl.BlockSpec((1, D_MODEL), lambda i: (0, 1)),
            modv(2), modv(4), modv(3),
            full((1, D_MODEL)), full((D_MODEL, LANES)), full((D_MODEL, LANES)), full((1, LANES)),
        ],
        out_specs=(
            row(D_MODEL, 0),
            pl.BlockSpec((tm * SUBLANES, LANES), lambda i: (i, 0)),
            row(LANES, 0), row(LANES, 0), row(LANES, 0),
            full((SUBLANES, LANES)),
        ),
        scratch_shapes=[pltpu.VMEM((SUBLANES, LANES), F32)],
        compiler_params=_cparams("arbitrary"),
        name="merge_route",
    )(o_a, o_b, proj, proj, x2, w_a, w_b, w_o, bm, bm, mod, mod, mod,
      norm2_w.reshape(1, D_MODEL), wr_hi, wr_lo, br)


def _row_slab(ref, row):
    return ref.at[pl.ds(pl.multiple_of(row * SUBLANES, SUBLANES), SUBLANES), :]


def _wait_slabs(src_ref, dst_ref, sem, count):
    def wait(a, c):
        pltpu.make_async_copy(_row_slab(src_ref, 0), _row_slab(dst_ref, 0), sem).wait()
        return c

    lax.fori_loop(0, count, wait, 0, unroll=DMA_WAIT_UNROLL)


def _dispatch_kernel(dest_ref, pad_lo_ref, pad_hi_ref, nu_ref, h_ref, xs_ref, zero_ref, sem, pad_sem, blk_sem):
    tm = COMBINE_TM
    base = pl.program_id(0) * (tm * TOP_K)
    blk_rows = EXPERT_ROWS * SUBLANES
    n_blocks = xs_ref.shape[0] // blk_rows

    zero_slab = zero_ref.at[pl.ds(0, SUBLANES), :]

    def block_copy(j):
        dst = xs_ref.at[pl.ds(pl.multiple_of(j * blk_rows, blk_rows), blk_rows), :]
        return pltpu.make_async_copy(zero_ref, dst, blk_sem)

    @pl.when(pl.program_id(0) == 0)
    def _():
        zero_ref[...] = jnp.zeros_like(zero_ref)

        def fill(r, c):
            pltpu.make_async_copy(zero_slab, _row_slab(xs_ref, r), pad_sem).start()
            return c

        def fill_block(j, c):
            block_copy(j).start()
            return c

        for e in range(N_EXPERTS):
            lax.fori_loop(pad_lo_ref[e], pad_hi_ref[e], fill, 0)
        lax.fori_loop(nu_ref[0], n_blocks, fill_block, 0)

    @pl.when(pl.program_id(0) == pl.num_programs(0) - 1)
    def _():
        def drain(r, c):
            pltpu.make_async_copy(zero_slab, _row_slab(xs_ref, 0), pad_sem).wait()
            return c

        def drain_block(j, c):
            block_copy(j).wait()
            return c

        for e in range(N_EXPERTS):
            lax.fori_loop(pad_lo_ref[e], pad_hi_ref[e], drain, 0)
        lax.fori_loop(nu_ref[0], n_blocks, drain_block, 0)

    def start(t, c):
        src = _row_slab(h_ref, t)
        for kk in range(TOP_K):
            pltpu.make_async_copy(src, _row_slab(xs_ref, dest_ref[base + t * TOP_K + kk]), sem).start(
                priority=kk % DMA_PRIORITIES)
        return c

    lax.fori_loop(0, tm, start, 0, unroll=DMA_ISSUE_UNROLL)
    _wait_slabs(h_ref, xs_ref, sem, tm * TOP_K)


def _dispatch(dest, pad_lo, pad_hi, n_used, h2_slabs, P):
    T = h2_slabs.shape[0] // SUBLANES
    tm = COMBINE_TM
    return pl.pallas_call(
        _dispatch_kernel,
        out_shape=jax.ShapeDtypeStruct((P * SUBLANES, LANES), F32),
        grid_spec=pltpu.PrefetchScalarGridSpec(
            num_scalar_prefetch=4,
            grid=(T // tm,),
            in_specs=[pl.BlockSpec((tm * SUBLANES, LANES), lambda i, *_: (i, 0))],
            out_specs=pl.BlockSpec(memory_space=pl.ANY),
            scratch_shapes=[
                pltpu.VMEM((EXPERT_ROWS * SUBLANES, LANES), F32),
                pltpu.SemaphoreType.DMA,
                pltpu.SemaphoreType.DMA,
                pltpu.SemaphoreType.DMA,
            ],
        ),
        compiler_params=_cparams("arbitrary"),
        name="moe_dispatch",
    )(dest, pad_lo, pad_hi, n_used, h2_slabs)


def _expert_kernel(be_ref, nu_ref, slot_ref, nxt_ref, xs_ref, wi_hbm, bi_ref, wo_hbm, bo_ref, perm_ref, ys_ref,
                   x_ref, wi16_ref, wo16_ref, wi32_ref, wo32_ref, wi_sem, wo_sem):
    R = EXPERT_ROWS
    j = pl.program_id(0)
    used = j < nu_ref[0]
    new_expert = (j == 0) | (be_ref[j] != be_ref[jnp.maximum(j - 1, 0)])

    def weight_copies(e, s):
        return (pltpu.make_async_copy(wi_hbm.at[e], wi32_ref.at[s], wi_sem.at[s]),
                pltpu.make_async_copy(wo_hbm.at[e], wo32_ref.at[s], wo_sem.at[s]))

    @pl.when(used & (j == 0))
    def _():
        for cp in weight_copies(be_ref[0], slot_ref[0]):
            cp.start()

    @pl.when(used & new_expert)
    def _():
        s = slot_ref[j]
        for cp in weight_copies(be_ref[j], s):
            cp.wait()

        @pl.when(nxt_ref[j] >= 0)
        def _():
            for cp in weight_copies(nxt_ref[j], 1 - s):
                cp.start()

        perm = perm_ref[...]
        for g in range(D_MODEL // LANES):
            w = wi32_ref[s, :, 2 * LANES * g:2 * LANES * (g + 1)].astype(BF16)
            sep = _dot(w, perm).astype(BF16)
            wi16_ref[:, LANES * g:LANES * (g + 1)] = sep[:, :LANES]
            wi16_ref[:, D_MODEL + LANES * g:D_MODEL + LANES * (g + 1)] = sep[:, LANES:]
        wo16_ref[...] = wo32_ref[s].astype(BF16)

    @pl.when(used)
    def _():
        for s in range(D_MODEL // LANES):
            x_ref[:, s * LANES:(s + 1) * LANES] = xs_ref[pl.ds(s, R, stride=SUBLANES), :].astype(BF16)
        hid = _dot(x_ref[...], wi16_ref[...]) + bi_ref[...]
        glu = jnp.minimum(hid[:, :D_MODEL], SWIGLU_LIMIT)
        lin = jnp.clip(hid[:, D_MODEL:], -SWIGLU_LIMIT, SWIGLU_LIMIT)
        act = glu * jax.nn.sigmoid(SWIGLU_ALPHA * glu) * (lin + 1.0)
        out = _dot(act.astype(BF16), wo16_ref[...]) + bo_ref[...]
        for s in range(D_MODEL // LANES):
            ys_ref[pl.ds(s, R, stride=SUBLANES), :] = out[:, s * LANES:(s + 1) * LANES]

    @pl.when(j >= nu_ref[0])
    def _():
        ys_ref[...] = jnp.zeros_like(ys_ref)


def _experts(block_expert, n_used, group_slot, next_expert, xs, w_in, b_in, w_out, b_out):
    R = EXPERT_ROWS
    n_blocks = xs.shape[0] // (R * SUBLANES)
    src = jnp.arange(2 * LANES)[:, None]
    dst = jnp.arange(2 * LANES)[None, :]
    perm = jnp.where(src == jnp.where(dst < LANES, 2 * dst, 2 * (dst - LANES) + 1), 1.0, 0.0).astype(BF16)
    return pl.pallas_call(
        _expert_kernel,
        out_shape=jax.ShapeDtypeStruct(xs.shape, F32),
        grid_spec=pltpu.PrefetchScalarGridSpec(
            num_scalar_prefetch=4,
            grid=(n_blocks,),
            in_specs=[
                pl.BlockSpec((R * SUBLANES, LANES),
                             lambda j, be, nu, *_: (jnp.minimum(j, jnp.maximum(nu[0] - 1, 0)), 0)),
                pl.BlockSpec(memory_space=pl.ANY),
                pl.BlockSpec((None, 1, 2 * D_MODEL), lambda j, be, *_: (be[j], 0, 0)),
                pl.BlockSpec(memory_space=pl.ANY),
                pl.BlockSpec((None, 1, D_MODEL), lambda j, be, *_: (be[j], 0, 0)),
                pl.BlockSpec((2 * LANES, 2 * LANES), lambda j, *_: (0, 0)),
            ],
            out_specs=pl.BlockSpec((R * SUBLANES, LANES), lambda j, *_: (j, 0)),
            scratch_shapes=[
                pltpu.VMEM((R, D_MODEL), BF16),
                pltpu.VMEM((D_MODEL, 2 * D_MODEL), BF16),
                pltpu.VMEM((D_MODEL, D_MODEL), BF16),
                pltpu.VMEM((2, D_MODEL, 2 * D_MODEL), F32),
                pltpu.VMEM((2, D_MODEL, D_MODEL), F32),
                pltpu.SemaphoreType.DMA((2,)),
                pltpu.SemaphoreType.DMA((2,)),
            ],
        ),
        compiler_params=pltpu.CompilerParams(
            dimension_semantics=("arbitrary",), vmem_limit_bytes=EXPERT_VMEM_LIMIT),
        name="moe_experts",
    )(block_expert, n_used, group_slot, next_expert, xs, w_in, b_in, w_out, b_out, perm)


def _combine_kernel(dest_ref, ys_ref, w4_ref, x1_ref, g2_ref, nw_ref, o_ref, buf_ref, sem):
    tm = COMBINE_TM
    step = pl.program_id(0)
    slot = step % 2

    def gather(tile, into):
        base = tile * (tm * TOP_K)

        def start(t, c):
            for kk in range(TOP_K):
                pltpu.make_async_copy(_row_slab(ys_ref, dest_ref[base + t * TOP_K + kk]),
                                      _row_slab(buf_ref.at[into, kk], t),
                                      sem.at[into]).start(priority=kk % DMA_PRIORITIES)
            return c

        lax.fori_loop(0, tm, start, 0, unroll=DMA_ISSUE_UNROLL)

    @pl.when(step == 0)
    def _():
        gather(0, 0)

    @pl.when(step + 1 < pl.num_programs(0))
    def _():
        gather(step + 1, 1 - slot)

    _wait_slabs(ys_ref, buf_ref.at[slot, 0], sem.at[slot], tm * TOP_K)

    w4 = w4_ref[...]
    g2 = g2_ref[...]
    parts = []
    ssq = jnp.zeros((tm, 1), F32)
    for s in range(D_MODEL // LANES):
        cols = slice(s * LANES, (s + 1) * LANES)
        y = jnp.zeros((tm, LANES), F32)
        for kk in range(TOP_K):
            y = y + buf_ref[slot, kk, pl.ds(s, tm, stride=SUBLANES), :] * w4[:, kk:kk + 1]
        x2 = x1_ref[:, cols] + g2[:, cols] * y
        ssq = ssq + jnp.sum(x2 * x2, axis=1, keepdims=True)
        parts.append(x2)
    inv = lax.rsqrt(ssq / D_MODEL + RMS_EPS)
    nw = nw_ref[...]
    for s in range(D_MODEL // LANES):
        cols = slice(s * LANES, (s + 1) * LANES)
        o_ref[:, cols] = parts[s] * inv * nw[:, cols]


def _combine(dest, ys, w4, x1, mod, final_w, S):
    T = x1.shape[0]
    tm = COMBINE_TM
    tiles_per_batch = S // tm
    return pl.pallas_call(
        _combine_kernel,
        out_shape=jax.ShapeDtypeStruct((T, D_MODEL), F32),
        grid_spec=pltpu.PrefetchScalarGridSpec(
            num_scalar_prefetch=1,
            grid=(T // tm,),
            in_specs=[
                pl.BlockSpec(memory_space=pl.ANY),
                pl.BlockSpec((tm, LANES), lambda i, dest: (i, 0)),
                pl.BlockSpec((tm, D_MODEL), lambda i, dest: (i, 0)),
                pl.BlockSpec((None, 1, D_MODEL), lambda i, dest: (i // tiles_per_batch, 0, 5)),
                pl.BlockSpec((1, D_MODEL), lambda i, dest: (0, 0)),
            ],
            out_specs=pl.BlockSpec((tm, D_MODEL), lambda i, dest: (i, 0)),
            scratch_shapes=[
                pltpu.VMEM((2, TOP_K, tm * SUBLANES, LANES), F32),
                pltpu.SemaphoreType.DMA((2,)),
            ],
        ),
        compiler_params=_cparams("arbitrary"),
        name="moe_combine",
    )(dest, ys, w4, x1, mod, final_w.reshape(1, D_MODEL))


def _layer(x2, c_mod, B, S, norm1_w, w_in, gla_gate_w, gla_gate_b, gla_norm_w, w_o_moba, w_o_gla,
           b_merge, w_out, norm2_w, w_router, b_router, w_exp_in, b_exp_in, w_exp_out, b_exp_out):
    T = B * S
    glow_lo = N_MAIN
    glow_hi = N_MAIN + GLA_GATE_RANK
    w_in16 = w_in.astype(BF16)
    w_proj = jnp.concatenate(
        [w_in16[:, :glow_lo], w_in16[:, glow_hi:], w_in16[:, glow_lo:glow_hi],
         jnp.zeros((D_MODEL, LANES - GLA_GATE_RANK), BF16)], axis=1)
    proj = _input_projection(x2, norm1_w, c_mod, w_proj, S)

    tables = _rope_tables(S)
    kaug, vb, kmean = _moba_prep(proj, tables, B, S)
    NB = S // MOBA_BLOCK
    kmean = kmean[:, :NB].reshape(B, NB, MOBA_HEADS, MOBA_HEAD_DIM).transpose(0, 2, 1, 3)
    kmean = jnp.pad(kmean, ((0, 0), (0, 0), (0, -NB % BF16_SUBLANES), (0, 0)))
    qaug = _moba_qprep(proj, tables, kmean, B, S)
    o_a = _moba_attention(qaug, kaug, vb, B, S)
    o_b = _gla(proj, gla_gate_w, gla_gate_b, gla_norm_w, B, S)

    x1, h2, e4, w4, p4, cnt = _merge_and_route(
        o_a, o_b, proj, x2, w_o_moba.astype(BF16), w_o_gla.astype(BF16), w_out.astype(BF16),
        b_merge, c_mod, norm2_w, w_router, b_router, S)

    R = EXPERT_ROWS
    counts = cnt[0, :N_EXPERTS].astype(jnp.int32)
    padded = (counts + R - 1) // R * R
    padded_end = jnp.cumsum(padded)
    padded_start = padded_end - padded
    n_blocks = -(-(T * TOP_K + N_EXPERTS * (R - 1)) // R)
    P = n_blocks * R
    dest = (padded_start[e4[:, :TOP_K]] + p4[:, :TOP_K].astype(jnp.int32)).reshape(-1)
    block_row0 = jnp.arange(n_blocks, dtype=jnp.int32) * R
    block_expert = jnp.minimum(
        jnp.sum(padded_end[None, :] <= block_row0[:, None], axis=1), N_EXPERTS - 1).astype(jnp.int32)
    n_used = (padded_end[-1:] // R).astype(jnp.int32)
    nonempty = (counts > 0).astype(jnp.int32)
    group_slot = ((jnp.cumsum(nonempty) - 1)[block_expert] % 2).astype(jnp.int32)
    next_first = padded_end[block_expert] // R
    next_expert = jnp.where(next_first < n_used[0],
                            block_expert[jnp.minimum(next_first, n_blocks - 1)], -1).astype(jnp.int32)

    xs = _dispatch(dest, padded_start + counts, padded_end, n_used, h2, P)
    b_ei = jnp.concatenate([b_exp_in[:, 0::2], b_exp_in[:, 1::2]], axis=-1)[:, None, :]
    ys = _experts(block_expert, n_used, group_slot, next_expert, xs, w_exp_in, b_ei, w_exp_out,
                  b_exp_out[:, None, :])
    return x1, ys, dest, w4


def kernel(x, c, w_ada, b_ada, norm1_w, w_in, gla_gate_w, gla_gate_b, gla_norm_w, w_o_moba, w_o_gla, b_merge, w_out, norm2_w, w_router, b_router, w_exp_in, b_exp_in, w_exp_out, b_exp_out, final_norm_w):
    B, S, _ = x.shape
    depth = w_ada.shape[0]
    assert depth == 1, "the combine kernel fuses the final norm, so a single layer is supported"
    x2 = x.reshape(B * S, D_MODEL)
    l = 0
    mod = _modulation(c, w_ada[l], b_ada[l])
    x1, ys, dest, w4 = _layer(
        x2, mod, B, S, norm1_w[l], w_in[l], gla_gate_w[l], gla_gate_b[l], gla_norm_w[l],
        w_o_moba[l], w_o_gla[l], b_merge[l], w_out[l], norm2_w[l], w_router[l], b_router[l],
        w_exp_in[l], b_exp_in[l], w_exp_out[l], b_exp_out[l])
    out = _combine(dest, ys, w4, x1, mod, final_norm_w, S)
    return out.reshape(B, S, D_MODEL)
```

```python
import functools

import jax
import jax.numpy as jnp
from jax import lax
from jax.experimental import pallas as pl
from jax.experimental.pallas import tpu as pltpu

F32 = jnp.float32
BF16 = jnp.bfloat16
HIGHEST = lax.Precision.HIGHEST

D_MODEL = 1024
MOBA_HEADS = 8
MOBA_HEAD_DIM = 128
MOBA_BLOCK = 256
MOBA_TOPK = 3
ROPE_DIM = MOBA_HEAD_DIM // 4
ROPE_THETA = 500000.0
GLA_HEADS = 4
GLA_KEY_DIM = D_MODEL // 2
GLA_VAL_DIM = D_MODEL
GLA_DK = GLA_KEY_DIM // GLA_HEADS
GLA_DV = GLA_VAL_DIM // GLA_HEADS
GLA_GATE_RANK = 16
GLA_GATE_NORMALIZER = 16.0
N_EXPERTS = 32
TOP_K = 4
SWIGLU_ALPHA = 1.702
SWIGLU_LIMIT = 7.0
RMS_EPS = 1e-5
LOG2E = 1.4426950408889634

LANES = 128
SUBLANES = 8
BF16_SUBLANES = 16
VMEM_LIMIT = 48 * 1024 * 1024
EXPERT_VMEM_LIMIT = 56 * 1024 * 1024

INPROJ_TM = 1024
INPROJ_TN = 1664
MOBA_GROUP = 4
MOBA_SUM_ROWS = BF16_SUBLANES
GLA_CHUNK = 64
GLA_STEP_CHUNKS = 4
GLA_SUB = 16
GLA_DECAY_TM = 512
GLA_SAFE_EXPONENT = 80.0
MERGE_TM = 512
EXPERT_ROWS = 256
COMBINE_TM = 256
MASK_NEG = -1e30
DMA_ISSUE_UNROLL = 4
DMA_WAIT_UNROLL = 16
DMA_PRIORITIES = 2

N_MAIN = 3 * D_MODEL + 2 * GLA_KEY_DIM + 2 * GLA_VAL_DIM
COL_GATES = N_MAIN
COL_GLOW = N_MAIN + 2 * D_MODEL
N_PROJ = COL_GLOW + LANES


def _dot(a, b, **kw):
    return jnp.dot(a, b, preferred_element_type=F32, **kw)


def _dot_nt(a, b, **kw):
    return lax.dot_general(a, b, (((1,), (1,)), ((), ())), preferred_element_type=F32, **kw)


def _truncate_to_bf16_bits(x, bitcast):
    return bitcast(bitcast(x, jnp.uint32) & jnp.uint32(0xFFFF0000), F32)


def _first_max_lane(g, lane_f):
    mx = jnp.max(g, axis=1, keepdims=True)
    return jnp.min(jnp.where(g == mx, lane_f, float(LANES)), axis=1, keepdims=True)


def _cparams(*sem):
    return pltpu.CompilerParams(dimension_semantics=sem, vmem_limit_bytes=VMEM_LIMIT)


def _mod_kernel(c_ref, w_ref, b_ref, o_ref):
    c = c_ref[...]
    ca = c * jax.nn.sigmoid(c)
    o_ref[...] = _dot(ca, w_ref[...], precision=HIGHEST) + b_ref[...]


def _modulation(c, w_ada, b_ada):
    B = c.shape[0]
    n = w_ada.shape[1]
    c8 = jnp.zeros((SUBLANES, D_MODEL), F32).at[:B].set(c)
    out = pl.pallas_call(
        _mod_kernel,
        out_shape=jax.ShapeDtypeStruct((SUBLANES, n), F32),
        grid=(n // D_MODEL,),
        in_specs=[
            pl.BlockSpec((SUBLANES, D_MODEL), lambda j: (0, 0)),
            pl.BlockSpec((D_MODEL, D_MODEL), lambda j: (0, j)),
            pl.BlockSpec((1, D_MODEL), lambda j: (0, j)),
        ],
        out_specs=pl.BlockSpec((SUBLANES, D_MODEL), lambda j: (0, j)),
        compiler_params=_cparams("arbitrary"),
        name="adaln_mod",
    )(c8, w_ada, b_ada.reshape(1, n))
    return out[:B].reshape(B, 1, n)


def _inproj_kernel(x_ref, nw_ref, sc_ref, sh_ref, w_ref, o_ref, h_ref):
    @pl.when(pl.program_id(1) == 0)
    def _():
        x = x_ref[...]
        ms = jnp.mean(x * x, axis=-1, keepdims=True)
        y = x * lax.rsqrt(ms + RMS_EPS) * nw_ref[...]
        h_ref[...] = (y * (1.0 + sc_ref[...]) + sh_ref[...]).astype(BF16)

    o_ref[...] = _dot(h_ref[...], w_ref[...])


def _input_projection(x2, norm_w, mod, w_proj, S):
    T = x2.shape[0]
    tiles_per_batch = S // INPROJ_TM
    return pl.pallas_call(
        _inproj_kernel,
        out_shape=jax.ShapeDtypeStruct((T, N_PROJ), F32),
        grid=(T // INPROJ_TM, N_PROJ // INPROJ_TN),
        in_specs=[
            pl.BlockSpec((INPROJ_TM, D_MODEL), lambda i, j: (i, 0)),
            pl.BlockSpec((1, D_MODEL), lambda i, j: (0, 0)),
            pl.BlockSpec((None, 1, D_MODEL), lambda i, j: (i // tiles_per_batch, 0, 1)),
            pl.BlockSpec((None, 1, D_MODEL), lambda i, j: (i // tiles_per_batch, 0, 0)),
            pl.BlockSpec((D_MODEL, INPROJ_TN), lambda i, j: (0, j)),
        ],
        out_specs=pl.BlockSpec((INPROJ_TM, INPROJ_TN), lambda i, j: (i, j)),
        scratch_shapes=[pltpu.VMEM((INPROJ_TM, D_MODEL), BF16)],
        compiler_params=_cparams("parallel", "arbitrary"),
        name="inproj",
    )(x2, norm_w.reshape(1, D_MODEL), mod, mod, w_proj)


def _rope_tables(S):
    half = ROPE_DIM // 2
    inv_freq = jnp.float32(ROPE_THETA) ** (-jnp.arange(half, dtype=jnp.float32) * 2.0 / ROPE_DIM)
    ang = jnp.arange(S, dtype=jnp.float32)[:, None] * inv_freq[None, :]
    cos, sin = jnp.cos(ang), jnp.sin(ang)
    ones = jnp.ones((S, MOBA_HEAD_DIM - ROPE_DIM), F32)
    zeros_hi = jnp.zeros((S, MOBA_HEAD_DIM - half), F32)
    zeros_lo = jnp.zeros((S, half), F32)
    cos_f = jnp.concatenate([cos, cos, ones], axis=1)
    sin_a = jnp.concatenate([-sin, zeros_hi], axis=1)
    sin_b = jnp.concatenate([zeros_lo, sin, ones * 0.0], axis=1)
    return cos_f, sin_a, sin_b


def _rope(x, cos_f, sin_a, sin_b):
    half = ROPE_DIM // 2
    up = pltpu.roll(x, MOBA_HEAD_DIM - half, 1)
    dn = pltpu.roll(x, half, 1)
    return x * cos_f + up * sin_a + dn * sin_b


def _kprep_kernel(k_ref, v_ref, cos_ref, sa_ref, sb_ref, kaug_ref, vb_ref, kmean_ref):
    n = pl.program_id(1)
    cos_f, sin_a, sin_b = cos_ref[...], sa_ref[...], sb_ref[...]
    lane = lax.broadcasted_iota(jnp.int32, (MOBA_BLOCK, LANES), 1)
    onehot = jnp.where(lane == n, 1.0, 0.0).astype(BF16)
    for h in range(MOBA_HEADS):
        cols = slice(h * MOBA_HEAD_DIM, (h + 1) * MOBA_HEAD_DIM)
        kr = _rope(k_ref[:, cols], cos_f, sin_a, sin_b)
        kmean_ref[:, cols] = jnp.mean(kr, axis=0, keepdims=True)
        kaug_ref[h, :, :MOBA_HEAD_DIM] = kr.astype(BF16)
        kaug_ref[h, :, MOBA_HEAD_DIM:] = onehot
        vb_ref[h, :MOBA_HEAD_DIM, :] = v_ref[:, cols].T.astype(BF16)
        vb_ref[h, MOBA_HEAD_DIM:, :] = jnp.ones((MOBA_SUM_ROWS, MOBA_BLOCK), BF16)


def _moba_prep(proj, tables, B, S):
    NB = S // MOBA_BLOCK
    NBP = NB
    src = lambda n: n
    tab_spec = pl.BlockSpec((MOBA_BLOCK, LANES), lambda b, n: (src(n), 0))
    return pl.pallas_call(
        _kprep_kernel,
        out_shape=(
            jax.ShapeDtypeStruct((B, MOBA_HEADS, NBP * MOBA_BLOCK, 2 * MOBA_HEAD_DIM), BF16),
            jax.ShapeDtypeStruct((B, MOBA_HEADS, MOBA_HEAD_DIM + MOBA_SUM_ROWS, NBP * MOBA_BLOCK), BF16),
            jax.ShapeDtypeStruct((B, NBP, 1, D_MODEL), F32),
        ),
        grid=(B, NBP),
        in_specs=[
            pl.BlockSpec((MOBA_BLOCK, D_MODEL), lambda b, n: (b * NB + src(n), 1)),
            pl.BlockSpec((MOBA_BLOCK, D_MODEL), lambda b, n: (b * NB + src(n), 2)),
            tab_spec, tab_spec, tab_spec,
        ],
        out_specs=(
            pl.BlockSpec((None, MOBA_HEADS, MOBA_BLOCK, 2 * MOBA_HEAD_DIM), lambda b, n: (b, 0, n, 0)),
            pl.BlockSpec((None, MOBA_HEADS, MOBA_HEAD_DIM + MOBA_SUM_ROWS, MOBA_BLOCK), lambda b, n: (b, 0, 0, n)),
            pl.BlockSpec((None, None, 1, D_MODEL), lambda b, n: (b, n, 0, 0)),
        ),
        compiler_params=_cparams("parallel", "parallel"),
        name="moba_kprep",
    )(proj, proj, *tables)


def _qprep_kernel(q_ref, cos_ref, sa_ref, sb_ref, kmean_ref, qaug_ref):
    i = pl.program_id(1)
    blk = MOBA_BLOCK
    scale = MOBA_HEAD_DIM ** -0.5
    cos_f, sin_a, sin_b = cos_ref[...], sa_ref[...], sb_ref[...]
    nbp = kmean_ref.shape[1]
    blk_id = lax.broadcasted_iota(jnp.int32, (nbp, blk), 0)
    blk_f = blk_id.astype(F32)
    neg_inf = jnp.float32(-jnp.inf)
    for h in range(MOBA_HEADS):
        cols = slice(h * MOBA_HEAD_DIM, (h + 1) * MOBA_HEAD_DIM)
        qr = _rope(q_ref[:, cols], cos_f, sin_a, sin_b)
        gate = _dot_nt(kmean_ref[h], qr, precision=HIGHEST)
        g = jnp.where(blk_id < i, gate, neg_inf)
        bias = jnp.where(blk_id == i, 0.0, MASK_NEG)
        for r in range(MOBA_TOPK):
            mx = jnp.max(g, axis=0, keepdims=True)
            first = jnp.min(jnp.where(g == mx, blk_f, float(nbp)), axis=0, keepdims=True)
            pick = blk_f == first
            unmask = jnp.where(i > r, 0.0, MASK_NEG)
            bias = jnp.where(pick, jnp.maximum(bias, unmask), bias)
            g = jnp.where(pick, neg_inf, g)
        qaug_ref[h, :MOBA_HEAD_DIM, :] = (qr * (scale * LOG2E)).T.astype(BF16)
        qaug_ref[h, MOBA_HEAD_DIM:MOBA_HEAD_DIM + nbp, :] = bias.astype(BF16)
        qaug_ref[h, MOBA_HEAD_DIM + nbp:, :] = jnp.zeros((MOBA_HEAD_DIM - nbp, blk), BF16)


def _moba_qprep(proj, tables, kmean, B, S):
    NB = S // MOBA_BLOCK
    tab_spec = pl.BlockSpec((MOBA_BLOCK, LANES), lambda b, i: (i, 0))
    return pl.pallas_call(
        _qprep_kernel,
        out_shape=jax.ShapeDtypeStruct((B, MOBA_HEADS, 2 * MOBA_HEAD_DIM, S), BF16),
        grid=(B, NB),
        in_specs=[
            pl.BlockSpec((MOBA_BLOCK, D_MODEL), lambda b, i: (b * NB + i, 0)),
            tab_spec, tab_spec, tab_spec,
            pl.BlockSpec((None, MOBA_HEADS, kmean.shape[2], MOBA_HEAD_DIM), lambda b, i: (b, 0, 0, 0)),
        ],
        out_specs=pl.BlockSpec((None, MOBA_HEADS, 2 * MOBA_HEAD_DIM, MOBA_BLOCK), lambda b, i: (b, 0, 0, i)),
        compiler_params=_cparams("parallel", "parallel"),
        name="moba_qprep",
    )(proj, *tables, kmean)


def _moba_kernel(q_ref, kaug_ref, v_ref, o_ref, m_ref, acc_ref, s_ref, mx_ref):
    n_full = pl.program_id(2)
    width = MOBA_GROUP * MOBA_BLOCK
    qt = q_ref[...]

    def scores(g):
        return _dot(kaug_ref[pl.ds(pl.multiple_of(g * width, width), width), :], qt)

    def prob_value(s, m, g):
        p = jnp.exp2(s - m).astype(BF16)
        return _dot(v_ref[:, pl.ds(pl.multiple_of(g * width, width), width)], p)

    def stage(g):
        s = scores(g)
        s_ref[...] = s
        mx_ref[...] = jnp.max(s, axis=0, keepdims=True)

    s_own = scores(n_full)
    stage(0)
    key = lax.broadcasted_iota(jnp.int32, (width, width), 0)
    qry = lax.broadcasted_iota(jnp.int32, (width, width), 1)
    s_own = jnp.where(key > qry, jnp.float32(-jnp.inf), s_own)
    m0 = jnp.max(s_own, axis=0, keepdims=True)
    m_ref[...] = m0
    acc_ref[...] = prob_value(s_own, m0, n_full)

    def body(g, carry):
        m_prev = m_ref[...]
        m_new = jnp.maximum(m_prev, mx_ref[...])
        alpha = jnp.exp2(m_prev - m_new)
        m_ref[...] = m_new
        pv = prob_value(s_ref[...], m_new, g)
        stage(jnp.minimum(g + 1, jnp.maximum(n_full - 1, 0)))
        acc_ref[...] = alpha * acc_ref[...] + pv
        return carry

    lax.fori_loop(0, n_full, body, 0)
    acc = acc_ref[...]
    out_t = acc[:MOBA_HEAD_DIM, :] / acc[MOBA_HEAD_DIM:MOBA_HEAD_DIM + 1, :]
    o_ref[...] = out_t.T.astype(o_ref.dtype)


def _moba_attention(qaug, kaug, vb, B, S):
    T = B * S
    tq = MOBA_GROUP * MOBA_BLOCK
    assert S % tq == 0
    NG = S // tq
    return pl.pallas_call(
        _moba_kernel,
        out_shape=jax.ShapeDtypeStruct((T, D_MODEL), BF16),
        grid=(B, MOBA_HEADS, NG),
        in_specs=[
            pl.BlockSpec((None, None, 2 * MOBA_HEAD_DIM, tq), lambda b, h, j: (b, h, 0, j)),
            pl.BlockSpec((None, None, S, 2 * MOBA_HEAD_DIM), lambda b, h, j: (b, h, 0, 0)),
            pl.BlockSpec((None, None, MOBA_HEAD_DIM + MOBA_SUM_ROWS, S), lambda b, h, j: (b, h, 0, 0)),
        ],
        out_specs=pl.BlockSpec((tq, MOBA_HEAD_DIM), lambda b, h, j: (b * NG + j, h)),
        scratch_shapes=[
            pltpu.VMEM((1, tq), F32),
            pltpu.VMEM((MOBA_HEAD_DIM + MOBA_SUM_ROWS, tq), F32),
            pltpu.VMEM((tq, tq), F32),
            pltpu.VMEM((1, tq), F32),
        ],
        compiler_params=_cparams("parallel", "parallel", "arbitrary"),
        name="moba_attn",
    )(qaug, kaug, vb)


def _gla_decay_kernel(gl_ref, gw_ref, gb_ref, b_ref, peak_ref):
    tm, C = GLA_DECAY_TM, GLA_CHUNK
    z = _dot(gl_ref[...].astype(BF16), gw_ref[...].astype(BF16)) + gb_ref[...]
    log_g = jax.nn.log_sigmoid(z) / GLA_GATE_NORMALIZER
    r_i = lax.broadcasted_iota(jnp.int32, (tm, tm), 0)
    c_i = lax.broadcasted_iota(jnp.int32, (tm, tm), 1)
    tri = jnp.where((c_i <= r_i) & (c_i >= (r_i // C) * C), 1.0, 0.0).astype(BF16)
    hi = _truncate_to_bf16_bits(log_g, pltpu.bitcast)
    rest = log_g - hi
    mid = _truncate_to_bf16_bits(rest, pltpu.bitcast)
    lo = rest - mid
    b_ref[...] = (_dot(tri, hi.astype(BF16)) + _dot(tri, mid.astype(BF16))) + _dot(tri, lo.astype(BF16))
    steps = -log_g
    for c in range(tm // C):
        pk = jnp.max(jnp.max(steps[c * C:(c + 1) * C, :], axis=1, keepdims=True), axis=0, keepdims=True)
        peak_ref[c:c + 1, :] = jnp.broadcast_to(pk, (1, LANES))


def _gla_decay(proj, gate_w, gate_b):
    T = proj.shape[0]
    tm = GLA_DECAY_TM
    gw = jnp.zeros((LANES, GLA_KEY_DIM), F32).at[:GLA_GATE_RANK].set(gate_w)
    return pl.pallas_call(
        _gla_decay_kernel,
        out_shape=(
            jax.ShapeDtypeStruct((T, GLA_KEY_DIM), F32),
            jax.ShapeDtypeStruct((T // GLA_CHUNK, LANES), F32),
        ),
        grid=(T // tm,),
        in_specs=[
            pl.BlockSpec((tm, LANES), lambda i: (i, COL_GLOW // LANES)),
            pl.BlockSpec((LANES, GLA_KEY_DIM), lambda i: (0, 0)),
            pl.BlockSpec((1, GLA_KEY_DIM), lambda i: (0, 0)),
        ],
        out_specs=(
            pl.BlockSpec((tm, GLA_KEY_DIM), lambda i: (i, 0)),
            pl.BlockSpec((tm // GLA_CHUNK, LANES), lambda i: (i, 0)),
        ),
        compiler_params=_cparams("parallel"),
        name="gla_decay",
    )(proj, gw, gate_b.reshape(1, GLA_KEY_DIM))


def _gla_kernel(flag_ref, q_ref, k_ref, v_ref, gr_ref, b_ref, nw_ref, o_ref, state_ref):
    C, SUB = GLA_CHUNK, GLA_SUB
    nsub = C // SUB

    @pl.when(pl.program_id(1) == 0)
    def _():
        state_ref[...] = jnp.zeros_like(state_ref)

    r_i = lax.broadcasted_iota(jnp.int32, (C, C), 0)
    c_i = lax.broadcasted_iota(jnp.int32, (C, C), 1)

    row_c = lax.broadcasted_iota(jnp.int32, (C, GLA_DK), 0)
    row_s = lax.broadcasted_iota(jnp.int32, (SUB, GLA_DK), 0)
    lane_s = lax.broadcasted_iota(jnp.int32, (SUB, C), 1)
    neg_inf = jnp.float32(-jnp.inf)
    nw = nw_ref[...]

    def scores_bounded(q, k, b):
        s_rows = []
        for I in range(nsub):
            rs = slice(I * SUB, (I + 1) * SUB)
            ref_b = b[I * SUB:I * SUB + 1, :]
            qs = q[rs] * jnp.exp(b[rs] - ref_b)
            ks = k * jnp.exp(jnp.where(row_c < (I + 1) * SUB, ref_b - b, neg_inf))
            s_rows.append(_dot_nt(qs.astype(BF16), ks.astype(BF16)))
        return jnp.where(c_i <= r_i, jnp.concatenate(s_rows, axis=0), 0.0)

    def scores_any(q, k, b):
        s_rows = []
        for I in range(nsub):
            rs = slice(I * SUB, (I + 1) * SUB)
            q_i, k_i, b_i = q[rs], k[rs], b[rs]
            if I == 0:
                s_i = jnp.zeros((SUB, C), F32)
            else:
                ref_b = b[I * SUB - 1:I * SUB, :]
                qs = q_i * jnp.exp(b_i - ref_b)
                ks = k * jnp.exp(jnp.where(row_c < I * SUB, ref_b - b, neg_inf))
                s_i = _dot_nt(qs.astype(BF16), ks.astype(BF16))
            for j in range(SUB):
                diff = jnp.where(row_s >= j, b_i - b_i[j:j + 1, :], neg_inf)
                colv = jnp.sum(q_i * k_i[j:j + 1, :] * jnp.exp(diff), axis=1, keepdims=True)
                s_i = jnp.where(lane_s == I * SUB + j, colv, s_i)
            s_rows.append(s_i)
        return jnp.concatenate(s_rows, axis=0)

    def all_heads(scores):
        for h in range(GLA_HEADS):
            kc = slice(h * GLA_DK, (h + 1) * GLA_DK)
            vc = slice(h * GLA_DV, (h + 1) * GLA_DV)
            st = state_ref[h]
            for cc in range(GLA_STEP_CHUNKS):
                rows = slice(cc * C, (cc + 1) * C)
                b = b_ref[rows, kc]
                q = q_ref[rows, kc] * (GLA_DK ** -0.5)
                k = k_ref[rows, kc]
                v = v_ref[rows, vc]
                b_last = b[C - 1:C, :]

                o = _dot_nt((q * jnp.exp(b)).astype(BF16), st.astype(BF16))
                o = o + _dot(scores(q, k, b).astype(BF16), v.astype(BF16))

                kd = k * jnp.exp(b_last - b)
                st = st * jnp.exp(b_last) + _dot(v.T.astype(BF16), kd.astype(BF16))

                ms = jnp.mean(o * o, axis=-1, keepdims=True)
                y = o * lax.rsqrt(ms + RMS_EPS) * nw
                g = gr_ref[rows, vc]
                o_ref[rows, vc] = (y * (g * jax.nn.sigmoid(g))).astype(o_ref.dtype)
            state_ref[h] = st

    bounded = flag_ref[pl.program_id(0) * pl.num_programs(1) + pl.program_id(1)] != 0
    lax.cond(bounded, lambda: all_heads(scores_bounded), lambda: all_heads(scores_any))


def _gla(proj, gate_w, gate_b, norm_w, B, S):
    T = B * S
    step_rows = GLA_STEP_CHUNKS * GLA_CHUNK
    NC = S // step_rows
    b_all, peak = _gla_decay(proj, gate_w, gate_b)
    step_peak = jnp.max(peak[:, 0].reshape(-1, GLA_STEP_CHUNKS), axis=1)
    bounded = ((GLA_SUB - 1) * step_peak < GLA_SAFE_EXPONENT).astype(jnp.int32)
    q_blk = (3 * D_MODEL) // GLA_KEY_DIM
    v_blk = (3 * D_MODEL + 2 * GLA_KEY_DIM) // GLA_VAL_DIM
    rows = lambda width, col: pl.BlockSpec((step_rows, width), lambda b, c, flags: (b * NC + c, col))
    return pl.pallas_call(
        _gla_kernel,
        out_shape=jax.ShapeDtypeStruct((T, GLA_VAL_DIM), BF16),
        grid_spec=pltpu.PrefetchScalarGridSpec(
            num_scalar_prefetch=1,
            grid=(B, NC),
            in_specs=[
                rows(GLA_KEY_DIM, q_blk), rows(GLA_KEY_DIM, q_blk + 1),
                rows(GLA_VAL_DIM, v_blk), rows(GLA_VAL_DIM, v_blk + 1),
                rows(GLA_KEY_DIM, 0),
                pl.BlockSpec((1, GLA_DV), lambda b, c, flags: (0, 0)),
            ],
            out_specs=rows(GLA_VAL_DIM, 0),
            scratch_shapes=[pltpu.VMEM((GLA_HEADS, GLA_DV, GLA_DK), F32)],
        ),
        compiler_params=_cparams("parallel", "arbitrary"),
        name="gla",
    )(bounded, proj, proj, proj, proj, b_all, norm_w.reshape(1, GLA_DV))


def _merge_kernel(oa_ref, ob_ref, ga_ref, gb_ref, x_ref, wa_ref, wb_ref, wo_ref, bma_ref, bmb_ref,
                  g1_ref, sc2_ref, sh2_ref, nw_ref, wrh_ref, wrl_ref, br_ref,
                  x1_ref, h2_ref, e4_ref, w4_ref, p4_ref, cnt_ref, run_ref):
    tm = MERGE_TM

    @pl.when(pl.program_id(0) == 0)
    def _():
        run_ref[...] = jnp.zeros_like(run_ref)

    y_a = _dot(oa_ref[...], wa_ref[...])
    y_b = _dot(ob_ref[...], wb_ref[...])
    g_a = jax.nn.sigmoid(ga_ref[...] + bma_ref[...])
    g_b = jax.nn.sigmoid(gb_ref[...] + bmb_ref[...])
    mix = _dot((g_a * y_a + g_b * y_b).astype(BF16), wo_ref[...])
    x1 = x_ref[...] + g1_ref[...] * mix
    x1_ref[...] = x1

    ms = jnp.mean(x1 * x1, axis=-1, keepdims=True)
    h2 = x1 * lax.rsqrt(ms + RMS_EPS) * nw_ref[...] * (1.0 + sc2_ref[...]) + sh2_ref[...]
    for s in range(D_MODEL // LANES):
        h2_ref[pl.ds(s, tm, stride=SUBLANES), :] = h2[:, s * LANES:(s + 1) * LANES]

    h_top = _truncate_to_bf16_bits(h2, pltpu.bitcast)
    h_hi = h_top.astype(BF16)
    h_lo = (h2 - h_top).astype(BF16)
    logits = (_dot(h_hi, wrh_ref[...]) + (_dot(h_lo, wrh_ref[...]) + _dot(h_hi, wrl_ref[...]))) + br_ref[...]
    lane = lax.broadcasted_iota(jnp.int32, (tm, LANES), 1)
    lane_f = lane.astype(F32)
    neg_inf = jnp.float32(-jnp.inf)
    g = jnp.where(lane < N_EXPERTS, logits, neg_inf)
    picks, tops = [], []
    for _ in range(TOP_K):
        mx = jnp.max(g, axis=1, keepdims=True)
        idx = _first_max_lane(g, lane_f)
        pick = lane_f == idx
        picks.append((pick, idx))
        tops.append(mx)
        g = jnp.where(pick, neg_inf, g)
    ex = [jnp.exp(t - tops[0]) for t in tops]
    denom = ex[0] + ex[1] + ex[2] + ex[3]

    sel = sum(jnp.where(pick, 1.0, 0.0) for pick, _ in picks)
    r_i = lax.broadcasted_iota(jnp.int32, (tm, tm), 0)
    c_i = lax.broadcasted_iota(jnp.int32, (tm, tm), 1)
    lower = jnp.where(c_i < r_i, 1.0, 0.0).astype(BF16)
    rank = run_ref[0:1, :] + _dot(lower, sel.astype(BF16))
    run_new = run_ref[0:1, :] + jnp.sum(sel, axis=0, keepdims=True)
    run_ref[...] = jnp.broadcast_to(run_new, run_ref.shape)
    cnt_ref[...] = jnp.broadcast_to(run_new, cnt_ref.shape)

    e4 = jnp.zeros((tm, LANES), jnp.int32)
    w4 = jnp.zeros((tm, LANES), F32)
    p4 = jnp.zeros((tm, LANES), F32)
    for r in range(TOP_K):
        pick, idx = picks[r]
        pos = jnp.sum(jnp.where(pick, rank, 0.0), axis=1, keepdims=True)
        e4 = jnp.where(lane == r, idx.astype(jnp.int32), e4)
        w4 = jnp.where(lane == r, ex[r] / denom, w4)
        p4 = jnp.where(lane == r, pos, p4)
    e4_ref[...] = e4
    w4_ref[...] = w4
    p4_ref[...] = p4


def _merge_and_route(o_a, o_b, proj, x2, w_a, w_b, w_o, b_merge, mod, norm2_w, w_router, b_router, S):
    T = x2.shape[0]
    tm = MERGE_TM
    tiles_per_batch = S // tm
    full = lambda shape: pl.BlockSpec(shape, lambda i: tuple(0 for _ in shape))
    row = lambda width, col: pl.BlockSpec((tm, width), lambda i: (i, col))
    modv = lambda k: pl.BlockSpec((None, 1, D_MODEL), lambda i: (i // tiles_per_batch, 0, k))
    wr = jnp.zeros((D_MODEL, LANES), F32).at[:, :N_EXPERTS].set(w_router)
    wr_top = _truncate_to_bf16_bits(wr, lax.bitcast_convert_type)
    wr_hi = wr_top.astype(BF16)
    wr_lo = (wr - wr_top).astype(BF16)
    br = jnp.zeros((1, LANES), F32).at[0, :N_EXPERTS].set(b_router)
    bm = b_merge.reshape(1, 2 * D_MODEL)
    return pl.pallas_call(
        _merge_kernel,
        out_shape=(
            jax.ShapeDtypeStruct((T, D_MODEL), F32),
            jax.ShapeDtypeStruct((T * SUBLANES, LANES), F32),
            jax.ShapeDtypeStruct((T, LANES), jnp.int32),
            jax.ShapeDtypeStruct((T, LANES), F32),
            jax.ShapeDtypeStruct((T, LANES), F32),
            jax.ShapeDtypeStruct((SUBLANES, LANES), F32),
        ),
        grid=(T // tm,),
        in_specs=[
            row(D_MODEL, 0), row(D_MODEL, 0),
            row(D_MODEL, COL_GATES // D_MODEL), row(D_MODEL, COL_GATES // D_MODEL + 1),
            row(D_MODEL, 0),
            full((D_MODEL, D_MODEL)), full((D_MODEL, D_MODEL)), full((D_MODEL, D_MODEL)),
            pl.BlockSpec((1, D_MODEL), lambda i: (0, 0)), pl.BlockSpec((1, D_MODEL), lambda i: (0, 1)),
            modv(2), modv(4), modv(3),
            full((1, D_MODEL)), full((D_MODEL, LANES)), full((D_MODEL, LANES)), full((1, LANES)),
        ],
        out_specs=(
            row(D_MODEL, 0),
            pl.BlockSpec((tm * SUBLANES, LANES), lambda i: (i, 0)),
            row(LANES, 0), row(LANES, 0), row(LANES, 0),
            full((SUBLANES, LANES)),
        ),
        scratch_shapes=[pltpu.VMEM((SUBLANES, LANES), F32)],
        compiler_params=_cparams("arbitrary"),
        name="merge_route",
    )(o_a, o_b, proj, proj, x2, w_a, w_b, w_o, bm, bm, mod, mod, mod,
      norm2_w.reshape(1, D_MODEL), wr_hi, wr_lo, br)


def _row_slab(ref, row):
    return ref.at[pl.ds(pl.multiple_of(row * SUBLANES, SUBLANES), SUBLANES), :]


def _wait_slabs(src_ref, dst_ref, sem, count):
    def wait(a, c):
        pltpu.make_async_copy(_row_slab(src_ref, 0), _row_slab(dst_ref, 0), sem).wait()
        return c

    lax.fori_loop(0, count, wait, 0, unroll=DMA_WAIT_UNROLL)


def _dispatch_kernel(dest_ref, pad_lo_ref, pad_hi_ref, nu_ref, h_ref, xs_ref, zero_ref, sem, pad_sem, blk_sem):
    tm = COMBINE_TM
    base = pl.program_id(0) * (tm * TOP_K)
    blk_rows = EXPERT_ROWS * SUBLANES
    n_blocks = xs_ref.shape[0] // blk_rows

    def block_copy(j):
        dst = xs_ref.at[pl.ds(pl.multiple_of(j * blk_rows, blk_rows), blk_rows), :]
        return pltpu.make_async_copy(zero_ref, dst, blk_sem)

    def pad_copies(act):
        for e in range(N_EXPERTS):
            lo = pad_lo_ref[e]
            n = pad_hi_ref[e] - lo
            piece = EXPERT_ROWS // 2
            while piece >= 1:
                @pl.when((n & piece) != 0)
                def _(piece=piece, lo=lo, n=n):
                    first = lo + (n & ~(2 * piece - 1))
                    dst = xs_ref.at[pl.ds(pl.multiple_of(first * SUBLANES, SUBLANES), piece * SUBLANES), :]
                    act(pltpu.make_async_copy(zero_ref.at[pl.ds(0, piece * SUBLANES), :], dst, pad_sem))
                piece //= 2

    @pl.when(pl.program_id(0) == 0)
    def _():
        zero_ref[...] = jnp.zeros_like(zero_ref)

        def fill_block(j, c):
            block_copy(j).start()
            return c

        pad_copies(lambda cp: cp.start())
        lax.fori_loop(nu_ref[0], n_blocks, fill_block, 0)

    @pl.when(pl.program_id(0) == pl.num_programs(0) - 1)
    def _():
        def drain_block(j, c):
            block_copy(j).wait()
            return c

        pad_copies(lambda cp: cp.wait())
        lax.fori_loop(nu_ref[0], n_blocks, drain_block, 0)

    def start(t, c):
        src = _row_slab(h_ref, t)
        for kk in range(TOP_K):
            pltpu.make_async_copy(src, _row_slab(xs_ref, dest_ref[base + t * TOP_K + kk]), sem).start(
                priority=kk % DMA_PRIORITIES)
        return c

    lax.fori_loop(0, tm, start, 0, unroll=DMA_ISSUE_UNROLL)
    _wait_slabs(h_ref, xs_ref, sem, tm * TOP_K)


def _dispatch(dest, pad_lo, pad_hi, n_used, h2_slabs, P):
    T = h2_slabs.shape[0] // SUBLANES
    tm = COMBINE_TM
    return pl.pallas_call(
        _dispatch_kernel,
        out_shape=jax.ShapeDtypeStruct((P * SUBLANES, LANES), F32),
        grid_spec=pltpu.PrefetchScalarGridSpec(
            num_scalar_prefetch=4,
            grid=(T // tm,),
            in_specs=[pl.BlockSpec((tm * SUBLANES, LANES), lambda i, *_: (i, 0))],
            out_specs=pl.BlockSpec(memory_space=pl.ANY),
            scratch_shapes=[
                pltpu.VMEM((EXPERT_ROWS * SUBLANES, LANES), F32),
                pltpu.SemaphoreType.DMA,
                pltpu.SemaphoreType.DMA,
                pltpu.SemaphoreType.DMA,
            ],
        ),
        compiler_params=_cparams("arbitrary"),
        name="moe_dispatch",
    )(dest, pad_lo, pad_hi, n_used, h2_slabs)


def _expert_kernel(be_ref, nu_ref, slot_ref, nxt_ref, xs_ref, wi_hbm, bi_ref, wo_hbm, bo_ref, perm_ref, ys_ref,
                   x_ref, wi16_ref, wo16_ref, wi32_ref, wo32_ref, wi_sem, wo_sem):
    R = EXPERT_ROWS
    j = pl.program_id(0)
    used = j < nu_ref[0]
    new_expert = (j == 0) | (be_ref[j] != be_ref[jnp.maximum(j - 1, 0)])

    def weight_copies(e, s):
        return (pltpu.make_async_copy(wi_hbm.at[e], wi32_ref.at[s], wi_sem.at[s]),
                pltpu.make_async_copy(wo_hbm.at[e], wo32_ref.at[s], wo_sem.at[s]))

    @pl.when(used & (j == 0))
    def _():
        for cp in weight_copies(be_ref[0], slot_ref[0]):
            cp.start()

    @pl.when(used & new_expert)
    def _():
        s = slot_ref[j]
        for cp in weight_copies(be_ref[j], s):
            cp.wait()

        @pl.when(nxt_ref[j] >= 0)
        def _():
            for cp in weight_copies(nxt_ref[j], 1 - s):
                cp.start()

        perm = perm_ref[...]
        for g in range(D_MODEL // LANES):
            w = wi32_ref[s, :, 2 * LANES * g:2 * LANES * (g + 1)].astype(BF16)
            sep = _dot(w, perm).astype(BF16)
            wi16_ref[:, LANES * g:LANES * (g + 1)] = sep[:, :LANES]
            wi16_ref[:, D_MODEL + LANES * g:D_MODEL + LANES * (g + 1)] = sep[:, LANES:]
        wo16_ref[...] = wo32_ref[s].astype(BF16)

    @pl.when(used)
    def _():
        for s in range(D_MODEL // LANES):
            x_ref[:, s * LANES:(s + 1) * LANES] = xs_ref[pl.ds(s, R, stride=SUBLANES), :].astype(BF16)
        hid = _dot(x_ref[...], wi16_ref[...]) + bi_ref[...]
        glu = jnp.minimum(hid[:, :D_MODEL], SWIGLU_LIMIT)
        lin = jnp.clip(hid[:, D_MODEL:], -SWIGLU_LIMIT, SWIGLU_LIMIT)
        act = glu * jax.nn.sigmoid(SWIGLU_ALPHA * glu) * (lin + 1.0)
        out = _dot(act.astype(BF16), wo16_ref[...]) + bo_ref[...]
        for s in range(D_MODEL // LANES):
            ys_ref[pl.ds(s, R, stride=SUBLANES), :] = out[:, s * LANES:(s + 1) * LANES]

    @pl.when(j >= nu_ref[0])
    def _():
        ys_ref[...] = jnp.zeros_like(ys_ref)


def _experts(block_expert, n_used, group_slot, next_expert, xs, w_in, b_in, w_out, b_out):
    R = EXPERT_ROWS
    n_blocks = xs.shape[0] // (R * SUBLANES)
    src = jnp.arange(2 * LANES)[:, None]
    dst = jnp.arange(2 * LANES)[None, :]
    perm = jnp.where(src == jnp.where(dst < LANES, 2 * dst, 2 * (dst - LANES) + 1), 1.0, 0.0).astype(BF16)
    return pl.pallas_call(
        _expert_kernel,
        out_shape=jax.ShapeDtypeStruct(xs.shape, F32),
        grid_spec=pltpu.PrefetchScalarGridSpec(
            num_scalar_prefetch=4,
            grid=(n_blocks,),
            in_specs=[
                pl.BlockSpec((R * SUBLANES, LANES),
                             lambda j, be, nu, *_: (jnp.minimum(j, jnp.maximum(nu[0] - 1, 0)), 0)),
                pl.BlockSpec(memory_space=pl.ANY),
                pl.BlockSpec((None, 1, 2 * D_MODEL), lambda j, be, *_: (be[j], 0, 0)),
                pl.BlockSpec(memory_space=pl.ANY),
                pl.BlockSpec((None, 1, D_MODEL), lambda j, be, *_: (be[j], 0, 0)),
                pl.BlockSpec((2 * LANES, 2 * LANES), lambda j, *_: (0, 0)),
            ],
            out_specs=pl.BlockSpec((R * SUBLANES, LANES), lambda j, *_: (j, 0)),
            scratch_shapes=[
                pltpu.VMEM((R, D_MODEL), BF16),
                pltpu.VMEM((D_MODEL, 2 * D_MODEL), BF16),
                pltpu.VMEM((D_MODEL, D_MODEL), BF16),
                pltpu.VMEM((2, D_MODEL, 2 * D_MODEL), F32),
                pltpu.VMEM((2, D_MODEL, D_MODEL), F32),
                pltpu.SemaphoreType.DMA((2,)),
                pltpu.SemaphoreType.DMA((2,)),
            ],
        ),
        compiler_params=pltpu.CompilerParams(
            dimension_semantics=("arbitrary",), vmem_limit_bytes=EXPERT_VMEM_LIMIT),
        name="moe_experts",
    )(block_expert, n_used, group_slot, next_expert, xs, w_in, b_in, w_out, b_out, perm)


def _combine_kernel(dest_ref, ys_ref, w4_ref, x1_ref, g2_ref, nw_ref, o_ref, buf_ref, sem):
    tm = COMBINE_TM
    step = pl.program_id(0)
    slot = step % 2

    def gather(tile, into):
        base = tile * (tm * TOP_K)

        def start(t, c):
            for kk in range(TOP_K):
                pltpu.make_async_copy(_row_slab(ys_ref, dest_ref[base + t * TOP_K + kk]),
                                      _row_slab(buf_ref.at[into, kk], t),
                                      sem.at[into]).start(priority=kk % DMA_PRIORITIES)
            return c

        lax.fori_loop(0, tm, start, 0, unroll=DMA_ISSUE_UNROLL)

    @pl.when(step == 0)
    def _():
        gather(0, 0)

    @pl.when(step + 1 < pl.num_programs(0))
    def _():
        gather(step + 1, 1 - slot)

    _wait_slabs(ys_ref, buf_ref.at[slot, 0], sem.at[slot], tm * TOP_K)

    w4 = w4_ref[...]
    g2 = g2_ref[...]
    parts = []
    ssq = jnp.zeros((tm, 1), F32)
    for s in range(D_MODEL // LANES):
        cols = slice(s * LANES, (s + 1) * LANES)
        y = jnp.zeros((tm, LANES), F32)
        for kk in range(TOP_K):
            y = y + buf_ref[slot, kk, pl.ds(s, tm, stride=SUBLANES), :] * w4[:, kk:kk + 1]
        x2 = x1_ref[:, cols] + g2[:, cols] * y
        ssq = ssq + jnp.sum(x2 * x2, axis=1, keepdims=True)
        parts.append(x2)
    inv = lax.rsqrt(ssq / D_MODEL + RMS_EPS)
    nw = nw_ref[...]
    for s in range(D_MODEL // LANES):
        cols = slice(s * LANES, (s + 1) * LANES)
        o_ref[:, cols] = parts[s] * inv * nw[:, cols]


def _combine(dest, ys, w4, x1, mod, final_w, S):
    T = x1.shape[0]
    tm = COMBINE_TM
    tiles_per_batch = S // tm
    return pl.pallas_call(
        _combine_kernel,
        out_shape=jax.ShapeDtypeStruct((T, D_MODEL), F32),
        grid_spec=pltpu.PrefetchScalarGridSpec(
            num_scalar_prefetch=1,
            grid=(T // tm,),
            in_specs=[
                pl.BlockSpec(memory_space=pl.ANY),
                pl.BlockSpec((tm, LANES), lambda i, dest: (i, 0)),
                pl.BlockSpec((tm, D_MODEL), lambda i, dest: (i, 0)),
                pl.BlockSpec((None, 1, D_MODEL), lambda i, dest: (i // tiles_per_batch, 0, 5)),
                pl.BlockSpec((1, D_MODEL), lambda i, dest: (0, 0)),
            ],
            out_specs=pl.BlockSpec((tm, D_MODEL), lambda i, dest: (i, 0)),
            scratch_shapes=[
                pltpu.VMEM((2, TOP_K, tm * SUBLANES, LANES), F32),
                pltpu.SemaphoreType.DMA((2,)),
            ],
        ),
        compiler_params=_cparams("arbitrary"),
        name="moe_combine",
    )(dest, ys, w4, x1, mod, final_w.reshape(1, D_MODEL))


def _layer(x2, c_mod, B, S, norm1_w, w_in, gla_gate_w, gla_gate_b, gla_norm_w, w_o_moba, w_o_gla,
           b_merge, w_out, norm2_w, w_router, b_router, w_exp_in, b_exp_in, w_exp_out, b_exp_out):
    T = B * S
    glow_lo = N_MAIN
    glow_hi = N_MAIN + GLA_GATE_RANK
    w_in16 = w_in.astype(BF16)
    w_proj = jnp.concatenate(
        [w_in16[:, :glow_lo], w_in16[:, glow_hi:], w_in16[:, glow_lo:glow_hi],
         jnp.zeros((D_MODEL, LANES - GLA_GATE_RANK), BF16)], axis=1)
    proj = _input_projection(x2, norm1_w, c_mod, w_proj, S)

    tables = _rope_tables(S)
    kaug, vb, kmean = _moba_prep(proj, tables, B, S)
    NB = S // MOBA_BLOCK
    kmean = kmean[:, :NB].reshape(B, NB, MOBA_HEADS, MOBA_HEAD_DIM).transpose(0, 2, 1, 3)
    kmean = jnp.pad(kmean, ((0, 0), (0, 0), (0, -NB % BF16_SUBLANES), (0, 0)))
    qaug = _moba_qprep(proj, tables, kmean, B, S)
    o_a = _moba_attention(qaug, kaug, vb, B, S)
    o_b = _gla(proj, gla_gate_w, gla_gate_b, gla_norm_w, B, S)

    x1, h2, e4, w4, p4, cnt = _merge_and_route(
        o_a, o_b, proj, x2, w_o_moba.astype(BF16), w_o_gla.astype(BF16), w_out.astype(BF16),
        b_merge, c_mod, norm2_w, w_router, b_router, S)

    R = EXPERT_ROWS
    counts = cnt[0, :N_EXPERTS].astype(jnp.int32)
    padded = (counts + R - 1) // R * R
    padded_end = jnp.cumsum(padded)
    padded_start = padded_end - padded
    n_blocks = -(-(T * TOP_K + N_EXPERTS * (R - 1)) // R)
    P = n_blocks * R
    experts = jnp.arange(N_EXPERTS, dtype=jnp.int32)

    def lookup(table, idx):
        return jnp.sum(jnp.where(idx[..., None] == experts, table, 0), axis=-1)

    dest = (lookup(padded_start, e4[:, :TOP_K]) + p4[:, :TOP_K].astype(jnp.int32)).reshape(-1)
    block_row0 = jnp.arange(n_blocks, dtype=jnp.int32) * R
    block_expert = jnp.minimum(
        jnp.sum(padded_end[None, :] <= block_row0[:, None], axis=1), N_EXPERTS - 1).astype(jnp.int32)
    n_used = (padded_end[-1:] // R).astype(jnp.int32)
    nonempty = counts > 0
    group_of_expert = jnp.cumsum(nonempty.astype(jnp.int32)) - 1
    later = jnp.where((experts[None, :] > experts[:, None]) & nonempty[None, :], experts[None, :], N_EXPERTS)
    next_of_expert = jnp.min(later, axis=1)
    next_of_expert = jnp.where(next_of_expert < N_EXPERTS, next_of_expert, -1)
    group_slot = (lookup(group_of_expert, block_expert) % 2).astype(jnp.int32)
    next_expert = lookup(next_of_expert, block_expert).astype(jnp.int32)

    xs = _dispatch(dest, padded_start + counts, padded_end, n_used, h2, P)
    b_ei = jnp.concatenate([b_exp_in[:, 0::2], b_exp_in[:, 1::2]], axis=-1)[:, None, :]
    ys = _experts(block_expert, n_used, group_slot, next_expert, xs, w_exp_in, b_ei, w_exp_out,
                  b_exp_out[:, None, :])
    return x1, ys, dest, w4


def kernel(x, c, w_ada, b_ada, norm1_w, w_in, gla_gate_w, gla_gate_b, gla_norm_w, w_o_moba, w_o_gla, b_merge, w_out, norm2_w, w_router, b_router, w_exp_in, b_exp_in, w_exp_out, b_exp_out, final_norm_w):
    B, S, _ = x.shape
    depth = w_ada.shape[0]
    assert depth == 1, "the combine kernel fuses the final norm, so a single layer is supported"
    x2 = x.reshape(B * S, D_MODEL)
    l = 0
    mod = _modulation(c, w_ada[l], b_ada[l])
    x1, ys, dest, w4 = _layer(
        x2, mod, B, S, norm1_w[l], w_in[l], gla_gate_w[l], gla_gate_b[l], gla_norm_w[l],
        w_o_moba[l], w_o_gla[l], b_merge[l], w_out[l], norm2_w[l], w_router[l], b_router[l],
        w_exp_in[l], b_exp_in[l], w_exp_out[l], b_exp_out[l])
    out = _combine(dest, ys, w4, x1, mod, final_norm_w, S)
    return out.reshape(B, S, D_MODEL)
```

```python
import functools

import jax
import jax.numpy as jnp
from jax import lax
from jax.experimental import pallas as pl
from jax.experimental.pallas import tpu as pltpu

F32 = jnp.float32
BF16 = jnp.bfloat16
HIGHEST = lax.Precision.HIGHEST

D_MODEL = 1024
MOBA_HEADS = 8
MOBA_HEAD_DIM = 128
MOBA_BLOCK = 256
MOBA_TOPK = 3
ROPE_DIM = MOBA_HEAD_DIM // 4
ROPE_THETA = 500000.0
GLA_HEADS = 4
GLA_KEY_DIM = D_MODEL // 2
GLA_VAL_DIM = D_MODEL
GLA_DK = GLA_KEY_DIM // GLA_HEADS
GLA_DV = GLA_VAL_DIM // GLA_HEADS
GLA_GATE_RANK = 16
GLA_GATE_NORMALIZER = 16.0
N_EXPERTS = 32
TOP_K = 4
SWIGLU_ALPHA = 1.702
SWIGLU_LIMIT = 7.0
RMS_EPS = 1e-5
LOG2E = 1.4426950408889634

LANES = 128
SUBLANES = 8
BF16_SUBLANES = 16
VMEM_LIMIT = 48 * 1024 * 1024
EXPERT_VMEM_LIMIT = 56 * 1024 * 1024

INPROJ_TM = 1024
INPROJ_TN = 1664
MOBA_GROUP = 4
MOBA_SUM_ROWS = BF16_SUBLANES
GLA_CHUNK = 64
GLA_STEP_CHUNKS = 4
GLA_SUB = 16
GLA_DECAY_TM = 512
GLA_DECAY_TRI = 256
GLA_SAFE_EXPONENT = 80.0
MERGE_TM = 512
EXPERT_ROWS = 256
COMBINE_TM = 256
COMBINE_ROW_CHUNK = 32
MASK_NEG = -1e30
DMA_ISSUE_UNROLL = 4
DMA_WAIT_UNROLL = 16
DMA_PRIORITIES = 2

N_MAIN = 3 * D_MODEL + 2 * GLA_KEY_DIM + 2 * GLA_VAL_DIM
COL_GATES = N_MAIN
COL_GLOW = N_MAIN + 2 * D_MODEL
N_PROJ = COL_GLOW + LANES


def _dot(a, b, **kw):
    return jnp.dot(a, b, preferred_element_type=F32, **kw)


def _dot_nt(a, b, **kw):
    return lax.dot_general(a, b, (((1,), (1,)), ((), ())), preferred_element_type=F32, **kw)


def _truncate_to_bf16_bits(x, bitcast):
    return bitcast(bitcast(x, jnp.uint32) & jnp.uint32(0xFFFF0000), F32)


def _first_max_lane(g, lane_f):
    mx = jnp.max(g, axis=1, keepdims=True)
    return jnp.min(jnp.where(g == mx, lane_f, float(LANES)), axis=1, keepdims=True)


def _cparams(*sem):
    return pltpu.CompilerParams(dimension_semantics=sem, vmem_limit_bytes=VMEM_LIMIT)


def _mod_kernel(c_ref, w_ref, b_ref, o_ref):
    c = c_ref[...]
    ca = c * jax.nn.sigmoid(c)
    o_ref[...] = _dot(ca, w_ref[...], precision=HIGHEST) + b_ref[...]


def _modulation(c, w_ada, b_ada):
    B = c.shape[0]
    n = w_ada.shape[1]
    c8 = jnp.zeros((SUBLANES, D_MODEL), F32).at[:B].set(c)
    out = pl.pallas_call(
        _mod_kernel,
        out_shape=jax.ShapeDtypeStruct((SUBLANES, n), F32),
        grid=(n // D_MODEL,),
        in_specs=[
            pl.BlockSpec((SUBLANES, D_MODEL), lambda j: (0, 0)),
            pl.BlockSpec((D_MODEL, D_MODEL), lambda j: (0, j)),
            pl.BlockSpec((1, D_MODEL), lambda j: (0, j)),
        ],
        out_specs=pl.BlockSpec((SUBLANES, D_MODEL), lambda j: (0, j)),
        compiler_params=_cparams("arbitrary"),
        name="adaln_mod",
    )(c8, w_ada, b_ada.reshape(1, n))
    return out[:B].reshape(B, 1, n)


def _inproj_kernel(x_ref, nw_ref, sc_ref, sh_ref, w_ref, o_ref, h_ref):
    @pl.when(pl.program_id(1) == 0)
    def _():
        x = x_ref[...]
        ms = jnp.mean(x * x, axis=-1, keepdims=True)
        y = x * lax.rsqrt(ms + RMS_EPS) * nw_ref[...]
        h_ref[...] = (y * (1.0 + sc_ref[...]) + sh_ref[...]).astype(BF16)

    o_ref[...] = _dot(h_ref[...], w_ref[...])


def _input_projection(x2, norm_w, mod, w_proj, S):
    T = x2.shape[0]
    tiles_per_batch = S // INPROJ_TM
    return pl.pallas_call(
        _inproj_kernel,
        out_shape=jax.ShapeDtypeStruct((T, N_PROJ), F32),
        grid=(T // INPROJ_TM, N_PROJ // INPROJ_TN),
        in_specs=[
            pl.BlockSpec((INPROJ_TM, D_MODEL), lambda i, j: (i, 0)),
            pl.BlockSpec((1, D_MODEL), lambda i, j: (0, 0)),
            pl.BlockSpec((None, 1, D_MODEL), lambda i, j: (i // tiles_per_batch, 0, 1)),
            pl.BlockSpec((None, 1, D_MODEL), lambda i, j: (i // tiles_per_batch, 0, 0)),
            pl.BlockSpec((D_MODEL, INPROJ_TN), lambda i, j: (0, j)),
        ],
        out_specs=pl.BlockSpec((INPROJ_TM, INPROJ_TN), lambda i, j: (i, j)),
        scratch_shapes=[pltpu.VMEM((INPROJ_TM, D_MODEL), BF16)],
        compiler_params=_cparams("parallel", "arbitrary"),
        name="inproj",
    )(x2, norm_w.reshape(1, D_MODEL), mod, mod, w_proj)


def _rope_tables(S):
    half = ROPE_DIM // 2
    inv_freq = jnp.float32(ROPE_THETA) ** (-jnp.arange(half, dtype=jnp.float32) * 2.0 / ROPE_DIM)
    ang = jnp.arange(S, dtype=jnp.float32)[:, None] * inv_freq[None, :]
    cos, sin = jnp.cos(ang), jnp.sin(ang)
    ones = jnp.ones((S, MOBA_HEAD_DIM - ROPE_DIM), F32)
    zeros_hi = jnp.zeros((S, MOBA_HEAD_DIM - half), F32)
    zeros_lo = jnp.zeros((S, half), F32)
    cos_f = jnp.concatenate([cos, cos, ones], axis=1)
    sin_a = jnp.concatenate([-sin, zeros_hi], axis=1)
    sin_b = jnp.concatenate([zeros_lo, sin, ones * 0.0], axis=1)
    return cos_f, sin_a, sin_b


def _rope(x, cos_f, sin_a, sin_b):
    half = ROPE_DIM // 2
    up = pltpu.roll(x, MOBA_HEAD_DIM - half, 1)
    dn = pltpu.roll(x, half, 1)
    return x * cos_f + up * sin_a + dn * sin_b


def _kprep_kernel(k_ref, v_ref, cos_ref, sa_ref, sb_ref, kaug_ref, vb_ref, kmean_ref):
    n = pl.program_id(1)
    cos_f, sin_a, sin_b = cos_ref[...], sa_ref[...], sb_ref[...]
    lane = lax.broadcasted_iota(jnp.int32, (MOBA_BLOCK, LANES), 1)
    onehot = jnp.where(lane == n, 1.0, 0.0).astype(BF16)
    for h in range(MOBA_HEADS):
        cols = slice(h * MOBA_HEAD_DIM, (h + 1) * MOBA_HEAD_DIM)
        kr = _rope(k_ref[:, cols], cos_f, sin_a, sin_b)
        kmean_ref[:, cols] = jnp.mean(kr, axis=0, keepdims=True)
        kaug_ref[h, :, :MOBA_HEAD_DIM] = kr.astype(BF16)
        kaug_ref[h, :, MOBA_HEAD_DIM:] = onehot
        vb_ref[h, :MOBA_HEAD_DIM, :] = v_ref[:, cols].T.astype(BF16)
        vb_ref[h, MOBA_HEAD_DIM:, :] = jnp.ones((MOBA_SUM_ROWS, MOBA_BLOCK), BF16)


def _moba_prep(proj, tables, B, S):
    NB = S // MOBA_BLOCK
    NBP = NB
    src = lambda n: n
    tab_spec = pl.BlockSpec((MOBA_BLOCK, LANES), lambda b, n: (src(n), 0))
    return pl.pallas_call(
        _kprep_kernel,
        out_shape=(
            jax.ShapeDtypeStruct((B, MOBA_HEADS, NBP * MOBA_BLOCK, 2 * MOBA_HEAD_DIM), BF16),
            jax.ShapeDtypeStruct((B, MOBA_HEADS, MOBA_HEAD_DIM + MOBA_SUM_ROWS, NBP * MOBA_BLOCK), BF16),
            jax.ShapeDtypeStruct((B, NBP, 1, D_MODEL), F32),
        ),
        grid=(B, NBP),
        in_specs=[
            pl.BlockSpec((MOBA_BLOCK, D_MODEL), lambda b, n: (b * NB + src(n), 1)),
            pl.BlockSpec((MOBA_BLOCK, D_MODEL), lambda b, n: (b * NB + src(n), 2)),
            tab_spec, tab_spec, tab_spec,
        ],
        out_specs=(
            pl.BlockSpec((None, MOBA_HEADS, MOBA_BLOCK, 2 * MOBA_HEAD_DIM), lambda b, n: (b, 0, n, 0)),
            pl.BlockSpec((None, MOBA_HEADS, MOBA_HEAD_DIM + MOBA_SUM_ROWS, MOBA_BLOCK), lambda b, n: (b, 0, 0, n)),
            pl.BlockSpec((None, None, 1, D_MODEL), lambda b, n: (b, n, 0, 0)),
        ),
        compiler_params=_cparams("parallel", "parallel"),
        name="moba_kprep",
    )(proj, proj, *tables)


def _qprep_kernel(q_ref, cos_ref, sa_ref, sb_ref, kmean_ref, qaug_ref):
    i = pl.program_id(1)
    blk = MOBA_BLOCK
    scale = MOBA_HEAD_DIM ** -0.5
    cos_f, sin_a, sin_b = cos_ref[...], sa_ref[...], sb_ref[...]
    nbp = kmean_ref.shape[1]
    blk_id = lax.broadcasted_iota(jnp.int32, (nbp, blk), 0)
    blk_f = blk_id.astype(F32)
    neg_inf = jnp.float32(-jnp.inf)
    for h in range(MOBA_HEADS):
        cols = slice(h * MOBA_HEAD_DIM, (h + 1) * MOBA_HEAD_DIM)
        qr = _rope(q_ref[:, cols], cos_f, sin_a, sin_b)
        gate = _dot_nt(kmean_ref[h], qr, precision=HIGHEST)
        g = jnp.where(blk_id < i, gate, neg_inf)
        bias = jnp.where(blk_id == i, 0.0, MASK_NEG)
        for r in range(MOBA_TOPK):
            mx = jnp.max(g, axis=0, keepdims=True)
            first = jnp.min(jnp.where(g == mx, blk_f, float(nbp)), axis=0, keepdims=True)
            pick = blk_f == first
            unmask = jnp.where(i > r, 0.0, MASK_NEG)
            bias = jnp.where(pick, jnp.maximum(bias, unmask), bias)
            g = jnp.where(pick, neg_inf, g)
        qaug_ref[h, :MOBA_HEAD_DIM, :] = (qr * (scale * LOG2E)).T.astype(BF16)
        qaug_ref[h, MOBA_HEAD_DIM:MOBA_HEAD_DIM + nbp, :] = bias.astype(BF16)
        qaug_ref[h, MOBA_HEAD_DIM + nbp:, :] = jnp.zeros((MOBA_HEAD_DIM - nbp, blk), BF16)


def _moba_qprep(proj, tables, kmean, B, S):
    NB = S // MOBA_BLOCK
    tab_spec = pl.BlockSpec((MOBA_BLOCK, LANES), lambda b, i: (i, 0))
    return pl.pallas_call(
        _qprep_kernel,
        out_shape=jax.ShapeDtypeStruct((B, MOBA_HEADS, 2 * MOBA_HEAD_DIM, S), BF16),
        grid=(B, NB),
        in_specs=[
            pl.BlockSpec((MOBA_BLOCK, D_MODEL), lambda b, i: (b * NB + i, 0)),
            tab_spec, tab_spec, tab_spec,
            pl.BlockSpec((None, MOBA_HEADS, kmean.shape[2], MOBA_HEAD_DIM), lambda b, i: (b, 0, 0, 0)),
        ],
        out_specs=pl.BlockSpec((None, MOBA_HEADS, 2 * MOBA_HEAD_DIM, MOBA_BLOCK), lambda b, i: (b, 0, 0, i)),
        compiler_params=_cparams("parallel", "parallel"),
        name="moba_qprep",
    )(proj, *tables, kmean)


def _moba_kernel(q_ref, kaug_ref, v_ref, o_ref, m_ref, acc_ref, s_ref, mx_ref):
    n_full = pl.program_id(2)
    width = MOBA_GROUP * MOBA_BLOCK
    qt = q_ref[...]

    def scores(g):
        return _dot(kaug_ref[pl.ds(pl.multiple_of(g * width, width), width), :], qt)

    def prob_value(s, m, g):
        p = jnp.exp2(s - m).astype(BF16)
        return _dot(v_ref[:, pl.ds(pl.multiple_of(g * width, width), width)], p)

    def stage(g):
        s = scores(g)
        s_ref[...] = s
        mx_ref[...] = jnp.max(s, axis=0, keepdims=True)

    s_own = scores(n_full)
    stage(0)
    key = lax.broadcasted_iota(jnp.int32, (width, width), 0)
    qry = lax.broadcasted_iota(jnp.int32, (width, width), 1)
    s_own = jnp.where(key > qry, jnp.float32(-jnp.inf), s_own)
    m0 = jnp.max(s_own, axis=0, keepdims=True)
    m_ref[...] = m0
    acc_ref[...] = prob_value(s_own, m0, n_full)

    def body(g, carry):
        m_prev = m_ref[...]
        m_new = jnp.maximum(m_prev, mx_ref[...])
        alpha = jnp.exp2(m_prev - m_new)
        m_ref[...] = m_new
        pv = prob_value(s_ref[...], m_new, g)
        stage(jnp.minimum(g + 1, jnp.maximum(n_full - 1, 0)))
        acc_ref[...] = alpha * acc_ref[...] + pv
        return carry

    lax.fori_loop(0, n_full, body, 0)
    acc = acc_ref[...]
    out_t = acc[:MOBA_HEAD_DIM, :] / acc[MOBA_HEAD_DIM:MOBA_HEAD_DIM + 1, :]
    o_ref[...] = out_t.T.astype(o_ref.dtype)


def _moba_attention(qaug, kaug, vb, B, S):
    T = B * S
    tq = MOBA_GROUP * MOBA_BLOCK
    assert S % tq == 0
    NG = S // tq
    return pl.pallas_call(
        _moba_kernel,
        out_shape=jax.ShapeDtypeStruct((T, D_MODEL), BF16),
        grid=(B, MOBA_HEADS, NG),
        in_specs=[
            pl.BlockSpec((None, None, 2 * MOBA_HEAD_DIM, tq), lambda b, h, j: (b, h, 0, j)),
            pl.BlockSpec((None, None, S, 2 * MOBA_HEAD_DIM), lambda b, h, j: (b, h, 0, 0)),
            pl.BlockSpec((None, None, MOBA_HEAD_DIM + MOBA_SUM_ROWS, S), lambda b, h, j: (b, h, 0, 0)),
        ],
        out_specs=pl.BlockSpec((tq, MOBA_HEAD_DIM), lambda b, h, j: (b * NG + j, h)),
        scratch_shapes=[
            pltpu.VMEM((1, tq), F32),
            pltpu.VMEM((MOBA_HEAD_DIM + MOBA_SUM_ROWS, tq), F32),
            pltpu.VMEM((tq, tq), F32),
            pltpu.VMEM((1, tq), F32),
        ],
        compiler_params=_cparams("parallel", "parallel", "arbitrary"),
        name="moba_attn",
    )(qaug, kaug, vb)


def _gla_decay_kernel(gl_ref, gw_ref, gb_ref, b_ref, peak_ref):
    tm, C = GLA_DECAY_TM, GLA_CHUNK
    z = _dot(gl_ref[...].astype(BF16), gw_ref[...].astype(BF16)) + gb_ref[...]
    log_g = jax.nn.log_sigmoid(z) / GLA_GATE_NORMALIZER
    tt = GLA_DECAY_TRI
    r_i = lax.broadcasted_iota(jnp.int32, (tt, tt), 0)
    c_i = lax.broadcasted_iota(jnp.int32, (tt, tt), 1)
    tri = jnp.where((c_i <= r_i) & (c_i >= (r_i // C) * C), 1.0, 0.0).astype(BF16)
    hi = _truncate_to_bf16_bits(log_g, pltpu.bitcast)
    rest = log_g - hi
    mid = _truncate_to_bf16_bits(rest, pltpu.bitcast)
    lo = rest - mid
    for r0 in range(0, tm, tt):
        rows = slice(r0, r0 + tt)
        b_ref[rows, :] = ((_dot(tri, hi[rows].astype(BF16)) + _dot(tri, mid[rows].astype(BF16)))
                          + _dot(tri, lo[rows].astype(BF16)))
    steps = -log_g
    for c in range(tm // C):
        pk = jnp.max(jnp.max(steps[c * C:(c + 1) * C, :], axis=1, keepdims=True), axis=0, keepdims=True)
        peak_ref[c:c + 1, :] = jnp.broadcast_to(pk, (1, LANES))


def _gla_decay(proj, gate_w, gate_b):
    T = proj.shape[0]
    tm = GLA_DECAY_TM
    gw = jnp.zeros((LANES, GLA_KEY_DIM), F32).at[:GLA_GATE_RANK].set(gate_w)
    return pl.pallas_call(
        _gla_decay_kernel,
        out_shape=(
            jax.ShapeDtypeStruct((T, GLA_KEY_DIM), F32),
            jax.ShapeDtypeStruct((T // GLA_CHUNK, LANES), F32),
        ),
        grid=(T // tm,),
        in_specs=[
            pl.BlockSpec((tm, LANES), lambda i: (i, COL_GLOW // LANES)),
            pl.BlockSpec((LANES, GLA_KEY_DIM), lambda i: (0, 0)),
            pl.BlockSpec((1, GLA_KEY_DIM), lambda i: (0, 0)),
        ],
        out_specs=(
            pl.BlockSpec((tm, GLA_KEY_DIM), lambda i: (i, 0)),
            pl.BlockSpec((tm // GLA_CHUNK, LANES), lambda i: (i, 0)),
        ),
        compiler_params=_cparams("parallel"),
        name="gla_decay",
    )(proj, gw, gate_b.reshape(1, GLA_KEY_DIM))


def _gla_kernel(flag_ref, q_ref, k_ref, v_ref, gr_ref, b_ref, nw_ref, o_ref, state_ref):
    C, SUB = GLA_CHUNK, GLA_SUB
    nsub = C // SUB

    @pl.when(pl.program_id(1) == 0)
    def _():
        state_ref[...] = jnp.zeros_like(state_ref)

    r_i = lax.broadcasted_iota(jnp.int32, (C, C), 0)
    c_i = lax.broadcasted_iota(jnp.int32, (C, C), 1)

    row_c = lax.broadcasted_iota(jnp.int32, (C, GLA_DK), 0)
    row_s = lax.broadcasted_iota(jnp.int32, (SUB, GLA_DK), 0)
    lane_s = lax.broadcasted_iota(jnp.int32, (SUB, C), 1)
    neg_inf = jnp.float32(-jnp.inf)
    nw = nw_ref[...]

    def scores_bounded(q, k, b):
        s_rows = []
        for I in range(nsub):
            rs = slice(I * SUB, (I + 1) * SUB)
            ref_b = b[I * SUB:I * SUB + 1, :]
            qs = q[rs] * jnp.exp(b[rs] - ref_b)
            ks = k * jnp.exp(jnp.where(row_c < (I + 1) * SUB, ref_b - b, neg_inf))
            s_rows.append(_dot_nt(qs.astype(BF16), ks.astype(BF16)))
        return jnp.where(c_i <= r_i, jnp.concatenate(s_rows, axis=0), 0.0)

    def scores_any(q, k, b):
        s_rows = []
        for I in range(nsub):
            rs = slice(I * SUB, (I + 1) * SUB)
            q_i, k_i, b_i = q[rs], k[rs], b[rs]
            if I == 0:
                s_i = jnp.zeros((SUB, C), F32)
            else:
                ref_b = b[I * SUB - 1:I * SUB, :]
                qs = q_i * jnp.exp(b_i - ref_b)
                ks = k * jnp.exp(jnp.where(row_c < I * SUB, ref_b - b, neg_inf))
                s_i = _dot_nt(qs.astype(BF16), ks.astype(BF16))
            for j in range(SUB):
                diff = jnp.where(row_s >= j, b_i - b_i[j:j + 1, :], neg_inf)
                colv = jnp.sum(q_i * k_i[j:j + 1, :] * jnp.exp(diff), axis=1, keepdims=True)
                s_i = jnp.where(lane_s == I * SUB + j, colv, s_i)
            s_rows.append(s_i)
        return jnp.concatenate(s_rows, axis=0)

    def all_heads(scores):
        for h in range(GLA_HEADS):
            kc = slice(h * GLA_DK, (h + 1) * GLA_DK)
            vc = slice(h * GLA_DV, (h + 1) * GLA_DV)
            st = state_ref[h]
            for cc in range(GLA_STEP_CHUNKS):
                rows = slice(cc * C, (cc + 1) * C)
                b = b_ref[rows, kc]
                q = q_ref[rows, kc] * (GLA_DK ** -0.5)
                k = k_ref[rows, kc]
                v = v_ref[rows, vc]
                b_last = b[C - 1:C, :]

                o = _dot_nt((q * jnp.exp(b)).astype(BF16), st.astype(BF16))
                o = o + _dot(scores(q, k, b).astype(BF16), v.astype(BF16))

                kd = k * jnp.exp(b_last - b)
                st = st * jnp.exp(b_last) + _dot(v.T.astype(BF16), kd.astype(BF16))

                ms = jnp.mean(o * o, axis=-1, keepdims=True)
                y = o * lax.rsqrt(ms + RMS_EPS) * nw
                g = gr_ref[rows, vc]
                o_ref[rows, vc] = (y * (g * jax.nn.sigmoid(g))).astype(o_ref.dtype)
            state_ref[h] = st

    bounded = flag_ref[pl.program_id(0) * pl.num_programs(1) + pl.program_id(1)] != 0
    lax.cond(bounded, lambda: all_heads(scores_bounded), lambda: all_heads(scores_any))


def _gla(proj, gate_w, gate_b, norm_w, B, S):
    T = B * S
    step_rows = GLA_STEP_CHUNKS * GLA_CHUNK
    NC = S // step_rows
    b_all, peak = _gla_decay(proj, gate_w, gate_b)
    step_peak = jnp.max(peak[:, 0].reshape(-1, GLA_STEP_CHUNKS), axis=1)
    bounded = ((GLA_SUB - 1) * step_peak < GLA_SAFE_EXPONENT).astype(jnp.int32)
    q_blk = (3 * D_MODEL) // GLA_KEY_DIM
    v_blk = (3 * D_MODEL + 2 * GLA_KEY_DIM) // GLA_VAL_DIM
    rows = lambda width, col: pl.BlockSpec((step_rows, width), lambda b, c, flags: (b * NC + c, col))
    return pl.pallas_call(
        _gla_kernel,
        out_shape=jax.ShapeDtypeStruct((T, GLA_VAL_DIM), BF16),
        grid_spec=pltpu.PrefetchScalarGridSpec(
            num_scalar_prefetch=1,
            grid=(B, NC),
            in_specs=[
                rows(GLA_KEY_DIM, q_blk), rows(GLA_KEY_DIM, q_blk + 1),
                rows(GLA_VAL_DIM, v_blk), rows(GLA_VAL_DIM, v_blk + 1),
                rows(GLA_KEY_DIM, 0),
                pl.BlockSpec((1, GLA_DV), lambda b, c, flags: (0, 0)),
            ],
            out_specs=rows(GLA_VAL_DIM, 0),
            scratch_shapes=[pltpu.VMEM((GLA_HEADS, GLA_DV, GLA_DK), F32)],
        ),
        compiler_params=_cparams("parallel", "arbitrary"),
        name="gla",
    )(bounded, proj, proj, proj, proj, b_all, norm_w.reshape(1, GLA_DV))


def _merge_kernel(oa_ref, ob_ref, ga_ref, gb_ref, x_ref, wa_ref, wb_ref, wo_ref, bma_ref, bmb_ref,
                  g1_ref, sc2_ref, sh2_ref, nw_ref, wrh_ref, wrl_ref, br_ref,
                  x1_ref, h2_ref, e4_ref, w4_ref, p4_ref, cnt_ref, run_ref):
    tm = MERGE_TM

    @pl.when(pl.program_id(0) == 0)
    def _():
        run_ref[...] = jnp.zeros_like(run_ref)

    y_a = _dot(oa_ref[...], wa_ref[...])
    y_b = _dot(ob_ref[...], wb_ref[...])
    g_a = jax.nn.sigmoid(ga_ref[...] + bma_ref[...])
    g_b = jax.nn.sigmoid(gb_ref[...] + bmb_ref[...])
    mix = _dot((g_a * y_a + g_b * y_b).astype(BF16), wo_ref[...])
    x1 = x_ref[...] + g1_ref[...] * mix
    x1_ref[...] = x1

    ms = jnp.mean(x1 * x1, axis=-1, keepdims=True)
    h2 = x1 * lax.rsqrt(ms + RMS_EPS) * nw_ref[...] * (1.0 + sc2_ref[...]) + sh2_ref[...]
    for s in range(D_MODEL // LANES):
        h2_ref[pl.ds(s, tm, stride=SUBLANES), :] = h2[:, s * LANES:(s + 1) * LANES]

    h_top = _truncate_to_bf16_bits(h2, pltpu.bitcast)
    h_hi = h_top.astype(BF16)
    h_lo = (h2 - h_top).astype(BF16)
    logits = (_dot(h_hi, wrh_ref[...]) + (_dot(h_lo, wrh_ref[...]) + _dot(h_hi, wrl_ref[...]))) + br_ref[...]
    lane = lax.broadcasted_iota(jnp.int32, (tm, LANES), 1)
    lane_f = lane.astype(F32)
    neg_inf = jnp.float32(-jnp.inf)
    g = jnp.where(lane < N_EXPERTS, logits, neg_inf)
    picks, tops = [], []
    for _ in range(TOP_K):
        mx = jnp.max(g, axis=1, keepdims=True)
        idx = _first_max_lane(g, lane_f)
        pick = lane_f == idx
        picks.append((pick, idx))
        tops.append(mx)
        g = jnp.where(pick, neg_inf, g)
    ex = [jnp.exp(t - tops[0]) for t in tops]
    denom = ex[0] + ex[1] + ex[2] + ex[3]

    sel = sum(jnp.where(pick, 1.0, 0.0) for pick, _ in picks)
    r_i = lax.broadcasted_iota(jnp.int32, (tm, tm), 0)
    c_i = lax.broadcasted_iota(jnp.int32, (tm, tm), 1)
    lower = jnp.where(c_i < r_i, 1.0, 0.0).astype(BF16)
    rank = run_ref[0:1, :] + _dot(lower, sel.astype(BF16))
    run_new = run_ref[0:1, :] + jnp.sum(sel, axis=0, keepdims=True)
    run_ref[...] = jnp.broadcast_to(run_new, run_ref.shape)
    cnt_ref[...] = jnp.broadcast_to(run_new, cnt_ref.shape)

    e4 = jnp.zeros((tm, LANES), jnp.int32)
    w4 = jnp.zeros((tm, LANES), F32)
    p4 = jnp.zeros((tm, LANES), F32)
    for r in range(TOP_K):
        pick, idx = picks[r]
        pos = jnp.sum(jnp.where(pick, rank, 0.0), axis=1, keepdims=True)
        e4 = jnp.where(lane == r, idx.astype(jnp.int32), e4)
        w4 = jnp.where(lane == r, ex[r] / denom, w4)
        p4 = jnp.where(lane == r, pos, p4)
    e4_ref[...] = e4
    w4_ref[...] = w4
    p4_ref[...] = p4


def _merge_and_route(o_a, o_b, proj, x2, w_a, w_b, w_o, b_merge, mod, norm2_w, w_router, b_router, S):
    T = x2.shape[0]
    tm = MERGE_TM
    tiles_per_batch = S // tm
    full = lambda shape: pl.BlockSpec(shape, lambda i: tuple(0 for _ in shape))
    row = lambda width, col: pl.BlockSpec((tm, width), lambda i: (i, col))
    modv = lambda k: pl.BlockSpec((None, 1, D_MODEL), lambda i: (i // tiles_per_batch, 0, k))
    wr = jnp.zeros((D_MODEL, LANES), F32).at[:, :N_EXPERTS].set(w_router)
    wr_top = _truncate_to_bf16_bits(wr, lax.bitcast_convert_type)
    wr_hi = wr_top.astype(BF16)
    wr_lo = (wr - wr_top).astype(BF16)
    br = jnp.zeros((1, LANES), F32).at[0, :N_EXPERTS].set(b_router)
    bm = b_merge.reshape(1, 2 * D_MODEL)
    return pl.pallas_call(
        _merge_kernel,
        out_shape=(
            jax.ShapeDtypeStruct((T, D_MODEL), F32),
            jax.ShapeDtypeStruct((T * SUBLANES, LANES), F32),
            jax.ShapeDtypeStruct((T, LANES), jnp.int32),
            jax.ShapeDtypeStruct((T, LANES), F32),
            jax.ShapeDtypeStruct((T, LANES), F32),
            jax.ShapeDtypeStruct((SUBLANES, LANES), F32),
        ),
        grid=(T // tm,),
        in_specs=[
            row(D_MODEL, 0), row(D_MODEL, 0),
            row(D_MODEL, COL_GATES // D_MODEL), row(D_MODEL, COL_GATES // D_MODEL + 1),
            row(D_MODEL, 0),
            full((D_MODEL, D_MODEL)), full((D_MODEL, D_MODEL)), full((D_MODEL, D_MODEL)),
            pl.BlockSpec((1, D_MODEL), lambda i: (0, 0)), pl.BlockSpec((1, D_MODEL), lambda i: (0, 1)),
            modv(2), modv(4), modv(3),
            full((1, D_MODEL)), full((D_MODEL, LANES)), full((D_MODEL, LANES)), full((1, LANES)),
        ],
        out_specs=(
            row(D_MODEL, 0),
            pl.BlockSpec((tm * SUBLANES, LANES), lambda i: (i, 0)),
            row(LANES, 0), row(LANES, 0), row(LANES, 0),
            full((SUBLANES, LANES)),
        ),
        scratch_shapes=[pltpu.VMEM((SUBLANES, LANES), F32)],
        compiler_params=_cparams("arbitrary"),
        name="merge_route",
    )(o_a, o_b, proj, proj, x2, w_a, w_b, w_o, bm, bm, mod, mod, mod,
      norm2_w.reshape(1, D_MODEL), wr_hi, wr_lo, br)


def _row_slab(ref, row):
    return ref.at[pl.ds(pl.multiple_of(row * SUBLANES, SUBLANES), SUBLANES), :]


def _wait_slabs(src_ref, dst_ref, sem, count):
    def wait(a, c):
        pltpu.make_async_copy(_row_slab(src_ref, 0), _row_slab(dst_ref, 0), sem).wait()
        return c

    lax.fori_loop(0, count, wait, 0, unroll=DMA_WAIT_UNROLL)


def _dispatch_kernel(dest_ref, pad_lo_ref, pad_hi_ref, nu_ref, h_ref, xs_ref, zero_ref, sem, pad_sem, blk_sem):
    tm = COMBINE_TM
    base = pl.program_id(0) * (tm * TOP_K)
    blk_rows = EXPERT_ROWS * SUBLANES
    n_blocks = xs_ref.shape[0] // blk_rows

    def block_copy(j):
        dst = xs_ref.at[pl.ds(pl.multiple_of(j * blk_rows, blk_rows), blk_rows), :]
        return pltpu.make_async_copy(zero_ref, dst, blk_sem)

    def pad_copies(act):
        for e in range(N_EXPERTS):
            lo = pad_lo_ref[e]
            n = pad_hi_ref[e] - lo
            piece = EXPERT_ROWS // 2
            while piece >= 1:
                @pl.when((n & piece) != 0)
                def _(piece=piece, lo=lo, n=n):
                    first = lo + (n & ~(2 * piece - 1))
                    dst = xs_ref.at[pl.ds(pl.multiple_of(first * SUBLANES, SUBLANES), piece * SUBLANES), :]
                    act(pltpu.make_async_copy(zero_ref.at[pl.ds(0, piece * SUBLANES), :], dst, pad_sem))
                piece //= 2

    @pl.when(pl.program_id(0) == 0)
    def _():
        zero_ref[...] = jnp.zeros_like(zero_ref)

        def fill_block(j, c):
            block_copy(j).start()
            return c

        pad_copies(lambda cp: cp.start())
        lax.fori_loop(nu_ref[0], n_blocks, fill_block, 0)

    @pl.when(pl.program_id(0) == pl.num_programs(0) - 1)
    def _():
        def drain_block(j, c):
            block_copy(j).wait()
            return c

        pad_copies(lambda cp: cp.wait())
        lax.fori_loop(nu_ref[0], n_blocks, drain_block, 0)

    def start(t, c):
        src = _row_slab(h_ref, t)
        for kk in range(TOP_K):
            pltpu.make_async_copy(src, _row_slab(xs_ref, dest_ref[base + t * TOP_K + kk]), sem).start(
                priority=kk % DMA_PRIORITIES)
        return c

    lax.fori_loop(0, tm, start, 0, unroll=DMA_ISSUE_UNROLL)
    _wait_slabs(h_ref, xs_ref, sem, tm * TOP_K)


def _dispatch(dest, pad_lo, pad_hi, n_used, h2_slabs, P):
    T = h2_slabs.shape[0] // SUBLANES
    tm = COMBINE_TM
    return pl.pallas_call(
        _dispatch_kernel,
        out_shape=jax.ShapeDtypeStruct((P * SUBLANES, LANES), F32),
        grid_spec=pltpu.PrefetchScalarGridSpec(
            num_scalar_prefetch=4,
            grid=(T // tm,),
            in_specs=[pl.BlockSpec((tm * SUBLANES, LANES), lambda i, *_: (i, 0))],
            out_specs=pl.BlockSpec(memory_space=pl.ANY),
            scratch_shapes=[
                pltpu.VMEM((EXPERT_ROWS * SUBLANES, LANES), F32),
                pltpu.SemaphoreType.DMA,
                pltpu.SemaphoreType.DMA,
                pltpu.SemaphoreType.DMA,
            ],
        ),
        compiler_params=_cparams("arbitrary"),
        name="moe_dispatch",
    )(dest, pad_lo, pad_hi, n_used, h2_slabs)


def _expert_kernel(be_ref, nu_ref, slot_ref, nxt_ref, xs_ref, wi_hbm, bi_ref, wo_hbm, bo_ref, perm_ref, ys_ref,
                   x_ref, wi16_ref, wo16_ref, wi32_ref, wo32_ref, wi_sem, wo_sem):
    R = EXPERT_ROWS
    j = pl.program_id(0)
    used = j < nu_ref[0]
    new_expert = (j == 0) | (be_ref[j] != be_ref[jnp.maximum(j - 1, 0)])

    def weight_copies(e, s):
        return (pltpu.make_async_copy(wi_hbm.at[e], wi32_ref.at[s], wi_sem.at[s]),
                pltpu.make_async_copy(wo_hbm.at[e], wo32_ref.at[s], wo_sem.at[s]))

    @pl.when(used & (j == 0))
    def _():
        for cp in weight_copies(be_ref[0], slot_ref[0]):
            cp.start()

    @pl.when(used & new_expert)
    def _():
        s = slot_ref[j]
        for cp in weight_copies(be_ref[j], s):
            cp.wait()

        @pl.when(nxt_ref[j] >= 0)
        def _():
            for cp in weight_copies(nxt_ref[j], 1 - s):
                cp.start()

        perm = perm_ref[...]
        for g in range(D_MODEL // LANES):
            w = wi32_ref[s, :, 2 * LANES * g:2 * LANES * (g + 1)].astype(BF16)
            sep = _dot(w, perm).astype(BF16)
            wi16_ref[:, LANES * g:LANES * (g + 1)] = sep[:, :LANES]
            wi16_ref[:, D_MODEL + LANES * g:D_MODEL + LANES * (g + 1)] = sep[:, LANES:]
        wo16_ref[...] = wo32_ref[s].astype(BF16)

    @pl.when(used)
    def _():
        for s in range(D_MODEL // LANES):
            x_ref[:, s * LANES:(s + 1) * LANES] = xs_ref[pl.ds(s, R, stride=SUBLANES), :].astype(BF16)
        hid = _dot(x_ref[...], wi16_ref[...]) + bi_ref[...]
        glu = jnp.minimum(hid[:, :D_MODEL], SWIGLU_LIMIT)
        lin = jnp.clip(hid[:, D_MODEL:], -SWIGLU_LIMIT, SWIGLU_LIMIT)
        act = glu * jax.nn.sigmoid(SWIGLU_ALPHA * glu) * (lin + 1.0)
        out = _dot(act.astype(BF16), wo16_ref[...]) + bo_ref[...]
        for s in range(D_MODEL // LANES):
            ys_ref[pl.ds(s, R, stride=SUBLANES), :] = out[:, s * LANES:(s + 1) * LANES]

    @pl.when(j >= nu_ref[0])
    def _():
        ys_ref[...] = jnp.zeros_like(ys_ref)


def _experts(block_expert, n_used, group_slot, next_expert, xs, w_in, b_in, w_out, b_out):
    R = EXPERT_ROWS
    n_blocks = xs.shape[0] // (R * SUBLANES)
    src = jnp.arange(2 * LANES)[:, None]
    dst = jnp.arange(2 * LANES)[None, :]
    perm = jnp.where(src == jnp.where(dst < LANES, 2 * dst, 2 * (dst - LANES) + 1), 1.0, 0.0).astype(BF16)
    return pl.pallas_call(
        _expert_kernel,
        out_shape=jax.ShapeDtypeStruct(xs.shape, F32),
        grid_spec=pltpu.PrefetchScalarGridSpec(
            num_scalar_prefetch=4,
            grid=(n_blocks,),
            in_specs=[
                pl.BlockSpec((R * SUBLANES, LANES),
                             lambda j, be, nu, *_: (jnp.minimum(j, jnp.maximum(nu[0] - 1, 0)), 0)),
                pl.BlockSpec(memory_space=pl.ANY),
                pl.BlockSpec((None, 1, 2 * D_MODEL), lambda j, be, *_: (be[j], 0, 0)),
                pl.BlockSpec(memory_space=pl.ANY),
                pl.BlockSpec((None, 1, D_MODEL), lambda j, be, *_: (be[j], 0, 0)),
                pl.BlockSpec((2 * LANES, 2 * LANES), lambda j, *_: (0, 0)),
            ],
            out_specs=pl.BlockSpec((R * SUBLANES, LANES), lambda j, *_: (j, 0)),
            scratch_shapes=[
                pltpu.VMEM((R, D_MODEL), BF16),
                pltpu.VMEM((D_MODEL, 2 * D_MODEL), BF16),
                pltpu.VMEM((D_MODEL, D_MODEL), BF16),
                pltpu.VMEM((2, D_MODEL, 2 * D_MODEL), F32),
                pltpu.VMEM((2, D_MODEL, D_MODEL), F32),
                pltpu.SemaphoreType.DMA((2,)),
                pltpu.SemaphoreType.DMA((2,)),
            ],
        ),
        compiler_params=pltpu.CompilerParams(
            dimension_semantics=("arbitrary",), vmem_limit_bytes=EXPERT_VMEM_LIMIT),
        name="moe_experts",
    )(block_expert, n_used, group_slot, next_expert, xs, w_in, b_in, w_out, b_out, perm)


def _combine_kernel(dest_ref, ys_ref, w4_ref, x1_ref, g2_ref, nw_ref, o_ref, buf_ref, sem):
    tm = COMBINE_TM
    step = pl.program_id(0)
    slot = step % 2

    def gather(tile, into):
        base = tile * (tm * TOP_K)

        def start(t, c):
            for kk in range(TOP_K):
                pltpu.make_async_copy(_row_slab(ys_ref, dest_ref[base + t * TOP_K + kk]),
                                      _row_slab(buf_ref.at[into, kk], t),
                                      sem.at[into]).start(priority=kk % DMA_PRIORITIES)
            return c

        lax.fori_loop(0, tm, start, 0, unroll=DMA_ISSUE_UNROLL)

    @pl.when(step == 0)
    def _():
        gather(0, 0)

    @pl.when(step + 1 < pl.num_programs(0))
    def _():
        gather(step + 1, 1 - slot)

    _wait_slabs(ys_ref, buf_ref.at[slot, 0], sem.at[slot], tm * TOP_K)

    g2 = g2_ref[...]
    nw = nw_ref[...]
    rc = COMBINE_ROW_CHUNK
    for r0 in range(0, tm, rc):
        wk = [jnp.broadcast_to(w4_ref[r0:r0 + rc, kk:kk + 1], (rc, LANES)) for kk in range(TOP_K)]
        parts = []
        ssq = jnp.zeros((rc, 1), F32)
        for s in range(D_MODEL // LANES):
            cols = slice(s * LANES, (s + 1) * LANES)
            y = jnp.zeros((rc, LANES), F32)
            for kk in range(TOP_K):
                y = y + buf_ref[slot, kk, pl.ds(r0 * SUBLANES + s, rc, stride=SUBLANES), :] * wk[kk]
            x2 = x1_ref[r0:r0 + rc, cols] + g2[:, cols] * y
            ssq = ssq + jnp.sum(x2 * x2, axis=1, keepdims=True)
            parts.append(x2)
        inv = lax.rsqrt(ssq / D_MODEL + RMS_EPS)
        for s in range(D_MODEL // LANES):
            cols = slice(s * LANES, (s + 1) * LANES)
            o_ref[r0:r0 + rc, cols] = parts[s] * inv * nw[:, cols]


def _combine(dest, ys, w4, x1, mod, final_w, S):
    T = x1.shape[0]
    tm = COMBINE_TM
    tiles_per_batch = S // tm
    return pl.pallas_call(
        _combine_kernel,
        out_shape=jax.ShapeDtypeStruct((T, D_MODEL), F32),
        grid_spec=pltpu.PrefetchScalarGridSpec(
            num_scalar_prefetch=1,
            grid=(T // tm,),
            in_specs=[
                pl.BlockSpec(memory_space=pl.ANY),
                pl.BlockSpec((tm, LANES), lambda i, dest: (i, 0)),
                pl.BlockSpec((tm, D_MODEL), lambda i, dest: (i, 0)),
                pl.BlockSpec((None, 1, D_MODEL), lambda i, dest: (i // tiles_per_batch, 0, 5)),
                pl.BlockSpec((1, D_MODEL), lambda i, dest: (0, 0)),
            ],
            out_specs=pl.BlockSpec((tm, D_MODEL), lambda i, dest: (i, 0)),
            scratch_shapes=[
                pltpu.VMEM((2, TOP_K, tm * SUBLANES, LANES), F32),
                pltpu.SemaphoreType.DMA((2,)),
            ],
        ),
        compiler_params=_cparams("arbitrary"),
        name="moe_combine",
    )(dest, ys, w4, x1, mod, final_w.reshape(1, D_MODEL))


def _layer(x2, c_mod, B, S, norm1_w, w_in, gla_gate_w, gla_gate_b, gla_norm_w, w_o_moba, w_o_gla,
           b_merge, w_out, norm2_w, w_router, b_router, w_exp_in, b_exp_in, w_exp_out, b_exp_out):
    T = B * S
    glow_lo = N_MAIN
    glow_hi = N_MAIN + GLA_GATE_RANK
    w_in16 = w_in.astype(BF16)
    w_proj = jnp.concatenate(
        [w_in16[:, :glow_lo], w_in16[:, glow_hi:], w_in16[:, glow_lo:glow_hi],
         jnp.zeros((D_MODEL, LANES - GLA_GATE_RANK), BF16)], axis=1)
    proj = _input_projection(x2, norm1_w, c_mod, w_proj, S)

    tables = _rope_tables(S)
    kaug, vb, kmean = _moba_prep(proj, tables, B, S)
    NB = S // MOBA_BLOCK
    kmean = kmean[:, :NB].reshape(B, NB, MOBA_HEADS, MOBA_HEAD_DIM).transpose(0, 2, 1, 3)
    kmean = jnp.pad(kmean, ((0, 0), (0, 0), (0, -NB % BF16_SUBLANES), (0, 0)))
    qaug = _moba_qprep(proj, tables, kmean, B, S)
    o_a = _moba_attention(qaug, kaug, vb, B, S)
    o_b = _gla(proj, gla_gate_w, gla_gate_b, gla_norm_w, B, S)

    x1, h2, e4, w4, p4, cnt = _merge_and_route(
        o_a, o_b, proj, x2, w_o_moba.astype(BF16), w_o_gla.astype(BF16), w_out.astype(BF16),
        b_merge, c_mod, norm2_w, w_router, b_router, S)

    R = EXPERT_ROWS
    counts = cnt[0, :N_EXPERTS].astype(jnp.int32)
    padded = (counts + R - 1) // R * R
    padded_end = jnp.cumsum(padded)
    padded_start = padded_end - padded
    n_blocks = -(-(T * TOP_K + N_EXPERTS * (R - 1)) // R)
    P = n_blocks * R
    experts = jnp.arange(N_EXPERTS, dtype=jnp.int32)

    def lookup(table, idx):
        return jnp.sum(jnp.where(idx[..., None] == experts, table, 0), axis=-1)

    dest = (lookup(padded_start, e4[:, :TOP_K]) + p4[:, :TOP_K].astype(jnp.int32)).reshape(-1)
    block_row0 = jnp.arange(n_blocks, dtype=jnp.int32) * R
    block_expert = jnp.minimum(
        jnp.sum(padded_end[None, :] <= block_row0[:, None], axis=1), N_EXPERTS - 1).astype(jnp.int32)
    n_used = (padded_end[-1:] // R).astype(jnp.int32)
    nonempty = counts > 0
    group_of_expert = jnp.cumsum(nonempty.astype(jnp.int32)) - 1
    later = jnp.where((experts[None, :] > experts[:, None]) & nonempty[None, :], experts[None, :], N_EXPERTS)
    next_of_expert = jnp.min(later, axis=1)
    next_of_expert = jnp.where(next_of_expert < N_EXPERTS, next_of_expert, -1)
    group_slot = (lookup(group_of_expert, block_expert) % 2).astype(jnp.int32)
    next_expert = lookup(next_of_expert, block_expert).astype(jnp.int32)

    xs = _dispatch(dest, padded_start + counts, padded_end, n_used, h2, P)
    b_ei = jnp.concatenate([b_exp_in[:, 0::2], b_exp_in[:, 1::2]], axis=-1)[:, None, :]
    ys = _experts(block_expert, n_used, group_slot, next_expert, xs, w_exp_in, b_ei, w_exp_out,
                  b_exp_out[:, None, :])
    return x1, ys, dest, w4


def kernel(x, c, w_ada, b_ada, norm1_w, w_in, gla_gate_w, gla_gate_b, gla_norm_w, w_o_moba, w_o_gla, b_merge, w_out, norm2_w, w_router, b_router, w_exp_in, b_exp_in, w_exp_out, b_exp_out, final_norm_w):
    B, S, _ = x.shape
    depth = w_ada.shape[0]
    assert depth == 1, "the combine kernel fuses the final norm, so a single layer is supported"
    x2 = x.reshape(B * S, D_MODEL)
    l = 0
    mod = _modulation(c, w_ada[l], b_ada[l])
    x1, ys, dest, w4 = _layer(
        x2, mod, B, S, norm1_w[l], w_in[l], gla_gate_w[l], gla_gate_b[l], gla_norm_w[l],
        w_o_moba[l], w_o_gla[l], b_merge[l], w_out[l], norm2_w[l], w_router[l], b_router[l],
        w_exp_in[l], b_exp_in[l], w_exp_out[l], b_exp_out[l])
    out = _combine(dest, ys, w4, x1, mod, final_norm_w, S)
    return out.reshape(B, S, D_MODEL)
```

```python
import functools

import jax
import jax.numpy as jnp
from jax import lax
from jax.experimental import pallas as pl
from jax.experimental.pallas import tpu as pltpu

F32 = jnp.float32
BF16 = jnp.bfloat16
HIGHEST = lax.Precision.HIGHEST

D_MODEL = 1024
MOBA_HEADS = 8
MOBA_HEAD_DIM = 128
MOBA_BLOCK = 256
MOBA_TOPK = 3
ROPE_DIM = MOBA_HEAD_DIM // 4
ROPE_THETA = 500000.0
GLA_HEADS = 4
GLA_KEY_DIM = D_MODEL // 2
GLA_VAL_DIM = D_MODEL
GLA_DK = GLA_KEY_DIM // GLA_HEADS
GLA_DV = GLA_VAL_DIM // GLA_HEADS
GLA_GATE_RANK = 16
GLA_GATE_NORMALIZER = 16.0
N_EXPERTS = 32
TOP_K = 4
SWIGLU_ALPHA = 1.702
SWIGLU_LIMIT = 7.0
RMS_EPS = 1e-5
LOG2E = 1.4426950408889634

LANES = 128
SUBLANES = 8
BF16_SUBLANES = 16
VMEM_LIMIT = 48 * 1024 * 1024
EXPERT_VMEM_LIMIT = 56 * 1024 * 1024
MOBA_VMEM_LIMIT = 58 * 1024 * 1024

INPROJ_TM = 1024
INPROJ_TN = 1664
MOBA_GROUP = 4
MOBA_STEP_HEADS = 2
MOBA_SUM_ROWS = BF16_SUBLANES
GLA_CHUNK = 64
GLA_STEP_CHUNKS = 4
GLA_SUB = 16
GLA_DECAY_TM = 512
GLA_DECAY_TRI = 256
GLA_SAFE_EXPONENT = 80.0
MERGE_TM = 512
EXPERT_ROWS = 256
COMBINE_TM = 256
COMBINE_ROW_CHUNK = 32
MASK_NEG = -1e30
DMA_ISSUE_UNROLL = 4
DMA_WAIT_UNROLL = 16
DMA_PRIORITIES = 2

N_MAIN = 3 * D_MODEL + 2 * GLA_KEY_DIM + 2 * GLA_VAL_DIM
COL_GATES = N_MAIN
COL_GLOW = N_MAIN + 2 * D_MODEL
N_PROJ = COL_GLOW + LANES


def _dot(a, b, **kw):
    return jnp.dot(a, b, preferred_element_type=F32, **kw)


def _dot_nt(a, b, **kw):
    return lax.dot_general(a, b, (((1,), (1,)), ((), ())), preferred_element_type=F32, **kw)


def _truncate_to_bf16_bits(x, bitcast):
    return bitcast(bitcast(x, jnp.uint32) & jnp.uint32(0xFFFF0000), F32)


def _first_max_lane(g, lane_f):
    mx = jnp.max(g, axis=1, keepdims=True)
    return jnp.min(jnp.where(g == mx, lane_f, float(LANES)), axis=1, keepdims=True)


def _cparams(*sem):
    return pltpu.CompilerParams(dimension_semantics=sem, vmem_limit_bytes=VMEM_LIMIT)


def _mod_kernel(c_ref, w_ref, b_ref, o_ref):
    c = c_ref[...]
    ca = c * jax.nn.sigmoid(c)
    o_ref[...] = _dot(ca, w_ref[...], precision=HIGHEST) + b_ref[...]


def _modulation(c, w_ada, b_ada):
    B = c.shape[0]
    n = w_ada.shape[1]
    c8 = jnp.zeros((SUBLANES, D_MODEL), F32).at[:B].set(c)
    out = pl.pallas_call(
        _mod_kernel,
        out_shape=jax.ShapeDtypeStruct((SUBLANES, n), F32),
        grid=(n // D_MODEL,),
        in_specs=[
            pl.BlockSpec((SUBLANES, D_MODEL), lambda j: (0, 0)),
            pl.BlockSpec((D_MODEL, D_MODEL), lambda j: (0, j)),
            pl.BlockSpec((1, D_MODEL), lambda j: (0, j)),
        ],
        out_specs=pl.BlockSpec((SUBLANES, D_MODEL), lambda j: (0, j)),
        compiler_params=_cparams("arbitrary"),
        name="adaln_mod",
    )(c8, w_ada, b_ada.reshape(1, n))
    return out[:B].reshape(B, 1, n)


def _inproj_kernel(x_ref, nw_ref, sc_ref, sh_ref, w_ref, o_ref, h_ref):
    @pl.when(pl.program_id(1) == 0)
    def _():
        x = x_ref[...]
        ms = jnp.mean(x * x, axis=-1, keepdims=True)
        y = x * lax.rsqrt(ms + RMS_EPS) * nw_ref[...]
        h_ref[...] = (y * (1.0 + sc_ref[...]) + sh_ref[...]).astype(BF16)

    o_ref[...] = _dot(h_ref[...], w_ref[...])


def _input_projection(x2, norm_w, mod, w_proj, S):
    T = x2.shape[0]
    tiles_per_batch = S // INPROJ_TM
    return pl.pallas_call(
        _inproj_kernel,
        out_shape=jax.ShapeDtypeStruct((T, N_PROJ), F32),
        grid=(T // INPROJ_TM, N_PROJ // INPROJ_TN),
        in_specs=[
            pl.BlockSpec((INPROJ_TM, D_MODEL), lambda i, j: (i, 0)),
            pl.BlockSpec((1, D_MODEL), lambda i, j: (0, 0)),
            pl.BlockSpec((None, 1, D_MODEL), lambda i, j: (i // tiles_per_batch, 0, 1)),
            pl.BlockSpec((None, 1, D_MODEL), lambda i, j: (i // tiles_per_batch, 0, 0)),
            pl.BlockSpec((D_MODEL, INPROJ_TN), lambda i, j: (0, j)),
        ],
        out_specs=pl.BlockSpec((INPROJ_TM, INPROJ_TN), lambda i, j: (i, j)),
        scratch_shapes=[pltpu.VMEM((INPROJ_TM, D_MODEL), BF16)],
        compiler_params=_cparams("parallel", "arbitrary"),
        name="inproj",
    )(x2, norm_w.reshape(1, D_MODEL), mod, mod, w_proj)


def _rope_tables(S):
    half = ROPE_DIM // 2
    inv_freq = jnp.float32(ROPE_THETA) ** (-jnp.arange(half, dtype=jnp.float32) * 2.0 / ROPE_DIM)
    ang = jnp.arange(S, dtype=jnp.float32)[:, None] * inv_freq[None, :]
    cos, sin = jnp.cos(ang), jnp.sin(ang)
    ones = jnp.ones((S, MOBA_HEAD_DIM - ROPE_DIM), F32)
    zeros_hi = jnp.zeros((S, MOBA_HEAD_DIM - half), F32)
    zeros_lo = jnp.zeros((S, half), F32)
    cos_f = jnp.concatenate([cos, cos, ones], axis=1)
    sin_a = jnp.concatenate([-sin, zeros_hi], axis=1)
    sin_b = jnp.concatenate([zeros_lo, sin, ones * 0.0], axis=1)
    return cos_f, sin_a, sin_b


def _rope(x, cos_f, sin_a, sin_b):
    half = ROPE_DIM // 2
    up = pltpu.roll(x, MOBA_HEAD_DIM - half, 1)
    dn = pltpu.roll(x, half, 1)
    return x * cos_f + up * sin_a + dn * sin_b


def _kprep_kernel(k_ref, v_ref, cos_ref, sa_ref, sb_ref, kaug_ref, vb_ref, kmean_ref):
    n = pl.program_id(1)
    cos_f, sin_a, sin_b = cos_ref[...], sa_ref[...], sb_ref[...]
    lane = lax.broadcasted_iota(jnp.int32, (MOBA_BLOCK, LANES), 1)
    onehot = jnp.where(lane == n, 1.0, 0.0).astype(BF16)
    for h in range(MOBA_HEADS):
        cols = slice(h * MOBA_HEAD_DIM, (h + 1) * MOBA_HEAD_DIM)
        kr = _rope(k_ref[:, cols], cos_f, sin_a, sin_b)
        kmean_ref[:, cols] = jnp.mean(kr, axis=0, keepdims=True)
        kaug_ref[h, :, :MOBA_HEAD_DIM] = kr.astype(BF16)
        kaug_ref[h, :, MOBA_HEAD_DIM:] = onehot
        vb_ref[h, :MOBA_HEAD_DIM, :] = v_ref[:, cols].T.astype(BF16)
        vb_ref[h, MOBA_HEAD_DIM:, :] = jnp.ones((MOBA_SUM_ROWS, MOBA_BLOCK), BF16)


def _moba_prep(proj, tables, B, S):
    NB = S // MOBA_BLOCK
    NBP = NB
    src = lambda n: n
    tab_spec = pl.BlockSpec((MOBA_BLOCK, LANES), lambda b, n: (src(n), 0))
    return pl.pallas_call(
        _kprep_kernel,
        out_shape=(
            jax.ShapeDtypeStruct((B, MOBA_HEADS, NBP * MOBA_BLOCK, 2 * MOBA_HEAD_DIM), BF16),
            jax.ShapeDtypeStruct((B, MOBA_HEADS, MOBA_HEAD_DIM + MOBA_SUM_ROWS, NBP * MOBA_BLOCK), BF16),
            jax.ShapeDtypeStruct((B, NBP, 1, D_MODEL), F32),
        ),
        grid=(B, NBP),
        in_specs=[
            pl.BlockSpec((MOBA_BLOCK, D_MODEL), lambda b, n: (b * NB + src(n), 1)),
            pl.BlockSpec((MOBA_BLOCK, D_MODEL), lambda b, n: (b * NB + src(n), 2)),
            tab_spec, tab_spec, tab_spec,
        ],
        out_specs=(
            pl.BlockSpec((None, MOBA_HEADS, MOBA_BLOCK, 2 * MOBA_HEAD_DIM), lambda b, n: (b, 0, n, 0)),
            pl.BlockSpec((None, MOBA_HEADS, MOBA_HEAD_DIM + MOBA_SUM_ROWS, MOBA_BLOCK), lambda b, n: (b, 0, 0, n)),
            pl.BlockSpec((None, None, 1, D_MODEL), lambda b, n: (b, n, 0, 0)),
        ),
        compiler_params=_cparams("parallel", "parallel"),
        name="moba_kprep",
    )(proj, proj, *tables)


def _qprep_kernel(q_ref, cos_ref, sa_ref, sb_ref, kmean_ref, qaug_ref):
    i = pl.program_id(1)
    blk = MOBA_BLOCK
    scale = MOBA_HEAD_DIM ** -0.5
    cos_f, sin_a, sin_b = cos_ref[...], sa_ref[...], sb_ref[...]
    nbp = kmean_ref.shape[1]
    blk_id = lax.broadcasted_iota(jnp.int32, (nbp, blk), 0)
    blk_f = blk_id.astype(F32)
    neg_inf = jnp.float32(-jnp.inf)
    for h in range(MOBA_HEADS):
        cols = slice(h * MOBA_HEAD_DIM, (h + 1) * MOBA_HEAD_DIM)
        qr = _rope(q_ref[:, cols], cos_f, sin_a, sin_b)
        gate = _dot_nt(kmean_ref[h], qr, precision=HIGHEST)
        g = jnp.where(blk_id < i, gate, neg_inf)
        bias = jnp.where(blk_id == i, 0.0, MASK_NEG)
        for r in range(MOBA_TOPK):
            mx = jnp.max(g, axis=0, keepdims=True)
            first = jnp.min(jnp.where(g == mx, blk_f, float(nbp)), axis=0, keepdims=True)
            pick = blk_f == first
            unmask = jnp.where(i > r, 0.0, MASK_NEG)
            bias = jnp.where(pick, jnp.maximum(bias, unmask), bias)
            g = jnp.where(pick, neg_inf, g)
        qaug_ref[h, :MOBA_HEAD_DIM, :] = (qr * (scale * LOG2E)).T.astype(BF16)
        qaug_ref[h, MOBA_HEAD_DIM:MOBA_HEAD_DIM + nbp, :] = bias.astype(BF16)
        qaug_ref[h, MOBA_HEAD_DIM + nbp:, :] = jnp.zeros((MOBA_HEAD_DIM - nbp, blk), BF16)


def _moba_qprep(proj, tables, kmean, B, S):
    NB = S // MOBA_BLOCK
    tab_spec = pl.BlockSpec((MOBA_BLOCK, LANES), lambda b, i: (i, 0))
    return pl.pallas_call(
        _qprep_kernel,
        out_shape=jax.ShapeDtypeStruct((B, MOBA_HEADS, 2 * MOBA_HEAD_DIM, S), BF16),
        grid=(B, NB),
        in_specs=[
            pl.BlockSpec((MOBA_BLOCK, D_MODEL), lambda b, i: (b * NB + i, 0)),
            tab_spec, tab_spec, tab_spec,
            pl.BlockSpec((None, MOBA_HEADS, kmean.shape[2], MOBA_HEAD_DIM), lambda b, i: (b, 0, 0, 0)),
        ],
        out_specs=pl.BlockSpec((None, MOBA_HEADS, 2 * MOBA_HEAD_DIM, MOBA_BLOCK), lambda b, i: (b, 0, 0, i)),
        compiler_params=_cparams("parallel", "parallel"),
        name="moba_qprep",
    )(proj, *tables, kmean)


def _moba_kernel(q_ref, kaug_ref, v_ref, o_ref, m_ref, acc_ref, s_ref, mx_ref):
    n_full = pl.program_id(2)
    width = MOBA_GROUP * MOBA_BLOCK
    heads = range(MOBA_STEP_HEADS)

    def scores(h, g):
        return _dot(kaug_ref[h, pl.ds(pl.multiple_of(g * width, width), width), :], q_ref[h])

    def prob_value(h, s, m, g):
        p = jnp.exp2(s - m).astype(BF16)
        return _dot(v_ref[h, :, pl.ds(pl.multiple_of(g * width, width), width)], p)

    def stage(h, g):
        s = scores(h, g)
        s_ref[h] = s
        mx_ref[h] = jnp.max(s, axis=0, keepdims=True)

    key = lax.broadcasted_iota(jnp.int32, (width, width), 0)
    qry = lax.broadcasted_iota(jnp.int32, (width, width), 1)
    for h in heads:
        s_own = scores(h, n_full)
        stage(h, 0)
        s_own = jnp.where(key > qry, jnp.float32(-jnp.inf), s_own)
        m0 = jnp.max(s_own, axis=0, keepdims=True)
        m_ref[h] = m0
        acc_ref[h] = prob_value(h, s_own, m0, n_full)

    def body(g, carry):
        nxt = jnp.minimum(g + 1, jnp.maximum(n_full - 1, 0))
        for h in heads:
            m_prev = m_ref[h]
            m_new = jnp.maximum(m_prev, mx_ref[h])
            alpha = jnp.exp2(m_prev - m_new)
            m_ref[h] = m_new
            pv = prob_value(h, s_ref[h], m_new, g)
            stage(h, nxt)
            acc_ref[h] = alpha * acc_ref[h] + pv
        return carry

    lax.fori_loop(0, n_full, body, 0)
    for h in heads:
        acc = acc_ref[h]
        out_t = acc[:MOBA_HEAD_DIM, :] / acc[MOBA_HEAD_DIM:MOBA_HEAD_DIM + 1, :]
        o_ref[:, h * MOBA_HEAD_DIM:(h + 1) * MOBA_HEAD_DIM] = out_t.T.astype(o_ref.dtype)


def _moba_attention(qaug, kaug, vb, B, S):
    T = B * S
    tq = MOBA_GROUP * MOBA_BLOCK
    hs = MOBA_STEP_HEADS
    assert S % tq == 0 and MOBA_HEADS % hs == 0
    NG = S // tq
    return pl.pallas_call(
        _moba_kernel,
        out_shape=jax.ShapeDtypeStruct((T, D_MODEL), BF16),
        grid=(B, MOBA_HEADS // hs, NG),
        in_specs=[
            pl.BlockSpec((None, hs, 2 * MOBA_HEAD_DIM, tq), lambda b, h, j: (b, h, 0, j)),
            pl.BlockSpec((None, hs, S, 2 * MOBA_HEAD_DIM), lambda b, h, j: (b, h, 0, 0)),
            pl.BlockSpec((None, hs, MOBA_HEAD_DIM + MOBA_SUM_ROWS, S), lambda b, h, j: (b, h, 0, 0)),
        ],
        out_specs=pl.BlockSpec((tq, hs * MOBA_HEAD_DIM), lambda b, h, j: (b * NG + j, h)),
        scratch_shapes=[
            pltpu.VMEM((hs, 1, tq), F32),
            pltpu.VMEM((hs, MOBA_HEAD_DIM + MOBA_SUM_ROWS, tq), F32),
            pltpu.VMEM((hs, tq, tq), F32),
            pltpu.VMEM((hs, 1, tq), F32),
        ],
        compiler_params=pltpu.CompilerParams(
            dimension_semantics=("parallel", "parallel", "arbitrary"), vmem_limit_bytes=MOBA_VMEM_LIMIT),
        name="moba_attn",
    )(qaug, kaug, vb)


def _gla_decay_kernel(gl_ref, gw_ref, gb_ref, b_ref, peak_ref):
    tm, C = GLA_DECAY_TM, GLA_CHUNK
    z = _dot(gl_ref[...].astype(BF16), gw_ref[...].astype(BF16)) + gb_ref[...]
    log_g = jax.nn.log_sigmoid(z) / GLA_GATE_NORMALIZER
    tt = GLA_DECAY_TRI
    r_i = lax.broadcasted_iota(jnp.int32, (tt, tt), 0)
    c_i = lax.broadcasted_iota(jnp.int32, (tt, tt), 1)
    tri = jnp.where((c_i <= r_i) & (c_i >= (r_i // C) * C), 1.0, 0.0).astype(BF16)
    hi = _truncate_to_bf16_bits(log_g, pltpu.bitcast)
    rest = log_g - hi
    mid = _truncate_to_bf16_bits(rest, pltpu.bitcast)
    lo = rest - mid
    for r0 in range(0, tm, tt):
        rows = slice(r0, r0 + tt)
        b_ref[rows, :] = ((_dot(tri, hi[rows].astype(BF16)) + _dot(tri, mid[rows].astype(BF16)))
                          + _dot(tri, lo[rows].astype(BF16)))
    steps = -log_g
    for c in range(tm // C):
        pk = jnp.max(jnp.max(steps[c * C:(c + 1) * C, :], axis=1, keepdims=True), axis=0, keepdims=True)
        peak_ref[c:c + 1, :] = jnp.broadcast_to(pk, (1, LANES))


def _gla_decay(proj, gate_w, gate_b):
    T = proj.shape[0]
    tm = GLA_DECAY_TM
    gw = jnp.zeros((LANES, GLA_KEY_DIM), F32).at[:GLA_GATE_RANK].set(gate_w)
    return pl.pallas_call(
        _gla_decay_kernel,
        out_shape=(
            jax.ShapeDtypeStruct((T, GLA_KEY_DIM), F32),
            jax.ShapeDtypeStruct((T // GLA_CHUNK, LANES), F32),
        ),
        grid=(T // tm,),
        in_specs=[
            pl.BlockSpec((tm, LANES), lambda i: (i, COL_GLOW // LANES)),
            pl.BlockSpec((LANES, GLA_KEY_DIM), lambda i: (0, 0)),
            pl.BlockSpec((1, GLA_KEY_DIM), lambda i: (0, 0)),
        ],
        out_specs=(
            pl.BlockSpec((tm, GLA_KEY_DIM), lambda i: (i, 0)),
            pl.BlockSpec((tm // GLA_CHUNK, LANES), lambda i: (i, 0)),
        ),
        compiler_params=_cparams("parallel"),
        name="gla_decay",
    )(proj, gw, gate_b.reshape(1, GLA_KEY_DIM))


def _gla_kernel(flag_ref, q_ref, k_ref, v_ref, gr_ref, b_ref, nw_ref, o_ref, state_ref):
    C, SUB = GLA_CHUNK, GLA_SUB
    nsub = C // SUB

    @pl.when(pl.program_id(1) == 0)
    def _():
        state_ref[...] = jnp.zeros_like(state_ref)

    r_i = lax.broadcasted_iota(jnp.int32, (C, C), 0)
    c_i = lax.broadcasted_iota(jnp.int32, (C, C), 1)

    row_c = lax.broadcasted_iota(jnp.int32, (C, GLA_DK), 0)
    row_s = lax.broadcasted_iota(jnp.int32, (SUB, GLA_DK), 0)
    lane_s = lax.broadcasted_iota(jnp.int32, (SUB, C), 1)
    neg_inf = jnp.float32(-jnp.inf)
    nw = nw_ref[...]

    def scores_bounded(q, k, b):
        s_rows = []
        for I in range(nsub):
            rs = slice(I * SUB, (I + 1) * SUB)
            ref_b = b[I * SUB:I * SUB + 1, :]
            qs = q[rs] * jnp.exp(b[rs] - ref_b)
            ks = k * jnp.exp(jnp.where(row_c < (I + 1) * SUB, ref_b - b, neg_inf))
            s_rows.append(_dot_nt(qs.astype(BF16), ks.astype(BF16)))
        return jnp.where(c_i <= r_i, jnp.concatenate(s_rows, axis=0), 0.0)

    def scores_any(q, k, b):
        s_rows = []
        for I in range(nsub):
            rs = slice(I * SUB, (I + 1) * SUB)
            q_i, k_i, b_i = q[rs], k[rs], b[rs]
            if I == 0:
                s_i = jnp.zeros((SUB, C), F32)
            else:
                ref_b = b[I * SUB - 1:I * SUB, :]
                qs = q_i * jnp.exp(b_i - ref_b)
                ks = k * jnp.exp(jnp.where(row_c < I * SUB, ref_b - b, neg_inf))
                s_i = _dot_nt(qs.astype(BF16), ks.astype(BF16))
            for j in range(SUB):
                diff = jnp.where(row_s >= j, b_i - b_i[j:j + 1, :], neg_inf)
                colv = jnp.sum(q_i * k_i[j:j + 1, :] * jnp.exp(diff), axis=1, keepdims=True)
                s_i = jnp.where(lane_s == I * SUB + j, colv, s_i)
            s_rows.append(s_i)
        return jnp.concatenate(s_rows, axis=0)

    def all_heads(scores):
        for h in range(GLA_HEADS):
            kc = slice(h * GLA_DK, (h + 1) * GLA_DK)
            vc = slice(h * GLA_DV, (h + 1) * GLA_DV)
            st = state_ref[h]
            for cc in range(GLA_STEP_CHUNKS):
                rows = slice(cc * C, (cc + 1) * C)
                b = b_ref[rows, kc]
                q = q_ref[rows, kc] * (GLA_DK ** -0.5)
                k = k_ref[rows, kc]
                v = v_ref[rows, vc]
                b_last = b[C - 1:C, :]

                o = _dot_nt((q * jnp.exp(b)).astype(BF16), st.astype(BF16))
                o = o + _dot(scores(q, k, b).astype(BF16), v.astype(BF16))

                kd = k * jnp.exp(b_last - b)
                st = st * jnp.exp(b_last) + _dot(v.T.astype(BF16), kd.astype(BF16))

                ms = jnp.mean(o * o, axis=-1, keepdims=True)
                y = o * lax.rsqrt(ms + RMS_EPS) * nw
                g = gr_ref[rows, vc]
                o_ref[rows, vc] = (y * (g * jax.nn.sigmoid(g))).astype(o_ref.dtype)
            state_ref[h] = st

    bounded = flag_ref[pl.program_id(0) * pl.num_programs(1) + pl.program_id(1)] != 0
    lax.cond(bounded, lambda: all_heads(scores_bounded), lambda: all_heads(scores_any))


def _gla(proj, gate_w, gate_b, norm_w, B, S):
    T = B * S
    step_rows = GLA_STEP_CHUNKS * GLA_CHUNK
    NC = S // step_rows
    b_all, peak = _gla_decay(proj, gate_w, gate_b)
    step_peak = jnp.max(peak[:, 0].reshape(-1, GLA_STEP_CHUNKS), axis=1)
    bounded = ((GLA_SUB - 1) * step_peak < GLA_SAFE_EXPONENT).astype(jnp.int32)
    q_blk = (3 * D_MODEL) // GLA_KEY_DIM
    v_blk = (3 * D_MODEL + 2 * GLA_KEY_DIM) // GLA_VAL_DIM
    rows = lambda width, col: pl.BlockSpec((step_rows, width), lambda b, c, flags: (b * NC + c, col))
    return pl.pallas_call(
        _gla_kernel,
        out_shape=jax.ShapeDtypeStruct((T, GLA_VAL_DIM), BF16),
        grid_spec=pltpu.PrefetchScalarGridSpec(
            num_scalar_prefetch=1,
            grid=(B, NC),
            in_specs=[
                rows(GLA_KEY_DIM, q_blk), rows(GLA_KEY_DIM, q_blk + 1),
                rows(GLA_VAL_DIM, v_blk), rows(GLA_VAL_DIM, v_blk + 1),
                rows(GLA_KEY_DIM, 0),
                pl.BlockSpec((1, GLA_DV), lambda b, c, flags: (0, 0)),
            ],
            out_specs=rows(GLA_VAL_DIM, 0),
            scratch_shapes=[pltpu.VMEM((GLA_HEADS, GLA_DV, GLA_DK), F32)],
        ),
        compiler_params=_cparams("parallel", "arbitrary"),
        name="gla",
    )(bounded, proj, proj, proj, proj, b_all, norm_w.reshape(1, GLA_DV))


def _merge_kernel(oa_ref, ob_ref, ga_ref, gb_ref, x_ref, wa_ref, wb_ref, wo_ref, bma_ref, bmb_ref,
                  g1_ref, sc2_ref, sh2_ref, nw_ref, wrh_ref, wrl_ref, br_ref,
                  x1_ref, h2_ref, e4_ref, w4_ref, p4_ref, cnt_ref, run_ref):
    tm = MERGE_TM

    @pl.when(pl.program_id(0) == 0)
    def _():
        run_ref[...] = jnp.zeros_like(run_ref)

    y_a = _dot(oa_ref[...], wa_ref[...])
    y_b = _dot(ob_ref[...], wb_ref[...])
    g_a = jax.nn.sigmoid(ga_ref[...] + bma_ref[...])
    g_b = jax.nn.sigmoid(gb_ref[...] + bmb_ref[...])
    mix = _dot((g_a * y_a + g_b * y_b).astype(BF16), wo_ref[...])
    x1 = x_ref[...] + g1_ref[...] * mix
    x1_ref[...] = x1

    ms = jnp.mean(x1 * x1, axis=-1, keepdims=True)
    h2 = x1 * lax.rsqrt(ms + RMS_EPS) * nw_ref[...] * (1.0 + sc2_ref[...]) + sh2_ref[...]
    for s in range(D_MODEL // LANES):
        h2_ref[pl.ds(s, tm, stride=SUBLANES), :] = h2[:, s * LANES:(s + 1) * LANES]

    h_top = _truncate_to_bf16_bits(h2, pltpu.bitcast)
    h_hi = h_top.astype(BF16)
    h_lo = (h2 - h_top).astype(BF16)
    logits = (_dot(h_hi, wrh_ref[...]) + (_dot(h_lo, wrh_ref[...]) + _dot(h_hi, wrl_ref[...]))) + br_ref[...]
    lane = lax.broadcasted_iota(jnp.int32, (tm, LANES), 1)
    lane_f = lane.astype(F32)
    neg_inf = jnp.float32(-jnp.inf)
    g = jnp.where(lane < N_EXPERTS, logits, neg_inf)
    picks, tops = [], []
    for _ in range(TOP_K):
        mx = jnp.max(g, axis=1, keepdims=True)
        idx = _first_max_lane(g, lane_f)
        pick = lane_f == idx
        picks.append((pick, idx))
        tops.append(mx)
        g = jnp.where(pick, neg_inf, g)
    ex = [jnp.exp(t - tops[0]) for t in tops]
    denom = ex[0] + ex[1] + ex[2] + ex[3]

    sel = sum(jnp.where(pick, 1.0, 0.0) for pick, _ in picks)
    r_i = lax.broadcasted_iota(jnp.int32, (tm, tm), 0)
    c_i = lax.broadcasted_iota(jnp.int32, (tm, tm), 1)
    lower = jnp.where(c_i < r_i, 1.0, 0.0).astype(BF16)
    rank = run_ref[0:1, :] + _dot(lower, sel.astype(BF16))
    run_new = run_ref[0:1, :] + jnp.sum(sel, axis=0, keepdims=True)
    run_ref[...] = jnp.broadcast_to(run_new, run_ref.shape)
    cnt_ref[...] = jnp.broadcast_to(run_new, cnt_ref.shape)

    e4 = jnp.zeros((tm, LANES), jnp.int32)
    w4 = jnp.zeros((tm, LANES), F32)
    p4 = jnp.zeros((tm, LANES), F32)
    for r in range(TOP_K):
        pick, idx = picks[r]
        pos = jnp.sum(jnp.where(pick, rank, 0.0), axis=1, keepdims=True)
        e4 = jnp.where(lane == r, idx.astype(jnp.int32), e4)
        w4 = jnp.where(lane == r, ex[r] / denom, w4)
        p4 = jnp.where(lane == r, pos, p4)
    e4_ref[...] = e4
    w4_ref[...] = w4
    p4_ref[...] = p4


def _merge_and_route(o_a, o_b, proj, x2, w_a, w_b, w_o, b_merge, mod, norm2_w, w_router, b_router, S):
    T = x2.shape[0]
    tm = MERGE_TM
    tiles_per_batch = S // tm
    full = lambda shape: pl.BlockSpec(shape, lambda i: tuple(0 for _ in shape))
    row = lambda width, col: pl.BlockSpec((tm, width), lambda i: (i, col))
    modv = lambda k: pl.BlockSpec((None, 1, D_MODEL), lambda i: (i // tiles_per_batch, 0, k))
    wr = jnp.zeros((D_MODEL, LANES), F32).at[:, :N_EXPERTS].set(w_router)
    wr_top = _truncate_to_bf16_bits(wr, lax.bitcast_convert_type)
    wr_hi = wr_top.astype(BF16)
    wr_lo = (wr - wr_top).astype(BF16)
    br = jnp.zeros((1, LANES), F32).at[0, :N_EXPERTS].set(b_router)
    bm = b_merge.reshape(1, 2 * D_MODEL)
    return pl.pallas_call(
        _merge_kernel,
        out_shape=(
            jax.ShapeDtypeStruct((T, D_MODEL), F32),
            jax.ShapeDtypeStruct((T * SUBLANES, LANES), F32),
            jax.ShapeDtypeStruct((T, LANES), jnp.int32),
            jax.ShapeDtypeStruct((T, LANES), F32),
            jax.ShapeDtypeStruct((T, LANES), F32),
            jax.ShapeDtypeStruct((SUBLANES, LANES), F32),
        ),
        grid=(T // tm,),
        in_specs=[
            row(D_MODEL, 0), row(D_MODEL, 0),
            row(D_MODEL, COL_GATES // D_MODEL), row(D_MODEL, COL_GATES // D_MODEL + 1),
            row(D_MODEL, 0),
            full((D_MODEL, D_MODEL)), full((D_MODEL, D_MODEL)), full((D_MODEL, D_MODEL)),
            pl.BlockSpec((1, D_MODEL), lambda i: (0, 0)), pl.BlockSpec((1, D_MODEL), lambda i: (0, 1)),
            modv(2), modv(4), modv(3),
            full((1, D_MODEL)), full((D_MODEL, LANES)), full((D_MODEL, LANES)), full((1, LANES)),
        ],
        out_specs=(
            row(D_MODEL, 0),
            pl.BlockSpec((tm * SUBLANES, LANES), lambda i: (i, 0)),
            row(LANES, 0), row(LANES, 0), row(LANES, 0),
            full((SUBLANES, LANES)),
        ),
        scratch_shapes=[pltpu.VMEM((SUBLANES, LANES), F32)],
        compiler_params=_cparams("arbitrary"),
        name="merge_route",
    )(o_a, o_b, proj, proj, x2, w_a, w_b, w_o, bm, bm, mod, mod, mod,
      norm2_w.reshape(1, D_MODEL), wr_hi, wr_lo, br)


def _row_slab(ref, row):
    return ref.at[pl.ds(pl.multiple_of(row * SUBLANES, SUBLANES), SUBLANES), :]


def _wait_slabs(src_ref, dst_ref, sem, count):
    def wait(a, c):
        pltpu.make_async_copy(_row_slab(src_ref, 0), _row_slab(dst_ref, 0), sem).wait()
        return c

    lax.fori_loop(0, count, wait, 0, unroll=DMA_WAIT_UNROLL)


def _dispatch_kernel(dest_ref, pad_lo_ref, pad_hi_ref, nu_ref, h_ref, xs_ref, zero_ref, sem, pad_sem, blk_sem):
    tm = COMBINE_TM
    base = pl.program_id(0) * (tm * TOP_K)
    blk_rows = EXPERT_ROWS * SUBLANES
    n_blocks = xs_ref.shape[0] // blk_rows

    def block_copy(j):
        dst = xs_ref.at[pl.ds(pl.multiple_of(j * blk_rows, blk_rows), blk_rows), :]
        return pltpu.make_async_copy(zero_ref, dst, blk_sem)

    def pad_copies(act):
        for e in range(N_EXPERTS):
            lo = pad_lo_ref[e]
            n = pad_hi_ref[e] - lo
            piece = EXPERT_ROWS // 2
            while piece >= 1:
                @pl.when((n & piece) != 0)
                def _(piece=piece, lo=lo, n=n):
                    first = lo + (n & ~(2 * piece - 1))
                    dst = xs_ref.at[pl.ds(pl.multiple_of(first * SUBLANES, SUBLANES), piece * SUBLANES), :]
                    act(pltpu.make_async_copy(zero_ref.at[pl.ds(0, piece * SUBLANES), :], dst, pad_sem))
                piece //= 2

    @pl.when(pl.program_id(0) == 0)
    def _():
        zero_ref[...] = jnp.zeros_like(zero_ref)

        def fill_block(j, c):
            block_copy(j).start()
            return c

        pad_copies(lambda cp: cp.start())
        lax.fori_loop(nu_ref[0], n_blocks, fill_block, 0)

    @pl.when(pl.program_id(0) == pl.num_programs(0) - 1)
    def _():
        def drain_block(j, c):
            block_copy(j).wait()
            return c

        pad_copies(lambda cp: cp.wait())
        lax.fori_loop(nu_ref[0], n_blocks, drain_block, 0)

    def start(t, c):
        src = _row_slab(h_ref, t)
        for kk in range(TOP_K):
            pltpu.make_async_copy(src, _row_slab(xs_ref, dest_ref[base + t * TOP_K + kk]), sem).start(
                priority=kk % DMA_PRIORITIES)
        return c

    lax.fori_loop(0, tm, start, 0, unroll=DMA_ISSUE_UNROLL)
    _wait_slabs(h_ref, xs_ref, sem, tm * TOP_K)


def _dispatch(dest, pad_lo, pad_hi, n_used, h2_slabs, P):
    T = h2_slabs.shape[0] // SUBLANES
    tm = COMBINE_TM
    return pl.pallas_call(
        _dispatch_kernel,
        out_shape=jax.ShapeDtypeStruct((P * SUBLANES, LANES), F32),
        grid_spec=pltpu.PrefetchScalarGridSpec(
            num_scalar_prefetch=4,
            grid=(T // tm,),
            in_specs=[pl.BlockSpec((tm * SUBLANES, LANES), lambda i, *_: (i, 0))],
            out_specs=pl.BlockSpec(memory_space=pl.ANY),
            scratch_shapes=[
                pltpu.VMEM((EXPERT_ROWS * SUBLANES, LANES), F32),
                pltpu.SemaphoreType.DMA,
                pltpu.SemaphoreType.DMA,
                pltpu.SemaphoreType.DMA,
            ],
        ),
        compiler_params=_cparams("arbitrary"),
        name="moe_dispatch",
    )(dest, pad_lo, pad_hi, n_used, h2_slabs)


def _expert_kernel(be_ref, nu_ref, slot_ref, nxt_ref, xs_ref, wi_hbm, bi_ref, wo_hbm, bo_ref, perm_ref, ys_ref,
                   x_ref, wi16_ref, wo16_ref, wi32_ref, wo32_ref, wi_sem, wo_sem):
    R = EXPERT_ROWS
    j = pl.program_id(0)
    used = j < nu_ref[0]
    new_expert = (j == 0) | (be_ref[j] != be_ref[jnp.maximum(j - 1, 0)])

    def weight_copies(e, s):
        return (pltpu.make_async_copy(wi_hbm.at[e], wi32_ref.at[s], wi_sem.at[s]),
                pltpu.make_async_copy(wo_hbm.at[e], wo32_ref.at[s], wo_sem.at[s]))

    @pl.when(used & (j == 0))
    def _():
        for cp in weight_copies(be_ref[0], slot_ref[0]):
            cp.start()

    @pl.when(used & new_expert)
    def _():
        s = slot_ref[j]
        for cp in weight_copies(be_ref[j], s):
            cp.wait()

        @pl.when(nxt_ref[j] >= 0)
        def _():
            for cp in weight_copies(nxt_ref[j], 1 - s):
                cp.start()

        perm = perm_ref[...]
        for g in range(D_MODEL // LANES):
            w = wi32_ref[s, :, 2 * LANES * g:2 * LANES * (g + 1)].astype(BF16)
            sep = _dot(w, perm).astype(BF16)
            wi16_ref[:, LANES * g:LANES * (g + 1)] = sep[:, :LANES]
            wi16_ref[:, D_MODEL + LANES * g:D_MODEL + LANES * (g + 1)] = sep[:, LANES:]
        wo16_ref[...] = wo32_ref[s].astype(BF16)

    @pl.when(used)
    def _():
        for s in range(D_MODEL // LANES):
            x_ref[:, s * LANES:(s + 1) * LANES] = xs_ref[pl.ds(s, R, stride=SUBLANES), :].astype(BF16)
        hid = _dot(x_ref[...], wi16_ref[...]) + bi_ref[...]
        glu = jnp.minimum(hid[:, :D_MODEL], SWIGLU_LIMIT)
        lin = jnp.clip(hid[:, D_MODEL:], -SWIGLU_LIMIT, SWIGLU_LIMIT)
        act = glu * jax.nn.sigmoid(SWIGLU_ALPHA * glu) * (lin + 1.0)
        out = _dot(act.astype(BF16), wo16_ref[...]) + bo_ref[...]
        for s in range(D_MODEL // LANES):
            ys_ref[pl.ds(s, R, stride=SUBLANES), :] = out[:, s * LANES:(s + 1) * LANES]

    @pl.when(j >= nu_ref[0])
    def _():
        ys_ref[...] = jnp.zeros_like(ys_ref)


def _experts(block_expert, n_used, group_slot, next_expert, xs, w_in, b_in, w_out, b_out):
    R = EXPERT_ROWS
    n_blocks = xs.shape[0] // (R * SUBLANES)
    src = jnp.arange(2 * LANES)[:, None]
    dst = jnp.arange(2 * LANES)[None, :]
    perm = jnp.where(src == jnp.where(dst < LANES, 2 * dst, 2 * (dst - LANES) + 1), 1.0, 0.0).astype(BF16)
    return pl.pallas_call(
        _expert_kernel,
        out_shape=jax.ShapeDtypeStruct(xs.shape, F32),
        grid_spec=pltpu.PrefetchScalarGridSpec(
            num_scalar_prefetch=4,
            grid=(n_blocks,),
            in_specs=[
                pl.BlockSpec((R * SUBLANES, LANES),
                             lambda j, be, nu, *_: (jnp.minimum(j, jnp.maximum(nu[0] - 1, 0)), 0)),
                pl.BlockSpec(memory_space=pl.ANY),
                pl.BlockSpec((None, 1, 2 * D_MODEL), lambda j, be, *_: (be[j], 0, 0)),
                pl.BlockSpec(memory_space=pl.ANY),
                pl.BlockSpec((None, 1, D_MODEL), lambda j, be, *_: (be[j], 0, 0)),
                pl.BlockSpec((2 * LANES, 2 * LANES), lambda j, *_: (0, 0)),
            ],
            out_specs=pl.BlockSpec((R * SUBLANES, LANES), lambda j, *_: (j, 0)),
            scratch_shapes=[
                pltpu.VMEM((R, D_MODEL), BF16),
                pltpu.VMEM((D_MODEL, 2 * D_MODEL), BF16),
                pltpu.VMEM((D_MODEL, D_MODEL), BF16),
                pltpu.VMEM((2, D_MODEL, 2 * D_MODEL), F32),
                pltpu.VMEM((2, D_MODEL, D_MODEL), F32),
                pltpu.SemaphoreType.DMA((2,)),
                pltpu.SemaphoreType.DMA((2,)),
            ],
        ),
        compiler_params=pltpu.CompilerParams(
            dimension_semantics=("arbitrary",), vmem_limit_bytes=EXPERT_VMEM_LIMIT),
        name="moe_experts",
    )(block_expert, n_used, group_slot, next_expert, xs, w_in, b_in, w_out, b_out, perm)


def _combine_kernel(dest_ref, ys_ref, w4_ref, x1_ref, g2_ref, nw_ref, o_ref, buf_ref, sem):
    tm = COMBINE_TM
    step = pl.program_id(0)
    slot = step % 2

    def gather(tile, into):
        base = tile * (tm * TOP_K)

        def start(t, c):
            for kk in range(TOP_K):
                pltpu.make_async_copy(_row_slab(ys_ref, dest_ref[base + t * TOP_K + kk]),
                                      _row_slab(buf_ref.at[into, kk], t),
                                      sem.at[into]).start(priority=kk % DMA_PRIORITIES)
            return c

        lax.fori_loop(0, tm, start, 0, unroll=DMA_ISSUE_UNROLL)

    @pl.when(step == 0)
    def _():
        gather(0, 0)

    @pl.when(step + 1 < pl.num_programs(0))
    def _():
        gather(step + 1, 1 - slot)

    _wait_slabs(ys_ref, buf_ref.at[slot, 0], sem.at[slot], tm * TOP_K)

    g2 = g2_ref[...]
    nw = nw_ref[...]
    rc = COMBINE_ROW_CHUNK
    for r0 in range(0, tm, rc):
        wk = [jnp.broadcast_to(w4_ref[r0:r0 + rc, kk:kk + 1], (rc, LANES)) for kk in range(TOP_K)]
        parts = []
        ssq = jnp.zeros((rc, 1), F32)
        for s in range(D_MODEL // LANES):
            cols = slice(s * LANES, (s + 1) * LANES)
            y = jnp.zeros((rc, LANES), F32)
            for kk in range(TOP_K):
                y = y + buf_ref[slot, kk, pl.ds(r0 * SUBLANES + s, rc, stride=SUBLANES), :] * wk[kk]
            x2 = x1_ref[r0:r0 + rc, cols] + g2[:, cols] * y
            ssq = ssq + jnp.sum(x2 * x2, axis=1, keepdims=True)
            parts.append(x2)
        inv = lax.rsqrt(ssq / D_MODEL + RMS_EPS)
        for s in range(D_MODEL // LANES):
            cols = slice(s * LANES, (s + 1) * LANES)
            o_ref[r0:r0 + rc, cols] = parts[s] * inv * nw[:, cols]


def _combine(dest, ys, w4, x1, mod, final_w, S):
    T = x1.shape[0]
    tm = COMBINE_TM
    tiles_per_batch = S // tm
    return pl.pallas_call(
        _combine_kernel,
        out_shape=jax.ShapeDtypeStruct((T, D_MODEL), F32),
        grid_spec=pltpu.PrefetchScalarGridSpec(
            num_scalar_prefetch=1,
            grid=(T // tm,),
            in_specs=[
                pl.BlockSpec(memory_space=pl.ANY),
                pl.BlockSpec((tm, LANES), lambda i, dest: (i, 0)),
                pl.BlockSpec((tm, D_MODEL), lambda i, dest: (i, 0)),
                pl.BlockSpec((None, 1, D_MODEL), lambda i, dest: (i // tiles_per_batch, 0, 5)),
                pl.BlockSpec((1, D_MODEL), lambda i, dest: (0, 0)),
            ],
            out_specs=pl.BlockSpec((tm, D_MODEL), lambda i, dest: (i, 0)),
            scratch_shapes=[
                pltpu.VMEM((2, TOP_K, tm * SUBLANES, LANES), F32),
                pltpu.SemaphoreType.DMA((2,)),
            ],
        ),
        compiler_params=_cparams("arbitrary"),
        name="moe_combine",
    )(dest, ys, w4, x1, mod, final_w.reshape(1, D_MODEL))


def _layer(x2, c_mod, B, S, norm1_w, w_in, gla_gate_w, gla_gate_b, gla_norm_w, w_o_moba, w_o_gla,
           b_merge, w_out, norm2_w, w_router, b_router, w_exp_in, b_exp_in, w_exp_out, b_exp_out):
    T = B * S
    glow_lo = N_MAIN
    glow_hi = N_MAIN + GLA_GATE_RANK
    w_in16 = w_in.astype(BF16)
    w_proj = jnp.concatenate(
        [w_in16[:, :glow_lo], w_in16[:, glow_hi:], w_in16[:, glow_lo:glow_hi],
         jnp.zeros((D_MODEL, LANES - GLA_GATE_RANK), BF16)], axis=1)
    proj = _input_projection(x2, norm1_w, c_mod, w_proj, S)

    tables = _rope_tables(S)
    kaug, vb, kmean = _moba_prep(proj, tables, B, S)
    NB = S // MOBA_BLOCK
    kmean = kmean[:, :NB].reshape(B, NB, MOBA_HEADS, MOBA_HEAD_DIM).transpose(0, 2, 1, 3)
    kmean = jnp.pad(kmean, ((0, 0), (0, 0), (0, -NB % BF16_SUBLANES), (0, 0)))
    qaug = _moba_qprep(proj, tables, kmean, B, S)
    o_a = _moba_attention(qaug, kaug, vb, B, S)
    o_b = _gla(proj, gla_gate_w, gla_gate_b, gla_norm_w, B, S)

    x1, h2, e4, w4, p4, cnt = _merge_and_route(
        o_a, o_b, proj, x2, w_o_moba.astype(BF16), w_o_gla.astype(BF16), w_out.astype(BF16),
        b_merge, c_mod, norm2_w, w_router, b_router, S)

    R = EXPERT_ROWS
    counts = cnt[0, :N_EXPERTS].astype(jnp.int32)
    padded = (counts + R - 1) // R * R
    padded_end = jnp.cumsum(padded)
    padded_start = padded_end - padded
    n_blocks = -(-(T * TOP_K + N_EXPERTS * (R - 1)) // R)
    P = n_blocks * R
    experts = jnp.arange(N_EXPERTS, dtype=jnp.int32)

    def lookup(table, idx):
        return jnp.sum(jnp.where(idx[..., None] == experts, table, 0), axis=-1)

    dest = (lookup(padded_start, e4[:, :TOP_K]) + p4[:, :TOP_K].astype(jnp.int32)).reshape(-1)
    block_row0 = jnp.arange(n_blocks, dtype=jnp.int32) * R
    block_expert = jnp.minimum(
        jnp.sum(padded_end[None, :] <= block_row0[:, None], axis=1), N_EXPERTS - 1).astype(jnp.int32)
    n_used = (padded_end[-1:] // R).astype(jnp.int32)
    nonempty = counts > 0
    group_of_expert = jnp.cumsum(nonempty.astype(jnp.int32)) - 1
    later = jnp.where((experts[None, :] > experts[:, None]) & nonempty[None, :], experts[None, :], N_EXPERTS)
    next_of_expert = jnp.min(later, axis=1)
    next_of_expert = jnp.where(next_of_expert < N_EXPERTS, next_of_expert, -1)
    group_slot = (lookup(group_of_expert, block_expert) % 2).astype(jnp.int32)
    next_expert = lookup(next_of_expert, block_expert).astype(jnp.int32)

    xs = _dispatch(dest, padded_start + counts, padded_end, n_used, h2, P)
    b_ei = jnp.concatenate([b_exp_in[:, 0::2], b_exp_in[:, 1::2]], axis=-1)[:, None, :]
    ys = _experts(block_expert, n_used, group_slot, next_expert, xs, w_exp_in, b_ei, w_exp_out,
                  b_exp_out[:, None, :])
    return x1, ys, dest, w4


def kernel(x, c, w_ada, b_ada, norm1_w, w_in, gla_gate_w, gla_gate_b, gla_norm_w, w_o_moba, w_o_gla, b_merge, w_out, norm2_w, w_router, b_router, w_exp_in, b_exp_in, w_exp_out, b_exp_out, final_norm_w):
    B, S, _ = x.shape
    depth = w_ada.shape[0]
    assert depth == 1, "the combine kernel fuses the final norm, so a single layer is supported"
    x2 = x.reshape(B * S, D_MODEL)
    l = 0
    mod = _modulation(c, w_ada[l], b_ada[l])
    x1, ys, dest, w4 = _layer(
        x2, mod, B, S, norm1_w[l], w_in[l], gla_gate_w[l], gla_gate_b[l], gla_norm_w[l],
        w_o_moba[l], w_o_gla[l], b_merge[l], w_out[l], norm2_w[l], w_router[l], b_router[l],
        w_exp_in[l], b_exp_in[l], w_exp_out[l], b_exp_out[l])
    out = _combine(dest, ys, w4, x1, mod, final_norm_w, S)
    return out.reshape(B, S, D_MODEL)
```
